```python
import math
import jax
import jax.numpy as jnp
from jax import lax
import numpy as np

D_MODEL = 1024
BATCH = 8
SEQ = 8192
DEPTH = 2

N_BRANCHES = 4
BRANCH_WIDTH = D_MODEL // 4
RMS_EPS = 1e-6
CONV_WIDTH = 4

GLA_HEADS = 4
GLA_DV = BRANCH_WIDTH // GLA_HEADS
GLA_DK = GLA_DV // 2
GLA_RANK = 16
GLA_GATE_NORM = 16.0
GLA_CHUNK = 64
GLA_QK_W = GLA_HEADS * GLA_DK
GLA_V_W = GLA_HEADS * GLA_DV

LRU_WIDTH = BRANCH_WIDTH
LRU_BLOCKS = 4
LRU_C = 8.0

DIFF_HEADS = 4
DIFF_DV = BRANCH_WIDTH // DIFF_HEADS
DIFF_DH = DIFF_DV // 2
DIFF_QK_W = DIFF_HEADS * 2 * DIFF_DH
DIFF_V_W = DIFF_HEADS * DIFF_DV
Q_BLOCK = 128

GDN_HEADS = 4
GDN_DK = BRANCH_WIDTH // GDN_HEADS
GDN_DV = BRANCH_WIDTH // GDN_HEADS
GDN_CHUNK = 64
GDN_QK_W = GDN_HEADS * GDN_DK
GDN_V_W = GDN_HEADS * GDN_DV

IN_SPLITS = (GLA_QK_W, GLA_QK_W, GLA_V_W, GLA_V_W, 2 * GLA_RANK,
             LRU_WIDTH, LRU_WIDTH,
             DIFF_QK_W, DIFF_QK_W, DIFF_V_W,
             2 * GDN_QK_W + GDN_V_W, GDN_V_W, 4 * GDN_HEADS)
D_IN = sum(IN_SPLITS)

MEM_LEN = 256
XATTN_HEADS = 4
XATTN_DH = D_MODEL // XATTN_HEADS

N_GROUPS = 4
EXPERTS_PER_GROUP = 8
N_EXPERTS = N_GROUPS * EXPERTS_PER_GROUP
TOP_K = 2
D_EXPERT = D_MODEL // 2
MOE_BLOCK = 128

kernel_name = 'hybrid_bidir_gla_rglru_diffattn_gdn_hmoe'


def rmsnorm(x, g):
    xf = x.astype(jnp.float32)
    y = xf * lax.rsqrt(jnp.mean(xf * xf, axis=-1, keepdims=True) + RMS_EPS)
    return (y * g.astype(jnp.float32)).astype(x.dtype)


def l2norm(t):
    return t * lax.rsqrt(jnp.sum(t * t, axis=-1, keepdims=True) + 1e-6)


def _flip(t):
    return jnp.flip(t, axis=1)


def alibi_slopes(n):
    return jnp.exp2(-8.0 * jnp.arange(1, n + 1, dtype=jnp.float32) / n)


def dwconv_centred(x, w):
    k = w.shape[0]
    return lax.conv_general_dilated(
        x, w[:, None, :].astype(x.dtype), window_strides=(1,),
        padding=[(k // 2, k - 1 - k // 2)],
        dimension_numbers=('NWC', 'WIO', 'NWC'),
        feature_group_count=x.shape[-1])


def _to_chunks(t, chunk):
    b, s, h = t.shape[:3]
    t = t.reshape((b, s // chunk, chunk, h) + t.shape[3:])
    return jnp.moveaxis(t, 3, 1)


def _from_chunks(o):
    nc, b, h, c, d = o.shape
    return o.transpose(1, 0, 3, 2, 4).reshape(b, nc * c, h, d)


def _linrec_combine(c1, c2):
    a1, b1 = c1
    a2, b2 = c2
    return a1 * a2, a2 * b1 + b2


def gla_scan(q, k, v, log_f):
    b_, s_, h_, dk = q.shape
    dv = v.shape[-1]
    c = GLA_CHUNK
    incl = jnp.tril(jnp.ones((c, c), dtype=bool))[:, :, None]

    def step(state, inp):
        q_i, k_i, v_i, g_i = inp
        bcum = jnp.cumsum(g_i, axis=2)
        decay = jnp.exp(jnp.where(incl, bcum[:, :, :, None, :] - bcum[:, :, None, :, :], -jnp.inf))
        scores = jnp.einsum('bhtd,bhsd,bhtsd->bhts', q_i, k_i, decay)
        o = (jnp.einsum('bhts,bhse->bhte', scores, v_i)
             + jnp.einsum('bhtd,bhde->bhte', q_i * jnp.exp(bcum), state))
        b_end = bcum[:, :, -1:, :]
        state = (jnp.exp(b_end)[:, :, 0, :, None] * state
                 + jnp.einsum('bhsd,bhse->bhde', k_i * jnp.exp(b_end - bcum), v_i))
        return state, o

    xs = tuple(jnp.moveaxis(_to_chunks(t, c), 2, 0) for t in (q, k, v, log_f))
    _, o = lax.scan(step, jnp.zeros((b_, h_, dk, dv), jnp.float32), xs)
    return _from_chunks(o)


def gla_mixer(q, k, v, og, lr, lr_up, lr_bias, norm_g):
    B, S, _ = q.shape
    f32 = jnp.float32
    qh = q.astype(f32).reshape(B, S, GLA_HEADS, GLA_DK) * GLA_DK ** -0.5
    kh = k.astype(f32).reshape(B, S, GLA_HEADS, GLA_DK)
    vh = v.astype(f32).reshape(B, S, GLA_HEADS, GLA_DV)
    lr = lr.astype(f32).reshape(B, S, 2, GLA_RANK)
    z = jnp.einsum('bsdr,drk->bsdk', lr, lr_up.astype(f32)) + lr_bias.astype(f32)
    log_f = (jax.nn.log_sigmoid(z) / GLA_GATE_NORM).reshape(B, S, 2, GLA_HEADS, GLA_DK)
    o_f = gla_scan(qh, kh, vh, log_f[:, :, 0])
    o_b = _flip(gla_scan(_flip(qh), _flip(kh), _flip(vh), _flip(log_f[:, :, 1])))
    o = rmsnorm(o_f + o_b, norm_g) * jax.nn.silu(og.astype(f32)).reshape(B, S, GLA_HEADS, GLA_DV)
    return o.reshape(B, S, GLA_V_W).astype(q.dtype)


def rglru_mixer(xb, gb, conv_w, conv_b, w_a, b_a, w_x, b_x, lam):
    B, S, W = xb.shape
    f32 = jnp.float32
    u = (dwconv_centred(xb, conv_w) + conv_b).astype(f32)
    ub = u.reshape(B, S, LRU_BLOCKS, W // LRU_BLOCKS)

    def gate(w, b):
        return jax.nn.sigmoid(jnp.einsum('bsni,nio->bsno', ub, w.astype(f32)).reshape(B, S, W) + b.astype(f32))

    def direction(d, reverse):
        r = gate(w_a[d], b_a[d])
        i = gate(w_x[d], b_x[d])
        log_a = -LRU_C * r * jax.nn.softplus(-lam[d].astype(f32))
        a = jnp.exp(log_a)
        xin = jnp.sqrt(-jnp.expm1(2.0 * log_a)) * (i * u)
        _, hs = lax.associative_scan(_linrec_combine, (a, xin), axis=1, reverse=reverse)
        return hs

    hsum = direction(0, False) + direction(1, True)
    return (hsum * jax.nn.gelu(gb.astype(f32))).astype(xb.dtype)


def diff_attention(q, k, v, lq1, lk1, lq2, lk2, norm_g, lambda_init):
    B, S, _ = q.shape
    f32 = jnp.float32
    nb = S // Q_BLOCK
    qh = (q.astype(f32).reshape(B, nb, Q_BLOCK, DIFF_HEADS, 2, DIFF_DH)
          .transpose(1, 0, 3, 4, 2, 5) * DIFF_DH ** -0.5)
    kt = k.astype(f32).reshape(B, S, DIFF_HEADS, 2, DIFF_DH).transpose(0, 2, 3, 1, 4)
    vt = v.astype(f32).reshape(B, S, DIFF_HEADS, DIFF_DV).transpose(0, 2, 1, 3)
    lam = (jnp.exp(jnp.sum(lq1.astype(f32) * lk1.astype(f32)))
           - jnp.exp(jnp.sum(lq2.astype(f32) * lk2.astype(f32))) + lambda_init)
    slopes = alibi_slopes(DIFF_HEADS)
    k_pos = jnp.arange(S, dtype=jnp.int32)

    def block(args):
        q_blk, start = args
        q_pos = start + jnp.arange(Q_BLOCK, dtype=jnp.int32)
        dist = jnp.abs(q_pos[:, None] - k_pos[None, :]).astype(f32)
        bias = -slopes[:, None, None] * dist
        s = jnp.einsum('bhmqd,bhmkd->bhmqk', q_blk, kt) + bias[None, :, None]
        p = jax.nn.softmax(s, axis=-1)
        attn = p[:, :, 0] - lam * p[:, :, 1]
        return jnp.einsum('bhqk,bhkd->bhqd', attn, vt)

    o = lax.map(block, (qh, jnp.arange(nb, dtype=jnp.int32) * Q_BLOCK))
    o = o.transpose(1, 0, 3, 2, 4).reshape(B, S, DIFF_HEADS, DIFF_DV)
    o = rmsnorm(o, norm_g) * (1.0 - lambda_init)
    return o.reshape(B, S, DIFF_V_W).astype(q.dtype)


def gdn_scan(q, k, v, beta, log_a):
    b_, s_, h_, dk = q.shape
    dv = v.shape[-1]
    c = GDN_CHUNK
    qc, kc, vc, bc, gc = (_to_chunks(t, c) for t in (q, k, v, beta, log_a))
    gam = jnp.cumsum(gc, axis=-1)
    diff = gam[..., :, None] - gam[..., None, :]
    incl = jnp.tril(jnp.ones((c, c), dtype=bool))
    strict = jnp.tril(jnp.ones((c, c), dtype=bool), k=-1)
    decay = jnp.exp(jnp.where(incl, diff, -jnp.inf))
    kk = jnp.einsum('bhntd,bhnsd->bhnts', kc, kc)
    a_mat = jnp.where(strict, bc[..., None] * kk * decay, 0.0) + jnp.eye(c, dtype=jnp.float32)
    rhs = jnp.concatenate([vc * bc[..., None], kc * (bc * jnp.exp(gam))[..., None]], axis=-1)
    sol = lax.linalg.triangular_solve(a_mat, rhs, left_side=True, lower=True, unit_diagonal=True)
    u_val, k_cum = sol[..., :dv], sol[..., dv:]
    qk = jnp.einsum('bhntd,bhnsd->bhnts', qc, kc) * decay
    q_dec = qc * jnp.exp(gam)[..., None]
    k_dec = kc * jnp.exp(gam[..., -1:] - gam)[..., None]
    c_dec = jnp.exp(gam[..., -1])

    def step(state, inp):
        u_i, kc_i, qk_i, qd_i, kd_i, cd_i = inp
        v_new = u_i - jnp.einsum('bhtd,bhde->bhte', kc_i, state)
        o = jnp.einsum('bhtd,bhde->bhte', qd_i, state) + jnp.einsum('bhts,bhse->bhte', qk_i, v_new)
        state = cd_i[..., None, None] * state + jnp.einsum('bhsd,bhse->bhde', kd_i, v_new)
        return state, o

    xs = tuple(jnp.moveaxis(t, 2, 0) for t in (u_val, k_cum, qk, q_dec, k_dec, c_dec))
    _, o = lax.scan(step, jnp.zeros((b_, h_, dk, dv), jnp.float32), xs)
    return _from_chunks(o)


def gdn_mixer(qkv, z, ba, conv_w, a_log, dt_bias, norm_g):
    B, S, _ = qkv.shape
    f32 = jnp.float32
    qkv = jax.nn.silu(dwconv_centred(qkv, conv_w)).astype(f32)
    q, k, v = jnp.split(qkv, [GDN_QK_W, 2 * GDN_QK_W], axis=-1)
    q = l2norm(q.reshape(B, S, GDN_HEADS, GDN_DK)) * GDN_DK ** -0.5
    k = l2norm(k.reshape(B, S, GDN_HEADS, GDN_DK))
    v = v.reshape(B, S, GDN_HEADS, GDN_DV)
    ba = ba.astype(f32).reshape(B, S, 4, GDN_HEADS)
    beta = jax.nn.sigmoid(ba[:, :, :2])
    log_a = -jnp.exp(a_log.astype(f32)) * jax.nn.softplus(ba[:, :, 2:] + dt_bias.astype(f32))
    o_f = gdn_scan(q, k, v, beta[:, :, 0], log_a[:, :, 0])
    o_b = _flip(gdn_scan(_flip(q), _flip(k), _flip(v), _flip(beta[:, :, 1]), _flip(log_a[:, :, 1])))
    o = rmsnorm(o_f + o_b, norm_g) * jax.nn.silu(z.astype(f32)).reshape(B, S, GDN_HEADS, GDN_DV)
    return o.reshape(B, S, GDN_V_W).astype(z.dtype)


def gated_merge(h, branches, merge_w, merge_b, branch_up):
    out = None
    for i, y in enumerate(branches):
        term = jax.nn.sigmoid(h @ merge_w[i] + merge_b[i]) * (y @ branch_up[i])
        out = term if out is None else out + term
    return out


def memory_cross_attention(h, mem_n, w_q, w_kv, w_o):
    B, S, D = h.shape
    M = mem_n.shape[1]
    q = (h @ w_q).reshape(B, S, XATTN_HEADS, XATTN_DH)
    kv = (mem_n @ w_kv).reshape(B, M, 2, XATTN_HEADS, XATTN_DH)
    k, v = kv[:, :, 0], kv[:, :, 1]
    s = jnp.einsum('bshd,bmhd->bhsm', q, k).astype(jnp.float32) * XATTN_DH ** -0.5
    p = jax.nn.softmax(s, axis=-1)
    o = jnp.einsum('bhsm,bmhd->bshd', p.astype(v.dtype), v).reshape(B, S, D)
    return o @ w_o


def hier_moe(h, w_group, b_group, w_expert, b_expert, w1, w3, w2):
    B, S, D = h.shape
    T = B * S
    f32 = jnp.float32
    ht = h.reshape(T, D)
    g_logits = (ht @ w_group).astype(f32) + b_group.astype(f32)
    g_prob = jax.nn.softmax(g_logits, axis=-1)
    _, g_idx = lax.top_k(g_logits, 1)
    g_w = jnp.take_along_axis(g_prob, g_idx, axis=1)
    e_logits = ((ht @ w_expert).astype(f32) + b_expert.astype(f32)).reshape(T, N_GROUPS, EXPERTS_PER_GROUP)
    e_logits = jnp.take_along_axis(
        e_logits, jnp.broadcast_to(g_idx[:, :, None], (T, 1, EXPERTS_PER_GROUP)), axis=1)[:, 0]
    e_top, e_idx = lax.top_k(e_logits, TOP_K)
    gate = jax.nn.softmax(e_top, axis=-1) * g_w
    eid = (g_idx * EXPERTS_PER_GROUP + e_idx).reshape(-1)
    tok = jnp.repeat(jnp.arange(T, dtype=jnp.int32), TOP_K)
    wts = gate.reshape(-1)
    A = T * TOP_K
    n_blk = -(-A // MOE_BLOCK) + N_EXPERTS
    P = n_blk * MOE_BLOCK
    order = jnp.argsort(eid)
    eid_s, tok_s, w_s = eid[order], tok[order], wts[order]
    counts = jnp.zeros((N_EXPERTS,), jnp.int32).at[eid].add(1)
    padded = (counts + MOE_BLOCK - 1) // MOE_BLOCK * MOE_BLOCK
    pad_end = jnp.cumsum(padded)
    pad_start = pad_end - padded
    seg_start = jnp.cumsum(counts) - counts
    dest = pad_start[eid_s] + jnp.arange(A, dtype=jnp.int32) - seg_start[eid_s]
    row_tok = jnp.full((P,), T, jnp.int32).at[dest].set(tok_s)
    row_w = jnp.zeros((P,), f32).at[dest].set(w_s)
    blk_expert = jnp.minimum(
        jnp.searchsorted(pad_end, jnp.arange(n_blk, dtype=jnp.int32) * MOE_BLOCK, side='right'),
        N_EXPERTS - 1)
    h_pad = jnp.concatenate([ht, jnp.zeros((1, D), ht.dtype)], axis=0)
    xb = h_pad[row_tok].reshape(n_blk, MOE_BLOCK, D)

    def expert_block(args):
        xi, e = args
        return (jax.nn.silu(xi @ w1[e]) * (xi @ w3[e])) @ w2[e]

    yb = lax.map(expert_block, (xb, blk_expert)).reshape(P, D)
    out = jnp.zeros((T + 1, D), h.dtype).at[row_tok].add(yb * row_w[:, None].astype(yb.dtype))
    return out[:T].reshape(B, S, D)


def setup_inputs(seed: int = 0) -> dict:
    key = jax.random.key(seed)
    ks = iter(jax.random.split(key, 64))
    L = DEPTH
    f32 = jnp.float32

    def nrm(shape, fan_in):
        return jax.random.normal(next(ks), shape, f32) * fan_in ** -0.5

    def gain(shape):
        return 1.0 + 0.02 * jax.random.normal(next(ks), shape, f32)

    def small(shape, s=0.02):
        return s * jax.random.normal(next(ks), shape, f32)

    x = jax.random.normal(next(ks), (BATCH, SEQ, D_MODEL), f32)
    mem = jax.random.normal(next(ks), (BATCH, MEM_LEN, D_MODEL), f32)
    a0 = jax.random.uniform(next(ks), (L, 2, LRU_WIDTH), f32, minval=0.9, maxval=0.999) ** (1.0 / LRU_C)
    lru_lambda = jnp.log(a0) - jnp.log1p(-a0)
    gdn_a_log = jnp.log(jax.random.uniform(next(ks), (L, 2, GDN_HEADS), f32, minval=1.0, maxval=16.0))
    dt = jnp.exp(jax.random.uniform(next(ks), (L, 2, GDN_HEADS), f32,
                                    minval=math.log(1e-3), maxval=math.log(1e-1)))
    gdn_dt_bias = dt + jnp.log(-jnp.expm1(-dt))
    blk = LRU_WIDTH // LRU_BLOCKS
    return {
        'x': x,
        'mem': mem,
        'mix_norm': gain((L, D_MODEL)),
        'w_in': nrm((L, D_MODEL, D_IN), D_MODEL),
        'gla_lr_up': nrm((L, 2, GLA_RANK, GLA_QK_W), GLA_RANK),
        'gla_lr_bias': small((L, 2, GLA_QK_W), 0.1),
        'gla_norm': gain((L, GLA_DV)),
        'lru_conv_w': nrm((L, CONV_WIDTH, LRU_WIDTH), CONV_WIDTH),
        'lru_conv_b': small((L, LRU_WIDTH)),
        'lru_w_a': nrm((L, 2, LRU_BLOCKS, blk, blk), blk),
        'lru_b_a': small((L, 2, LRU_WIDTH)),
        'lru_w_x': nrm((L, 2, LRU_BLOCKS, blk, blk), blk),
        'lru_b_x': small((L, 2, LRU_WIDTH)),
        'lru_lambda': lru_lambda,
        'diff_lq1': small((L, DIFF_DH), 0.1),
        'diff_lk1': small((L, DIFF_DH), 0.1),
        'diff_lq2': small((L, DIFF_DH), 0.1),
        'diff_lk2': small((L, DIFF_DH), 0.1),
        'diff_norm': gain((L, DIFF_DV)),
        'gdn_conv_w': nrm((L, CONV_WIDTH, 2 * GDN_QK_W + GDN_V_W), CONV_WIDTH),
        'gdn_a_log': gdn_a_log,
        'gdn_dt_bias': gdn_dt_bias,
        'gdn_norm': gain((L, GDN_DV)),
        'merge_w': nrm((L, N_BRANCHES, D_MODEL, D_MODEL), D_MODEL),
        'merge_b': small((L, N_BRANCHES, D_MODEL)),
        'branch_up': nrm((L, N_BRANCHES, BRANCH_WIDTH, D_MODEL), BRANCH_WIDTH),
        'mix_out': nrm((L, D_MODEL, D_MODEL), D_MODEL),
        'xattn_norm': gain((L, D_MODEL)),
        'mem_norm': gain((L, D_MODEL)),
        'xattn_wq': nrm((L, D_MODEL, D_MODEL), D_MODEL),
        'xattn_wkv': nrm((L, D_MODEL, 2 * D_MODEL), D_MODEL),
        'xattn_wo': nrm((L, D_MODEL, D_MODEL), D_MODEL),
        'moe_norm': gain((L, D_MODEL)),
        'moe_w_group': nrm((L, D_MODEL, N_GROUPS), D_MODEL),
        'moe_b_group': small((L, N_GROUPS), 0.01),
        'moe_w_expert': nrm((L, D_MODEL, N_EXPERTS), D_MODEL),
        'moe_b_expert': small((L, N_EXPERTS), 0.01),
        'moe_w1': nrm((L, N_EXPERTS, D_MODEL, D_EXPERT), D_MODEL),
        'moe_w3': nrm((L, N_EXPERTS, D_MODEL, D_EXPERT), D_MODEL),
        'moe_w2': nrm((L, N_EXPERTS, D_EXPERT, D_MODEL), D_EXPERT),
        'final_norm': gain((D_MODEL,)),
    }


def reference(x, mem, mix_norm, w_in, gla_lr_up, gla_lr_bias, gla_norm,
              lru_conv_w, lru_conv_b, lru_w_a, lru_b_a, lru_w_x, lru_b_x, lru_lambda,
              diff_lq1, diff_lk1, diff_lq2, diff_lk2, diff_norm,
              gdn_conv_w, gdn_a_log, gdn_dt_bias, gdn_norm,
              merge_w, merge_b, branch_up, mix_out,
              xattn_norm, mem_norm, xattn_wq, xattn_wkv, xattn_wo,
              moe_norm, moe_w_group, moe_b_group, moe_w_expert, moe_b_expert,
              moe_w1, moe_w3, moe_w2, final_norm):
    split_at = np.cumsum(IN_SPLITS)[:-1].tolist()
    for l in range(DEPTH):
        h = rmsnorm(x, mix_norm[l])
        parts = jnp.split(h @ w_in[l], split_at, axis=-1)
        gq, gk, gv, gog, glr, lx, lg, dq, dk, dv, nqkv, nz, nba = parts
        y_gla = gla_mixer(gq, gk, gv, gog, glr, gla_lr_up[l], gla_lr_bias[l], gla_norm[l])
        y_lru = rglru_mixer(lx, lg, lru_conv_w[l], lru_conv_b[l], lru_w_a[l], lru_b_a[l],
                            lru_w_x[l], lru_b_x[l], lru_lambda[l])
        y_diff = diff_attention(dq, dk, dv, diff_lq1[l], diff_lk1[l], diff_lq2[l], diff_lk2[l],
                                diff_norm[l], 0.8 - 0.6 * math.exp(-0.3 * l))
        y_gdn = gdn_mixer(nqkv, nz, nba, gdn_conv_w[l], gdn_a_log[l], gdn_dt_bias[l], gdn_norm[l])
        merged = gated_merge(h, (y_gla, y_lru, y_diff, y_gdn), merge_w[l], merge_b[l], branch_up[l])
        x = x + merged @ mix_out[l]
        h = rmsnorm(x, xattn_norm[l])
        x = x + memory_cross_attention(h, rmsnorm(mem, mem_norm[l]), xattn_wq[l], xattn_wkv[l], xattn_wo[l])
        h = rmsnorm(x, moe_norm[l])
        x = x + hier_moe(h, moe_w_group[l], moe_b_group[l], moe_w_expert[l], moe_b_expert[l],
                         moe_w1[l], moe_w3[l], moe_w2[l])
    return rmsnorm(x, final_norm)
```

```python
import functools
import math

import jax
import jax.numpy as jnp
import numpy as np
from jax import lax
from jax.experimental import pallas as pl
from jax.experimental.pallas import tpu as pltpu

D_MODEL = 1024
DEPTH = 2
N_BRANCHES = 4
BRANCH_WIDTH = D_MODEL // 4
RMS_EPS = 1e-6
CONV_WIDTH = 4

GLA_HEADS = 4
GLA_DV = BRANCH_WIDTH // GLA_HEADS
GLA_DK = GLA_DV // 2
GLA_RANK = 16
GLA_GATE_NORM = 16.0
GLA_CHUNK = 64
GLA_QK_W = GLA_HEADS * GLA_DK
GLA_V_W = GLA_HEADS * GLA_DV

LRU_WIDTH = BRANCH_WIDTH
LRU_BLOCKS = 4
LRU_C = 8.0

DIFF_HEADS = 4
DIFF_DV = BRANCH_WIDTH // DIFF_HEADS
DIFF_DH = DIFF_DV // 2
DIFF_QK_W = DIFF_HEADS * 2 * DIFF_DH
DIFF_V_W = DIFF_HEADS * DIFF_DV

GDN_HEADS = 4
GDN_DK = BRANCH_WIDTH // GDN_HEADS
GDN_DV = BRANCH_WIDTH // GDN_HEADS
GDN_CHUNK = 64
GDN_QK_W = GDN_HEADS * GDN_DK
GDN_V_W = GDN_HEADS * GDN_DV

IN_SPLITS = (GLA_QK_W, GLA_QK_W, GLA_V_W, GLA_V_W, 2 * GLA_RANK,
             LRU_WIDTH, LRU_WIDTH,
             DIFF_QK_W, DIFF_QK_W, DIFF_V_W,
             2 * GDN_QK_W + GDN_V_W, GDN_V_W, 4 * GDN_HEADS)

XATTN_HEADS = 4
XATTN_DH = D_MODEL // XATTN_HEADS

N_GROUPS = 4
EXPERTS_PER_GROUP = 8
N_EXPERTS = N_GROUPS * EXPERTS_PER_GROUP
TOP_K = 2
D_EXPERT = D_MODEL // 2
MOE_BLOCK = 128

LANES = 128
VMEM_LIMIT = 56 * 1024 * 1024

F32 = jnp.float32
BF16 = jnp.bfloat16
NEG_BIG = -1e30


def rmsnorm(x, g):
    xf = x.astype(jnp.float32)
    y = xf * lax.rsqrt(jnp.mean(xf * xf, axis=-1, keepdims=True) + RMS_EPS)
    return (y * g.astype(jnp.float32)).astype(x.dtype)


def l2norm(t):
    return t * lax.rsqrt(jnp.sum(t * t, axis=-1, keepdims=True) + 1e-6)


def _flip(t):
    return jnp.flip(t, axis=1)


def dwconv_centred(x, w):
    k = w.shape[0]
    return lax.conv_general_dilated(
        x, w[:, None, :].astype(x.dtype), window_strides=(1,),
        padding=[(k // 2, k - 1 - k // 2)],
        dimension_numbers=('NWC', 'WIO', 'NWC'),
        feature_group_count=x.shape[-1])


def _to_chunks(t, chunk):
    b, s, h = t.shape[:3]
    t = t.reshape((b, s // chunk, chunk, h) + t.shape[3:])
    return jnp.moveaxis(t, 3, 1)


def _from_chunks(o):
    nc, b, h, c, d = o.shape
    return o.transpose(1, 0, 3, 2, 4).reshape(b, nc * c, h, d)


def _linrec_combine(c1, c2):
    a1, b1 = c1
    a2, b2 = c2
    return a1 * a2, a2 * b1 + b2


def gla_scan(q, k, v, log_f):
    b_, s_, h_, dk = q.shape
    dv = v.shape[-1]
    c = GLA_CHUNK
    incl = jnp.tril(jnp.ones((c, c), dtype=bool))[:, :, None]

    def step(state, inp):
        q_i, k_i, v_i, g_i = inp
        bcum = jnp.cumsum(g_i, axis=2)
        decay = jnp.exp(jnp.where(incl, bcum[:, :, :, None, :] - bcum[:, :, None, :, :], -jnp.inf))
        scores = jnp.einsum('bhtd,bhsd,bhtsd->bhts', q_i, k_i, decay)
        o = (jnp.einsum('bhts,bhse->bhte', scores, v_i)
             + jnp.einsum('bhtd,bhde->bhte', q_i * jnp.exp(bcum), state))
        b_end = bcum[:, :, -1:, :]
        state = (jnp.exp(b_end)[:, :, 0, :, None] * state
                 + jnp.einsum('bhsd,bhse->bhde', k_i * jnp.exp(b_end - bcum), v_i))
        return state, o

    xs = tuple(jnp.moveaxis(_to_chunks(t, c), 2, 0) for t in (q, k, v, log_f))
    _, o = lax.scan(step, jnp.zeros((b_, h_, dk, dv), jnp.float32), xs)
    return _from_chunks(o)


def gla_mixer(q, k, v, og, lr, lr_up, lr_bias, norm_g):
    B, S, _ = q.shape
    f32 = jnp.float32
    qh = q.astype(f32).reshape(B, S, GLA_HEADS, GLA_DK) * GLA_DK ** -0.5
    kh = k.astype(f32).reshape(B, S, GLA_HEADS, GLA_DK)
    vh = v.astype(f32).reshape(B, S, GLA_HEADS, GLA_DV)
    lr = lr.astype(f32).reshape(B, S, 2, GLA_RANK)
    z = jnp.einsum('bsdr,drk->bsdk', lr, lr_up.astype(f32)) + lr_bias.astype(f32)
    log_f = (jax.nn.log_sigmoid(z) / GLA_GATE_NORM).reshape(B, S, 2, GLA_HEADS, GLA_DK)
    o_f = gla_scan(qh, kh, vh, log_f[:, :, 0])
    o_b = _flip(gla_scan(_flip(qh), _flip(kh), _flip(vh), _flip(log_f[:, :, 1])))
    o = rmsnorm(o_f + o_b, norm_g) * jax.nn.silu(og.astype(f32)).reshape(B, S, GLA_HEADS, GLA_DV)
    return o.reshape(B, S, GLA_V_W).astype(q.dtype)


def rglru_mixer(xb, gb, conv_w, conv_b, w_a, b_a, w_x, b_x, lam):
    B, S, W = xb.shape
    f32 = jnp.float32
    u = (dwconv_centred(xb, conv_w) + conv_b).astype(f32)
    ub = u.reshape(B, S, LRU_BLOCKS, W // LRU_BLOCKS)

    def gate(w, b):
        return jax.nn.sigmoid(jnp.einsum('bsni,nio->bsno', ub, w.astype(f32)).reshape(B, S, W) + b.astype(f32))

    def direction(d, reverse):
        r = gate(w_a[d], b_a[d])
        i = gate(w_x[d], b_x[d])
        log_a = -LRU_C * r * jax.nn.softplus(-lam[d].astype(f32))
        a = jnp.exp(log_a)
        xin = jnp.sqrt(-jnp.expm1(2.0 * log_a)) * (i * u)
        _, hs = lax.associative_scan(_linrec_combine, (a, xin), axis=1, reverse=reverse)
        return hs

    hsum = direction(0, False) + direction(1, True)
    return (hsum * jax.nn.gelu(gb.astype(f32))).astype(xb.dtype)


DIFF_TQ = 256
DIFF_TK = 256
DIFF_POS_LANE = 2 * DIFF_DH


def _diff_attn_kernel(slopes_ref, lam_ref, q_ref, k_ref, vt_ref, g_ref, o_ref,
                      m_scr, l_scr, acc_scr, *, tq, tk, seq, out_scale):
    h = pl.program_id(1)
    qi = pl.program_id(2)
    slope = slopes_ref[h]
    lam = lam_ref[0]
    nk = seq // tk

    q = q_ref[0, 0]
    lane = lax.broadcasted_iota(jnp.int32, q.shape, 1)
    zero = jnp.zeros_like(q)
    q1 = jnp.where((lane >= DIFF_DH) & (lane < DIFF_POS_LANE), zero, q)
    q2 = jnp.where(lane < DIFF_DH, zero, q)
    qs = jnp.concatenate([q1, q2], axis=0)
    lane2 = lax.broadcasted_iota(jnp.int32, qs.shape, 1)
    q_low = qs
    q_up = jnp.where(lane2 >= DIFF_POS_LANE, -qs, qs)
    q_mid = jnp.where(lane2 >= DIFF_POS_LANE, jnp.zeros_like(qs), qs)

    col = lax.broadcasted_iota(jnp.int32, (1, 2 * tq), 1)
    qpos = qi * tq + jnp.where(col >= tq, col - tq, col)
    qpos_f = qpos.astype(F32)

    m_scr[...] = jnp.full(m_scr.shape, NEG_BIG, F32)
    l_scr[...] = jnp.zeros(l_scr.shape, F32)
    acc_scr[...] = jnp.zeros(acc_scr.shape, F32)

    def tile(kj, qv, row_term, diag):
        k0 = pl.multiple_of(kj * tk, tk)
        kt = k_ref[0, 0, pl.ds(k0, tk), :]
        s = lax.dot_general(kt, qv, (((1,), (1,)), ((), ())), preferred_element_type=F32)
        if diag:
            kpos = k0 + lax.broadcasted_iota(jnp.int32, (tk, 2 * tq), 0)
            s = s - slope * jnp.abs(qpos - kpos).astype(F32)
        m_old = m_scr[...]
        m_blk = jnp.max(s, axis=0, keepdims=True)
        if row_term is not None:
            m_blk = m_blk + row_term
        m_new = jnp.maximum(m_old, m_blk)
        alpha = jnp.exp(m_old - m_new)
        shift = m_new if row_term is None else m_new - row_term
        p = jnp.exp(s - shift)
        l_scr[...] = alpha * l_scr[...] + jnp.sum(p, axis=0, keepdims=True)
        vt = vt_ref[0, 0, :, pl.ds(k0, tk)]
        acc_scr[...] = alpha * acc_scr[...] + jnp.dot(vt, p.astype(BF16), preferred_element_type=F32)
        m_scr[...] = m_new

    n_low = (qi * tq) // tk
    n_diag_end = ((qi + 1) * tq + tk - 1) // tk

    def low_body(kj, c):
        tile(kj, q_low, -slope * qpos_f, False)
        return c

    def diag_body(kj, c):
        tile(kj, q_mid, None, True)
        return c

    def up_body(kj, c):
        tile(kj, q_up, slope * qpos_f, False)
        return c

    lax.fori_loop(0, n_low, low_body, 0)
    lax.fori_loop(n_low, n_diag_end, diag_body, 0)
    lax.fori_loop(n_diag_end, nk, up_body, 0)

    inv_l = 1.0 / l_scr[...]
    o2 = acc_scr[...] * inv_l
    o = o2[:, :tq] - lam * o2[:, tq:]
    ms = jnp.mean(o * o, axis=0, keepdims=True)
    o = o * lax.rsqrt(ms + RMS_EPS) * g_ref[...] * out_scale
    o_ref[0, 0] = o.astype(o_ref.dtype)


def diff_attention_pallas(dq, dk, dv, lq1, lk1, lq2, lk2, norm_g, lambda_init):
    B, S, _ = dq.shape
    H = DIFF_HEADS
    tq = min(DIFF_TQ, S)
    tk = min(DIFF_TK, S)
    slopes = jnp.exp2(-8.0 * jnp.arange(1, H + 1, dtype=F32) / H)
    lam = (jnp.exp(jnp.sum(lq1.astype(F32) * lk1.astype(F32)))
           - jnp.exp(jnp.sum(lq2.astype(F32) * lk2.astype(F32))) + lambda_init).reshape(1)
    pad = LANES - 2 * DIFF_DH - 2
    q4 = dq.astype(F32).reshape(B, S, H, 2 * DIFF_DH).transpose(0, 2, 1, 3) * DIFF_DH ** -0.5
    q_aug = jnp.concatenate([q4, jnp.ones((B, H, S, 2), F32), jnp.zeros((B, H, S, pad), F32)], axis=-1).astype(BF16)
    k4 = dk.astype(F32).reshape(B, S, H, 2 * DIFF_DH).transpose(0, 2, 1, 3)
    j = jnp.arange(S, dtype=jnp.int32)
    jlo = (j & 255).astype(F32)
    jhi = (j - (j & 255)).astype(F32)
    kpos = jnp.stack([jlo, jhi], axis=-1)[None, None] * slopes[None, :, None, None]
    k_aug = jnp.concatenate([k4, jnp.broadcast_to(kpos, (B, H, S, 2)), jnp.zeros((B, H, S, pad), F32)],
                            axis=-1).astype(BF16)
    vt = dv.astype(BF16).reshape(B, S, H, DIFF_DV).transpose(0, 2, 3, 1)
    g = norm_g.astype(F32).reshape(DIFF_DV, 1)

    kern = functools.partial(_diff_attn_kernel, tq=tq, tk=tk, seq=S, out_scale=1.0 - lambda_init)
    out = pl.pallas_call(
        kern,
        grid=(B, H, S // tq),
        in_specs=[
            pl.BlockSpec(memory_space=pltpu.SMEM),
            pl.BlockSpec(memory_space=pltpu.SMEM),
            pl.BlockSpec((1, 1, tq, LANES), lambda b, h, i: (b, h, i, 0)),
            pl.BlockSpec((1, 1, S, LANES), lambda b, h, i: (b, h, 0, 0)),
            pl.BlockSpec((1, 1, DIFF_DV, S), lambda b, h, i: (b, h, 0, 0)),
            pl.BlockSpec((DIFF_DV, 1), lambda b, h, i: (0, 0)),
        ],
        out_specs=pl.BlockSpec((1, 1, DIFF_DV, tq), lambda b, h, i: (b, h, 0, i)),
        out_shape=jax.ShapeDtypeStruct((B, H, DIFF_DV, S), F32),
        scratch_shapes=[
            pltpu.VMEM((1, 2 * tq), F32),
            pltpu.VMEM((1, 2 * tq), F32),
            pltpu.VMEM((DIFF_DV, 2 * tq), F32),
        ],
        compiler_params=pltpu.CompilerParams(
            dimension_semantics=("parallel", "parallel", "arbitrary"),
            vmem_limit_bytes=VMEM_LIMIT),
        name="diff_attn",
    )(slopes, lam, q_aug, k_aug, vt, g)
    return out.transpose(0, 3, 1, 2).reshape(B, S, DIFF_V_W)


def gdn_scan(q, k, v, beta, log_a):
    b_, s_, h_, dk = q.shape
    dv = v.shape[-1]
    c = GDN_CHUNK
    qc, kc, vc, bc, gc = (_to_chunks(t, c) for t in (q, k, v, beta, log_a))
    gam = jnp.cumsum(gc, axis=-1)
    diff = gam[..., :, None] - gam[..., None, :]
    incl = jnp.tril(jnp.ones((c, c), dtype=bool))
    strict = jnp.tril(jnp.ones((c, c), dtype=bool), k=-1)
    decay = jnp.exp(jnp.where(incl, diff, -jnp.inf))
    kk = jnp.einsum('bhntd,bhnsd->bhnts', kc, kc)
    a_mat = jnp.where(strict, bc[..., None] * kk * decay, 0.0) + jnp.eye(c, dtype=jnp.float32)
    rhs = jnp.concatenate([vc * bc[..., None], kc * (bc * jnp.exp(gam))[..., None]], axis=-1)
    sol = lax.linalg.triangular_solve(a_mat, rhs, left_side=True, lower=True, unit_diagonal=True)
    u_val, k_cum = sol[..., :dv], sol[..., dv:]
    qk = jnp.einsum('bhntd,bhnsd->bhnts', qc, kc) * decay
    q_dec = qc * jnp.exp(gam)[..., None]
    k_dec = kc * jnp.exp(gam[..., -1:] - gam)[..., None]
    c_dec = jnp.exp(gam[..., -1])

    def step(state, inp):
        u_i, kc_i, qk_i, qd_i, kd_i, cd_i = inp
        v_new = u_i - jnp.einsum('bhtd,bhde->bhte', kc_i, state)
        o = jnp.einsum('bhtd,bhde->bhte', qd_i, state) + jnp.einsum('bhts,bhse->bhte', qk_i, v_new)
        state = cd_i[..., None, None] * state + jnp.einsum('bhsd,bhse->bhde', kd_i, v_new)
        return state, o

    xs = tuple(jnp.moveaxis(t, 2, 0) for t in (u_val, k_cum, qk, q_dec, k_dec, c_dec))
    _, o = lax.scan(step, jnp.zeros((b_, h_, dk, dv), jnp.float32), xs)
    return _from_chunks(o)


def gdn_mixer(qkv, z, ba, conv_w, a_log, dt_bias, norm_g):
    B, S, _ = qkv.shape
    f32 = jnp.float32
    qkv = jax.nn.silu(dwconv_centred(qkv, conv_w)).astype(f32)
    q, k, v = jnp.split(qkv, [GDN_QK_W, 2 * GDN_QK_W], axis=-1)
    q = l2norm(q.reshape(B, S, GDN_HEADS, GDN_DK)) * GDN_DK ** -0.5
    k = l2norm(k.reshape(B, S, GDN_HEADS, GDN_DK))
    v = v.reshape(B, S, GDN_HEADS, GDN_DV)
    ba = ba.astype(f32).reshape(B, S, 4, GDN_HEADS)
    beta = jax.nn.sigmoid(ba[:, :, :2])
    log_a = -jnp.exp(a_log.astype(f32)) * jax.nn.softplus(ba[:, :, 2:] + dt_bias.astype(f32))
    o_f = gdn_scan(q, k, v, beta[:, :, 0], log_a[:, :, 0])
    o_b = _flip(gdn_scan(_flip(q), _flip(k), _flip(v), _flip(beta[:, :, 1]), _flip(log_a[:, :, 1])))
    o = rmsnorm(o_f + o_b, norm_g) * jax.nn.silu(z.astype(f32)).reshape(B, S, GDN_HEADS, GDN_DV)
    return o.reshape(B, S, GDN_V_W).astype(z.dtype)


def gated_merge(h, branches, merge_w, merge_b, branch_up):
    out = None
    for i, y in enumerate(branches):
        term = jax.nn.sigmoid(h @ merge_w[i] + merge_b[i]) * (y @ branch_up[i])
        out = term if out is None else out + term
    return out


def memory_cross_attention(h, mem_n, w_q, w_kv, w_o):
    B, S, D = h.shape
    M = mem_n.shape[1]
    q = (h @ w_q).reshape(B, S, XATTN_HEADS, XATTN_DH)
    kv = (mem_n @ w_kv).reshape(B, M, 2, XATTN_HEADS, XATTN_DH)
    k, v = kv[:, :, 0], kv[:, :, 1]
    s = jnp.einsum('bshd,bmhd->bhsm', q, k).astype(jnp.float32) * XATTN_DH ** -0.5
    p = jax.nn.softmax(s, axis=-1)
    o = jnp.einsum('bhsm,bmhd->bshd', p.astype(v.dtype), v).reshape(B, S, D)
    return o @ w_o


def hier_moe(h, w_group, b_group, w_expert, b_expert, w1, w3, w2):
    B, S, D = h.shape
    T = B * S
    f32 = jnp.float32
    ht = h.reshape(T, D)
    g_logits = (ht @ w_group).astype(f32) + b_group.astype(f32)
    g_prob = jax.nn.softmax(g_logits, axis=-1)
    _, g_idx = lax.top_k(g_logits, 1)
    g_w = jnp.take_along_axis(g_prob, g_idx, axis=1)
    e_logits = ((ht @ w_expert).astype(f32) + b_expert.astype(f32)).reshape(T, N_GROUPS, EXPERTS_PER_GROUP)
    e_logits = jnp.take_along_axis(
        e_logits, jnp.broadcast_to(g_idx[:, :, None], (T, 1, EXPERTS_PER_GROUP)), axis=1)[:, 0]
    e_top, e_idx = lax.top_k(e_logits, TOP_K)
    gate = jax.nn.softmax(e_top, axis=-1) * g_w
    eid = (g_idx * EXPERTS_PER_GROUP + e_idx).reshape(-1)
    tok = jnp.repeat(jnp.arange(T, dtype=jnp.int32), TOP_K)
    wts = gate.reshape(-1)
    A = T * TOP_K
    n_blk = -(-A // MOE_BLOCK) + N_EXPERTS
    P = n_blk * MOE_BLOCK
    order = jnp.argsort(eid)
    eid_s, tok_s, w_s = eid[order], tok[order], wts[order]
    counts = jnp.zeros((N_EXPERTS,), jnp.int32).at[eid].add(1)
    padded = (counts + MOE_BLOCK - 1) // MOE_BLOCK * MOE_BLOCK
    pad_end = jnp.cumsum(padded)
    pad_start = pad_end - padded
    seg_start = jnp.cumsum(counts) - counts
    dest = pad_start[eid_s] + jnp.arange(A, dtype=jnp.int32) - seg_start[eid_s]
    row_tok = jnp.full((P,), T, jnp.int32).at[dest].set(tok_s)
    row_w = jnp.zeros((P,), f32).at[dest].set(w_s)
    blk_expert = jnp.minimum(
        jnp.searchsorted(pad_end, jnp.arange(n_blk, dtype=jnp.int32) * MOE_BLOCK, side='right'),
        N_EXPERTS - 1)
    h_pad = jnp.concatenate([ht, jnp.zeros((1, D), ht.dtype)], axis=0)
    xb = h_pad[row_tok].reshape(n_blk, MOE_BLOCK, D)

    def expert_block(args):
        xi, e = args
        return (jax.nn.silu(xi @ w1[e]) * (xi @ w3[e])) @ w2[e]

    yb = lax.map(expert_block, (xb, blk_expert)).reshape(P, D)
    out = jnp.zeros((T + 1, D), h.dtype).at[row_tok].add(yb * row_w[:, None].astype(yb.dtype))
    return out[:T].reshape(B, S, D)


def kernel(x, mem, mix_norm, w_in, gla_lr_up, gla_lr_bias, gla_norm, lru_conv_w, lru_conv_b, lru_w_a, lru_b_a, lru_w_x, lru_b_x, lru_lambda, diff_lq1, diff_lk1, diff_lq2, diff_lk2, diff_norm, gdn_conv_w, gdn_a_log, gdn_dt_bias, gdn_norm, merge_w, merge_b, branch_up, mix_out, xattn_norm, mem_norm, xattn_wq, xattn_wkv, xattn_wo, moe_norm, moe_w_group, moe_b_group, moe_w_expert, moe_b_expert, moe_w1, moe_w3, moe_w2, final_norm):
    split_at = np.cumsum(IN_SPLITS)[:-1].tolist()
    for l in range(DEPTH):
        h = rmsnorm(x, mix_norm[l])
        parts = jnp.split(h @ w_in[l], split_at, axis=-1)
        gq, gk, gv, gog, glr, lx, lg, dq, dk, dv, nqkv, nz, nba = parts
        y_gla = gla_mixer(gq, gk, gv, gog, glr, gla_lr_up[l], gla_lr_bias[l], gla_norm[l])
        y_lru = rglru_mixer(lx, lg, lru_conv_w[l], lru_conv_b[l], lru_w_a[l], lru_b_a[l],
                            lru_w_x[l], lru_b_x[l], lru_lambda[l])
        y_diff = diff_attention_pallas(dq, dk, dv, diff_lq1[l], diff_lk1[l], diff_lq2[l], diff_lk2[l],
                                       diff_norm[l], 0.8 - 0.6 * math.exp(-0.3 * l))
        y_gdn = gdn_mixer(nqkv, nz, nba, gdn_conv_w[l], gdn_a_log[l], gdn_dt_bias[l], gdn_norm[l])
        merged = gated_merge(h, (y_gla, y_lru, y_diff, y_gdn), merge_w[l], merge_b[l], branch_up[l])
        x = x + merged @ mix_out[l]
        h = rmsnorm(x, xattn_norm[l])
        x = x + memory_cross_attention(h, rmsnorm(mem, mem_norm[l]), xattn_wq[l], xattn_wkv[l], xattn_wo[l])
        h = rmsnorm(x, moe_norm[l])
        x = x + hier_moe(h, moe_w_group[l], moe_b_group[l], moe_w_expert[l], moe_b_expert[l],
                         moe_w1[l], moe_w3[l], moe_w2[l])
    return rmsnorm(x, final_norm)
```

```python
import functools
import math

import jax
import jax.numpy as jnp
import numpy as np
from jax import lax
from jax.experimental import pallas as pl
from jax.experimental.pallas import tpu as pltpu

D_MODEL = 1024
DEPTH = 2
N_BRANCHES = 4
BRANCH_WIDTH = D_MODEL // 4
RMS_EPS = 1e-6
CONV_WIDTH = 4

GLA_HEADS = 4
GLA_DV = BRANCH_WIDTH // GLA_HEADS
GLA_DK = GLA_DV // 2
GLA_RANK = 16
GLA_GATE_NORM = 16.0
GLA_CHUNK = 64
GLA_QK_W = GLA_HEADS * GLA_DK
GLA_V_W = GLA_HEADS * GLA_DV

LRU_WIDTH = BRANCH_WIDTH
LRU_BLOCKS = 4
LRU_C = 8.0

DIFF_HEADS = 4
DIFF_DV = BRANCH_WIDTH // DIFF_HEADS
DIFF_DH = DIFF_DV // 2
DIFF_QK_W = DIFF_HEADS * 2 * DIFF_DH
DIFF_V_W = DIFF_HEADS * DIFF_DV

GDN_HEADS = 4
GDN_DK = BRANCH_WIDTH // GDN_HEADS
GDN_DV = BRANCH_WIDTH // GDN_HEADS
GDN_CHUNK = 64
GDN_QK_W = GDN_HEADS * GDN_DK
GDN_V_W = GDN_HEADS * GDN_DV

IN_SPLITS = (GLA_QK_W, GLA_QK_W, GLA_V_W, GLA_V_W, 2 * GLA_RANK,
             LRU_WIDTH, LRU_WIDTH,
             DIFF_QK_W, DIFF_QK_W, DIFF_V_W,
             2 * GDN_QK_W + GDN_V_W, GDN_V_W, 4 * GDN_HEADS)

XATTN_HEADS = 4
XATTN_DH = D_MODEL // XATTN_HEADS

N_GROUPS = 4
EXPERTS_PER_GROUP = 8
N_EXPERTS = N_GROUPS * EXPERTS_PER_GROUP
TOP_K = 2
D_EXPERT = D_MODEL // 2
MOE_BLOCK = 128

LANES = 128
VMEM_LIMIT = 56 * 1024 * 1024

F32 = jnp.float32
BF16 = jnp.bfloat16
NEG_BIG = -1e30


def rmsnorm(x, g):
    xf = x.astype(jnp.float32)
    y = xf * lax.rsqrt(jnp.mean(xf * xf, axis=-1, keepdims=True) + RMS_EPS)
    return (y * g.astype(jnp.float32)).astype(x.dtype)


def l2norm(t):
    return t * lax.rsqrt(jnp.sum(t * t, axis=-1, keepdims=True) + 1e-6)


def _flip(t):
    return jnp.flip(t, axis=1)


def dwconv_centred(x, w):
    k = w.shape[0]
    return lax.conv_general_dilated(
        x, w[:, None, :].astype(x.dtype), window_strides=(1,),
        padding=[(k // 2, k - 1 - k // 2)],
        dimension_numbers=('NWC', 'WIO', 'NWC'),
        feature_group_count=x.shape[-1])


def _to_chunks(t, chunk):
    b, s, h = t.shape[:3]
    t = t.reshape((b, s // chunk, chunk, h) + t.shape[3:])
    return jnp.moveaxis(t, 3, 1)


def _from_chunks(o):
    nc, b, h, c, d = o.shape
    return o.transpose(1, 0, 3, 2, 4).reshape(b, nc * c, h, d)


def _linrec_combine(c1, c2):
    a1, b1 = c1
    a2, b2 = c2
    return a1 * a2, a2 * b1 + b2


def gla_scan(q, k, v, log_f):
    b_, s_, h_, dk = q.shape
    dv = v.shape[-1]
    c = GLA_CHUNK
    incl = jnp.tril(jnp.ones((c, c), dtype=bool))[:, :, None]

    def step(state, inp):
        q_i, k_i, v_i, g_i = inp
        bcum = jnp.cumsum(g_i, axis=2)
        decay = jnp.exp(jnp.where(incl, bcum[:, :, :, None, :] - bcum[:, :, None, :, :], -jnp.inf))
        scores = jnp.einsum('bhtd,bhsd,bhtsd->bhts', q_i, k_i, decay)
        o = (jnp.einsum('bhts,bhse->bhte', scores, v_i)
             + jnp.einsum('bhtd,bhde->bhte', q_i * jnp.exp(bcum), state))
        b_end = bcum[:, :, -1:, :]
        state = (jnp.exp(b_end)[:, :, 0, :, None] * state
                 + jnp.einsum('bhsd,bhse->bhde', k_i * jnp.exp(b_end - bcum), v_i))
        return state, o

    xs = tuple(jnp.moveaxis(_to_chunks(t, c), 2, 0) for t in (q, k, v, log_f))
    _, o = lax.scan(step, jnp.zeros((b_, h_, dk, dv), jnp.float32), xs)
    return _from_chunks(o)


def gla_mixer(q, k, v, og, lr, lr_up, lr_bias, norm_g):
    B, S, _ = q.shape
    f32 = jnp.float32
    qh = q.astype(f32).reshape(B, S, GLA_HEADS, GLA_DK) * GLA_DK ** -0.5
    kh = k.astype(f32).reshape(B, S, GLA_HEADS, GLA_DK)
    vh = v.astype(f32).reshape(B, S, GLA_HEADS, GLA_DV)
    lr = lr.astype(f32).reshape(B, S, 2, GLA_RANK)
    z = jnp.einsum('bsdr,drk->bsdk', lr, lr_up.astype(f32)) + lr_bias.astype(f32)
    log_f = (jax.nn.log_sigmoid(z) / GLA_GATE_NORM).reshape(B, S, 2, GLA_HEADS, GLA_DK)
    o_f = gla_scan(qh, kh, vh, log_f[:, :, 0])
    o_b = _flip(gla_scan(_flip(qh), _flip(kh), _flip(vh), _flip(log_f[:, :, 1])))
    o = rmsnorm(o_f + o_b, norm_g) * jax.nn.silu(og.astype(f32)).reshape(B, S, GLA_HEADS, GLA_DV)
    return o.reshape(B, S, GLA_V_W).astype(q.dtype)


def rglru_mixer(xb, gb, conv_w, conv_b, w_a, b_a, w_x, b_x, lam):
    B, S, W = xb.shape
    f32 = jnp.float32
    u = (dwconv_centred(xb, conv_w) + conv_b).astype(f32)
    ub = u.reshape(B, S, LRU_BLOCKS, W // LRU_BLOCKS)

    def gate(w, b):
        return jax.nn.sigmoid(jnp.einsum('bsni,nio->bsno', ub, w.astype(f32)).reshape(B, S, W) + b.astype(f32))

    def direction(d, reverse):
        r = gate(w_a[d], b_a[d])
        i = gate(w_x[d], b_x[d])
        log_a = -LRU_C * r * jax.nn.softplus(-lam[d].astype(f32))
        a = jnp.exp(log_a)
        xin = jnp.sqrt(-jnp.expm1(2.0 * log_a)) * (i * u)
        _, hs = lax.associative_scan(_linrec_combine, (a, xin), axis=1, reverse=reverse)
        return hs

    hsum = direction(0, False) + direction(1, True)
    return (hsum * jax.nn.gelu(gb.astype(f32))).astype(xb.dtype)


def _cparams(*sem):
    return pltpu.CompilerParams(dimension_semantics=sem, vmem_limit_bytes=VMEM_LIMIT)


def _split_bf16(a, n):
    parts, r = [], a
    for i in range(n):
        p = r.astype(BF16)
        parts.append(p)
        if i + 1 < n:
            r = r - p.astype(F32)
    return parts


def _dot_nn(a, b):
    return jnp.dot(a, b, preferred_element_type=F32)


def _dot_nt(a, b):
    return lax.dot_general(a, b, (((1,), (1,)), ((), ())), preferred_element_type=F32)


def _dot_tn(a, b):
    return lax.dot_general(a, b, (((0,), (0,)), ((), ())), preferred_element_type=F32)


def _dot_split(a, b_exact, n=3, dot=_dot_nn):
    acc = None
    for p in _split_bf16(a, n):
        t = dot(p, b_exact)
        acc = t if acc is None else acc + t
    return acc


def _dot_f32(a, b, dot=_dot_nn):
    a_hi, a_lo = _split_bf16(a, 2)
    b_hi, b_lo = _split_bf16(b, 2)
    return dot(a_hi, b_hi) + (dot(a_hi, b_lo) + dot(a_lo, b_hi))


def _rms_rows(x, g):
    return x * lax.rsqrt(jnp.mean(x * x, axis=-1, keepdims=True) + RMS_EPS) * g


def _sigmoid(x):
    return 1.0 / (1.0 + jnp.exp(-x))


def _softplus(x):
    return jnp.maximum(x, 0.0) + jnp.log(1.0 + jnp.exp(-jnp.abs(x)))


def _silu(x):
    return x * _sigmoid(x)


_G_GLA = (0, 640)
_G_GOG = (640, 896)
_G_LX = (896, 1152)
_G_LG = (1152, 1408)
_G_DQ = (1408, 1664)
_G_DK = (1664, 1920)
_G_DV = (1920, 2176)
_G_NQKV = (2176, 2944)
_G_NZ = (2944, 3200)
_G_NBA = (3200, 3328)
IN_PAD_W = 3328
IN_TM = 512


def _permute_w_in(w):
    z = lambda n: jnp.zeros((w.shape[0], n), w.dtype)
    return jnp.concatenate([w[:, 0:512], w[:, 768:800], z(96), w[:, 512:768], w[:, 800:3120], z(112)],
                           axis=1).astype(BF16)


def _inproj_kernel(x_ref, g_ref, w_ref, gla_ref, gog_ref, lx_ref, lg_ref, dq_ref, dk_ref, dv_ref,
                   nqkv_ref, nz_ref, nba_ref):
    h = _rms_rows(x_ref[...], g_ref[...]).astype(BF16)

    def proj(grp):
        return jnp.dot(h, w_ref[:, grp[0]:grp[1]], preferred_element_type=F32)

    gla_ref[...] = proj(_G_GLA)
    gog_ref[...] = proj(_G_GOG).astype(BF16)
    lx_ref[...] = proj(_G_LX)
    lg_ref[...] = proj(_G_LG).astype(BF16)
    dq_ref[...] = (proj(_G_DQ) * DIFF_DH ** -0.5).astype(BF16)
    dk_ref[...] = proj(_G_DK).astype(BF16)
    dv_ref[...] = proj(_G_DV).astype(BF16)
    nqkv_ref[...] = proj(_G_NQKV)
    nz_ref[...] = proj(_G_NZ).astype(BF16)
    nba_ref[...] = proj(_G_NBA)


def inproj_pallas(xt, g, w_in):
    T, D = xt.shape
    tm = min(IN_TM, T)
    groups = [(_G_GLA, F32), (_G_GOG, BF16), (_G_LX, F32), (_G_LG, BF16), (_G_DQ, BF16), (_G_DK, BF16),
              (_G_DV, BF16), (_G_NQKV, F32), (_G_NZ, BF16), (_G_NBA, F32)]
    return pl.pallas_call(
        _inproj_kernel,
        grid=(T // tm,),
        in_specs=[pl.BlockSpec((tm, D), lambda i: (i, 0)),
                  pl.BlockSpec((1, D), lambda i: (0, 0)),
                  pl.BlockSpec((D, IN_PAD_W), lambda i: (0, 0))],
        out_specs=[pl.BlockSpec((tm, b - a), lambda i: (i, 0)) for (a, b), _ in groups],
        out_shape=[jax.ShapeDtypeStruct((T, b - a), dt) for (a, b), dt in groups],
        compiler_params=_cparams("parallel"),
        name="inproj",
    )(xt, g.reshape(1, D).astype(F32), _permute_w_in(w_in))


def _gla_kernel(x_ref, tri_ref, w_ref, b_ref, o_ref, st_ref, *, nb):
    c = GLA_CHUNK

    @pl.when(pl.program_id(1) == 0)
    def _():
        st_ref[...] = jnp.zeros(st_ref.shape, F32)

    tri = tri_ref[0]
    tri_bf = tri.astype(BF16)
    tri4 = jnp.concatenate([tri] * GLA_HEADS, axis=0)
    w = w_ref[0]
    bias = b_ref[0]
    lane_qk = lax.broadcasted_iota(jnp.int32, (c, GLA_QK_W), 1) // GLA_DK
    lane_v = lax.broadcasted_iota(jnp.int32, (c, GLA_V_W), 1) // GLA_DV
    row_s = lax.broadcasted_iota(jnp.int32, (GLA_V_W, GLA_QK_W), 0) // GLA_DV
    col_s = lax.broadcasted_iota(jnp.int32, (GLA_V_W, GLA_QK_W), 1) // GLA_DK
    st_mask = row_s == col_s

    for b in range(nb):
        blk = x_ref[b]
        q = blk[:, 0:128] * GLA_DK ** -0.5
        k = blk[:, 128:256]
        v = blk[:, 256:512].astype(BF16)
        lr = blk[:, 512:640]
        z = _dot_f32(lr, w) + bias
        g = (jnp.minimum(z, 0.0) - jnp.log(1.0 + jnp.exp(-jnp.abs(z)))) * (1.0 / GLA_GATE_NORM)
        acc = None
        for p in _split_bf16(g, 3):
            t = _dot_nn(tri_bf, p)
            acc = t if acc is None else acc + t
        bc = acc
        tot = jnp.sum(g, axis=0, keepdims=True)
        ref = 0.5 * tot
        qt = q * jnp.exp(bc - ref)
        kt = (k * jnp.exp(ref - bc)).astype(BF16)
        qd = (q * jnp.exp(bc)).astype(BF16)
        kd = (k * jnp.exp(tot - bc)).astype(BF16)
        qstack = jnp.concatenate(
            [jnp.where(lane_qk == hh, qt, 0.0) for hh in range(GLA_HEADS)], axis=0).astype(BF16)
        s = _dot_nt(qstack, kt) * tri4
        st = st_ref[b]
        o = _dot_nt(qd, st.astype(BF16))
        for hh in range(GLA_HEADS):
            oh = _dot_nn(s[hh * c:(hh + 1) * c].astype(BF16), v)
            o = o + jnp.where(lane_v == hh, oh, 0.0)
        o_ref[0, b] = o
        upd = _dot_tn(v, kd)
        st_ref[b] = st * jnp.exp(tot) + jnp.where(st_mask, upd, 0.0)


def gla_pallas(gla_in, lr_up, lr_bias):
    B, S, W = gla_in.shape
    c = GLA_CHUNK
    nc = S // c
    r = np.arange(c)
    tri = jnp.asarray(np.stack([r[:, None] >= r[None, :], r[:, None] <= r[None, :]]).astype(np.float32))
    w = jnp.zeros((2, LANES, GLA_QK_W), F32)
    w = w.at[0, 0:GLA_RANK].set(lr_up[0].astype(F32)).at[1, GLA_RANK:2 * GLA_RANK].set(lr_up[1].astype(F32))
    bias = lr_bias.astype(F32).reshape(2, 1, GLA_QK_W)

    def tmap(d, i):
        return (0, i + d * (nc - 1 - 2 * i), 0)

    return pl.pallas_call(
        functools.partial(_gla_kernel, nb=B),
        grid=(2, nc),
        in_specs=[pl.BlockSpec((B, c, W), tmap),
                  pl.BlockSpec((1, c, c), lambda d, i: (d, 0, 0)),
                  pl.BlockSpec((1, LANES, GLA_QK_W), lambda d, i: (d, 0, 0)),
                  pl.BlockSpec((1, 1, GLA_QK_W), lambda d, i: (d, 0, 0))],
        out_specs=pl.BlockSpec((1, B, c, GLA_V_W), lambda d, i: (d,) + tmap(d, i)),
        out_shape=jax.ShapeDtypeStruct((2, B, S, GLA_V_W), F32),
        scratch_shapes=[pltpu.VMEM((B, GLA_V_W, GLA_QK_W), F32)],
        compiler_params=_cparams("arbitrary", "arbitrary"),
        name="gla",
    )(gla_in, tri, w, bias)


LRU_TB = 512
HALO = 8


def _halo_specs(tb, width, nt):
    r = tb // HALO
    return [pl.BlockSpec((1, HALO, width), lambda b, i: (b, jnp.maximum(i * r - 1, 0), 0)),
            pl.BlockSpec((1, tb, width), lambda b, i: (b, i, 0)),
            pl.BlockSpec((1, HALO, width), lambda b, i: (b, jnp.minimum((i + 1) * r, nt * r - 1), 0))]


def _conv4(prev_ref, cur_ref, next_ref, w, nt):
    i = pl.program_id(1)
    cur = cur_ref[0]
    tb = cur.shape[0]
    prev = prev_ref[0] * jnp.where(i > 0, 1.0, 0.0)
    nxt = next_ref[0] * jnp.where(i < nt - 1, 1.0, 0.0)
    ext = jnp.concatenate([prev, cur, nxt], axis=0)
    out = None
    for j in range(CONV_WIDTH):
        off = HALO + j - CONV_WIDTH // 2
        t = ext[off:off + tb] * w[j:j + 1]
        out = t if out is None else out + t
    return out


def _lru_prep_kernel(prev_ref, cur_ref, next_ref, cw_ref, cb_ref, wg_ref, bg_ref, lam_ref, a_ref, x_ref, *, nt):
    u = _conv4(prev_ref, cur_ref, next_ref, cw_ref[...], nt) + cb_ref[...]
    gates = _sigmoid(jnp.dot(u.astype(BF16), wg_ref[...], preferred_element_type=F32) + bg_ref[...])
    w = LRU_WIDTH
    for d in range(2):
        r = gates[:, (2 * d) * w:(2 * d + 1) * w]
        ig = gates[:, (2 * d + 1) * w:(2 * d + 2) * w]
        log_a = -LRU_C * r * _softplus(-lam_ref[d:d + 1])
        a_ref[d, 0] = jnp.exp(log_a)
        x_ref[d, 0] = jnp.sqrt(1.0 - jnp.exp(2.0 * log_a)) * (ig * u)


def _lru_scan_kernel(a_ref, x_ref, h_ref, st_ref, *, tb):
    d = pl.program_id(0)

    @pl.when(pl.program_id(1) == 0)
    def _():
        st_ref[...] = jnp.zeros(st_ref.shape, F32)

    def body(t, h):
        tt = t + d * (tb - 1 - 2 * t)
        h = a_ref[0, :, pl.ds(tt, 1), :] * h + x_ref[0, :, pl.ds(tt, 1), :]
        h_ref[0, :, pl.ds(tt, 1), :] = h
        return h

    st_ref[...] = lax.fori_loop(0, tb, body, st_ref[...], unroll=8)


def lru_pallas(lx, conv_w, conv_b, w_a, b_a, w_x, b_x, lam):
    B, S, W = lx.shape
    tb = min(LRU_TB, S)
    nt = S // tb
    blk = W // LRU_BLOCKS

    def dense(wb):
        m = jnp.zeros((W, W), F32)
        for n in range(LRU_BLOCKS):
            m = m.at[n * blk:(n + 1) * blk, n * blk:(n + 1) * blk].set(wb[n].astype(F32))
        return m

    wg = jnp.concatenate([dense(w_a[0]), dense(w_x[0]), dense(w_a[1]), dense(w_x[1])], axis=1).astype(BF16)
    bg = jnp.concatenate([b_a[0], b_x[0], b_a[1], b_x[1]]).astype(F32).reshape(1, 4 * W)
    full = lambda shape: pl.BlockSpec(shape, lambda b, i: (0,) * len(shape))
    a, xin = pl.pallas_call(
        functools.partial(_lru_prep_kernel, nt=nt),
        grid=(B, nt),
        in_specs=_halo_specs(tb, W, nt) + [full((CONV_WIDTH, W)), full((1, W)), full((W, 4 * W)), full((1, 4 * W)),
                                           full((2, W))],
        out_specs=[pl.BlockSpec((2, 1, tb, W), lambda b, i: (0, b, i, 0))] * 2,
        out_shape=[jax.ShapeDtypeStruct((2, B, S, W), F32)] * 2,
        compiler_params=_cparams("parallel", "parallel"),
        name="lru_prep",
    )(lx, lx, lx, conv_w.astype(F32), conv_b.astype(F32).reshape(1, W), wg, bg, lam.astype(F32))

    def tmap(d, i):
        return (d, 0, i + d * (nt - 1 - 2 * i), 0)

    return pl.pallas_call(
        functools.partial(_lru_scan_kernel, tb=tb),
        grid=(2, nt),
        in_specs=[pl.BlockSpec((1, B, tb, W), tmap)] * 2,
        out_specs=pl.BlockSpec((1, B, tb, W), tmap),
        out_shape=jax.ShapeDtypeStruct((2, B, S, W), F32),
        scratch_shapes=[pltpu.VMEM((B, 1, W), F32)],
        compiler_params=_cparams("arbitrary", "arbitrary"),
        name="lru_scan",
    )(a, xin)


DIFF_TQ = 256
DIFF_TK = 256
DIFF_POS_LANE = 2 * DIFF_DH


def _diff_attn_kernel(slopes_ref, lam_ref, q_ref, k_ref, vt_ref, g_ref, o_ref,
                      m_scr, l_scr, acc_scr, *, tq, tk, seq, out_scale):
    h = pl.program_id(1)
    qi = pl.program_id(2)
    slope = slopes_ref[h]
    lam = lam_ref[0]
    nk = seq // tk

    q = q_ref[0, 0]
    lane = lax.broadcasted_iota(jnp.int32, q.shape, 1)
    zero = jnp.zeros_like(q)
    q1 = jnp.where((lane >= DIFF_DH) & (lane < DIFF_POS_LANE), zero, q)
    q2 = jnp.where(lane < DIFF_DH, zero, q)
    qs = jnp.concatenate([q1, q2], axis=0)
    lane2 = lax.broadcasted_iota(jnp.int32, qs.shape, 1)
    q_low = qs
    q_up = jnp.where(lane2 >= DIFF_POS_LANE, -qs, qs)
    q_mid = jnp.where(lane2 >= DIFF_POS_LANE, jnp.zeros_like(qs), qs)

    col = lax.broadcasted_iota(jnp.int32, (1, 2 * tq), 1)
    qpos = qi * tq + jnp.where(col >= tq, col - tq, col)
    qpos_f = qpos.astype(F32)

    m_scr[...] = jnp.full(m_scr.shape, NEG_BIG, F32)
    l_scr[...] = jnp.zeros(l_scr.shape, F32)
    acc_scr[...] = jnp.zeros(acc_scr.shape, F32)

    def tile(kj, qv, row_term, diag):
        k0 = pl.multiple_of(kj * tk, tk)
        kt = k_ref[0, 0, pl.ds(k0, tk), :]
        s = lax.dot_general(kt, qv, (((1,), (1,)), ((), ())), preferred_element_type=F32)
        if diag:
            kpos = k0 + lax.broadcasted_iota(jnp.int32, (tk, 2 * tq), 0)
            s = s - slope * jnp.abs(qpos - kpos).astype(F32)
        m_old = m_scr[...]
        m_blk = jnp.max(s, axis=0, keepdims=True)
        if row_term is not None:
            m_blk = m_blk + row_term
        m_new = jnp.maximum(m_old, m_blk)
        alpha = jnp.exp(m_old - m_new)
        shift = m_new if row_term is None else m_new - row_term
        p = jnp.exp(s - shift)
        l_scr[...] = alpha * l_scr[...] + jnp.sum(p, axis=0, keepdims=True)
        vt = vt_ref[0, 0, :, pl.ds(k0, tk)]
        acc_scr[...] = alpha * acc_scr[...] + jnp.dot(vt, p.astype(BF16), preferred_element_type=F32)
        m_scr[...] = m_new

    n_low = (qi * tq) // tk
    n_diag_end = ((qi + 1) * tq + tk - 1) // tk

    def low_body(kj, c):
        tile(kj, q_low, -slope * qpos_f, False)
        return c

    def diag_body(kj, c):
        tile(kj, q_mid, None, True)
        return c

    def up_body(kj, c):
        tile(kj, q_up, slope * qpos_f, False)
        return c

    lax.fori_loop(0, n_low, low_body, 0)
    lax.fori_loop(n_low, n_diag_end, diag_body, 0)
    lax.fori_loop(n_diag_end, nk, up_body, 0)

    inv_l = 1.0 / l_scr[...]
    o2 = acc_scr[...] * inv_l
    o = o2[:, :tq] - lam * o2[:, tq:]
    ms = jnp.mean(o * o, axis=0, keepdims=True)
    o = o * lax.rsqrt(ms + RMS_EPS) * g_ref[...] * out_scale
    o_ref[0, 0] = o.astype(o_ref.dtype)


def diff_attention_pallas(dq, dk, dv, lq1, lk1, lq2, lk2, norm_g, lambda_init):
    B, S, _ = dq.shape
    H = DIFF_HEADS
    tq = min(DIFF_TQ, S)
    tk = min(DIFF_TK, S)
    slopes = jnp.exp2(-8.0 * jnp.arange(1, H + 1, dtype=F32) / H)
    lam = (jnp.exp(jnp.sum(lq1.astype(F32) * lk1.astype(F32)))
           - jnp.exp(jnp.sum(lq2.astype(F32) * lk2.astype(F32))) + lambda_init).reshape(1)
    pad = LANES - 2 * DIFF_DH - 2
    q4 = dq.astype(BF16).reshape(B, S, H, 2 * DIFF_DH).transpose(0, 2, 1, 3)
    q_aug = jnp.concatenate([q4, jnp.ones((B, H, S, 2), BF16), jnp.zeros((B, H, S, pad), BF16)], axis=-1)
    k4 = dk.astype(BF16).reshape(B, S, H, 2 * DIFF_DH).transpose(0, 2, 1, 3)
    j = jnp.arange(S, dtype=jnp.int32)
    jlo = (j & 255).astype(F32)
    jhi = (j - (j & 255)).astype(F32)
    kpos = jnp.stack([jlo, jhi], axis=-1)[None, None] * slopes[None, :, None, None]
    k_aug = jnp.concatenate([k4, jnp.broadcast_to(kpos, (B, H, S, 2)).astype(BF16),
                             jnp.zeros((B, H, S, pad), BF16)], axis=-1)
    vt = dv.astype(BF16).reshape(B, S, H, DIFF_DV).transpose(0, 2, 3, 1)
    g = norm_g.astype(F32).reshape(DIFF_DV, 1)

    kern = functools.partial(_diff_attn_kernel, tq=tq, tk=tk, seq=S, out_scale=1.0 - lambda_init)
    out = pl.pallas_call(
        kern,
        grid=(B, H, S // tq),
        in_specs=[
            pl.BlockSpec(memory_space=pltpu.SMEM),
            pl.BlockSpec(memory_space=pltpu.SMEM),
            pl.BlockSpec((1, 1, tq, LANES), lambda b, h, i: (b, h, i, 0)),
            pl.BlockSpec((1, 1, S, LANES), lambda b, h, i: (b, h, 0, 0)),
            pl.BlockSpec((1, 1, DIFF_DV, S), lambda b, h, i: (b, h, 0, 0)),
            pl.BlockSpec((DIFF_DV, 1), lambda b, h, i: (0, 0)),
        ],
        out_specs=pl.BlockSpec((1, 1, DIFF_DV, tq), lambda b, h, i: (b, h, 0, i)),
        out_shape=jax.ShapeDtypeStruct((B, H, DIFF_DV, S), F32),
        scratch_shapes=[
            pltpu.VMEM((1, 2 * tq), F32),
            pltpu.VMEM((1, 2 * tq), F32),
            pltpu.VMEM((DIFF_DV, 2 * tq), F32),
        ],
        compiler_params=pltpu.CompilerParams(
            dimension_semantics=("parallel", "parallel", "arbitrary"),
            vmem_limit_bytes=VMEM_LIMIT),
        name="diff_attn",
    )(slopes, lam, q_aug, k_aug, vt, g)
    return out.transpose(0, 3, 1, 2).reshape(B, S, DIFF_V_W)


MERGE_TM = 256


def _group_rms(o, ones_bd, g, width):
    ss = _dot_split(o * o, ones_bd, n=2)
    return o * lax.rsqrt(ss * (1.0 / width) + RMS_EPS) * g


def _gelu_tanh(x):
    return 0.5 * x * (1.0 + jnp.tanh(0.7978845608028654 * (x + 0.044715 * (x * x * x))))


def _merge_kernel(x_ref, gla_ref, gog_ref, lru_ref, lg_ref, dif_ref, gdn_ref, nz_ref,
                  mg_ref, glag_ref, gdng_ref, ones_ref, mw_ref, mb_ref, up_ref, wo_ref, o_ref):
    x = x_ref[...]
    h = _rms_rows(x, mg_ref[...]).astype(BF16)
    ones_bd = ones_ref[...]
    y_gla = _group_rms(gla_ref[0] + gla_ref[1], ones_bd, glag_ref[...], GLA_DV) * _silu(gog_ref[...].astype(F32))
    y_lru = (lru_ref[0] + lru_ref[1]) * _gelu_tanh(lg_ref[...].astype(F32))
    y_dif = dif_ref[...]
    y_gdn = _group_rms(gdn_ref[0] + gdn_ref[1], ones_bd, gdng_ref[...], GDN_DV) * _silu(nz_ref[...].astype(F32))
    merged = None
    for i, y in enumerate((y_gla, y_lru, y_dif, y_gdn)):
        gate = _sigmoid(jnp.dot(h, mw_ref[i], preferred_element_type=F32) + mb_ref[i])
        term = gate * jnp.dot(y.astype(BF16), up_ref[i], preferred_element_type=F32)
        merged = term if merged is None else merged + term
    o_ref[...] = x + jnp.dot(merged.astype(BF16), wo_ref[...], preferred_element_type=F32)


def merge_pallas(xt, gla_o, gog, lru_h, lg, y_diff, gdn_o, nz, mix_g, gla_g, gdn_g, merge_w, merge_b, branch_up,
                 mix_out):
    T, D = xt.shape
    tm = min(MERGE_TM, T)
    W = BRANCH_WIDTH
    r = np.arange(W) // GLA_DV
    ones_bd = jnp.asarray((r[:, None] == r[None, :]).astype(np.float32)).astype(BF16)
    tok = lambda w: pl.BlockSpec((tm, w), lambda i: (i, 0))
    tok2 = pl.BlockSpec((2, tm, W), lambda i: (0, i, 0))
    const = lambda shape: pl.BlockSpec(shape, lambda i: (0,) * len(shape), pipeline_mode=pl.Buffered(1))
    return pl.pallas_call(
        _merge_kernel,
        grid=(T // tm,),
        in_specs=[tok(D), tok2, tok(W), tok2, tok(W), tok(W), tok2, tok(W),
                  const((1, D)), const((1, W)), const((1, W)), const((W, W)),
                  const((N_BRANCHES, D, D)), const((N_BRANCHES, 1, D)), const((N_BRANCHES, W, D)), const((D, D))],
        out_specs=tok(D),
        out_shape=jax.ShapeDtypeStruct((T, D), F32),
        compiler_params=_cparams("parallel"),
        name="merge",
    )(xt, gla_o, gog, lru_h, lg, y_diff, gdn_o, nz,
      mix_g.reshape(1, D).astype(F32), jnp.tile(gla_g.astype(F32), GLA_HEADS).reshape(1, W),
      jnp.tile(gdn_g.astype(F32), GDN_HEADS).reshape(1, W), ones_bd,
      merge_w.astype(BF16), merge_b.astype(F32).reshape(N_BRANCHES, 1, D), branch_up.astype(BF16),
      mix_out.astype(BF16))


XATTN_TM = 512


def _kv_kernel(m_ref, g_ref, w_ref, o_ref):
    mn = _rms_rows(m_ref[0], g_ref[...]).astype(BF16)
    o_ref[0] = jnp.dot(mn, w_ref[...], preferred_element_type=F32).astype(BF16)


def _xattn_kernel(x_ref, g_ref, wq_ref, kv_ref, wo_ref, o_ref):
    x = x_ref[0]
    h = _rms_rows(x, g_ref[...]).astype(BF16)
    q = jnp.dot(h, wq_ref[...], preferred_element_type=F32).astype(BF16)
    outs = []
    for hh in range(XATTN_HEADS):
        lo = hh * XATTN_DH
        k = kv_ref[0, :, lo:lo + XATTN_DH]
        v = kv_ref[0, :, D_MODEL + lo:D_MODEL + lo + XATTN_DH]
        s = _dot_nt(q[:, lo:lo + XATTN_DH], k) * XATTN_DH ** -0.5
        p = jnp.exp(s - jnp.max(s, axis=-1, keepdims=True))
        p = p / jnp.sum(p, axis=-1, keepdims=True)
        outs.append(jnp.dot(p.astype(BF16), v, preferred_element_type=F32).astype(BF16))
    o = jnp.concatenate(outs, axis=-1)
    o_ref[0] = x + jnp.dot(o, wo_ref[...], preferred_element_type=F32)


def xattn_pallas(x, mem, xg, mg, wq, wkv, wo):
    B, S, D = x.shape
    M = mem.shape[1]
    tm = min(XATTN_TM, S)
    kv = pl.pallas_call(
        _kv_kernel,
        grid=(B,),
        in_specs=[pl.BlockSpec((1, M, D), lambda b: (b, 0, 0)),
                  pl.BlockSpec((1, D), lambda b: (0, 0)),
                  pl.BlockSpec((D, 2 * D), lambda b: (0, 0))],
        out_specs=pl.BlockSpec((1, M, 2 * D), lambda b: (b, 0, 0)),
        out_shape=jax.ShapeDtypeStruct((B, M, 2 * D), BF16),
        compiler_params=_cparams("parallel"),
        name="xattn_kv",
    )(mem, mg.reshape(1, D).astype(F32), wkv.astype(BF16))
    const = lambda shape: pl.BlockSpec(shape, lambda b, i: (0,) * len(shape), pipeline_mode=pl.Buffered(1))
    return pl.pallas_call(
        _xattn_kernel,
        grid=(B, S // tm),
        in_specs=[pl.BlockSpec((1, tm, D), lambda b, i: (b, i, 0)),
                  const((1, D)), const((D, D)),
                  pl.BlockSpec((1, M, 2 * D), lambda b, i: (b, 0, 0)),
                  const((D, D))],
        out_specs=pl.BlockSpec((1, tm, D), lambda b, i: (b, i, 0)),
        out_shape=jax.ShapeDtypeStruct((B, S, D), F32),
        compiler_params=_cparams("parallel", "parallel"),
        name="xattn",
    )(x, xg.reshape(1, D).astype(F32), wq.astype(BF16), kv, wo.astype(BF16))


ROUTE_TM = 512
ROUTE_EXPERT_LANE = 32
MOE_BM = 256
COMBINE_TM = 512


def _router_kernel(x_ref, g_ref, w_ref, b_ref, h_ref, r_ref):
    h = _rms_rows(x_ref[...], g_ref[...])
    h_ref[...] = h
    logits = _dot_f32(h, w_ref[...]) + b_ref[...]
    lane = lax.broadcasted_iota(jnp.int32, logits.shape, 1)
    big = jnp.int32(1 << 20)
    neg = jnp.float32(-jnp.inf)

    def top(vals):
        m = jnp.max(vals, axis=-1, keepdims=True)
        idx = jnp.min(jnp.where(vals == m, lane, big), axis=-1, keepdims=True)
        return m, idx

    is_g = lane < N_GROUPS
    gmax, gidx = top(jnp.where(is_g, logits, neg))
    gsum = jnp.sum(jnp.where(is_g, jnp.exp(logits - gmax), 0.0), axis=-1, keepdims=True)
    g_w = 1.0 / gsum
    lo = ROUTE_EXPERT_LANE + EXPERTS_PER_GROUP * gidx
    in_grp = jnp.abs(2 * (lane - lo) - (EXPERTS_PER_GROUP - 1)) < EXPERTS_PER_GROUP
    el = jnp.where(in_grp, logits, neg)
    e1, i1 = top(el)
    e2, i2 = top(jnp.where(lane == i1, neg, el))
    t = jnp.exp(e2 - e1)
    w1 = g_w / (1.0 + t)
    w2 = g_w * t / (1.0 + t)
    f = lambda v: v.astype(F32)
    r_ref[...] = jnp.where(lane == 0, f(i1 - ROUTE_EXPERT_LANE),
                           jnp.where(lane == 1, f(i2 - ROUTE_EXPERT_LANE),
                                     jnp.where(lane == 2, w1, jnp.where(lane == 3, w2, 0.0))))


def router_pallas(xt, g, w_group, b_group, w_expert, b_expert):
    T, D = xt.shape
    tm = min(ROUTE_TM, T)
    w = jnp.zeros((D, LANES), F32).at[:, :N_GROUPS].set(w_group.astype(F32))
    w = w.at[:, ROUTE_EXPERT_LANE:ROUTE_EXPERT_LANE + N_EXPERTS].set(w_expert.astype(F32))
    b = jnp.zeros((1, LANES), F32).at[0, :N_GROUPS].set(b_group.astype(F32))
    b = b.at[0, ROUTE_EXPERT_LANE:ROUTE_EXPERT_LANE + N_EXPERTS].set(b_expert.astype(F32))
    return pl.pallas_call(
        _router_kernel,
        grid=(T // tm,),
        in_specs=[pl.BlockSpec((tm, D), lambda i: (i, 0)),
                  pl.BlockSpec((1, D), lambda i: (0, 0)),
                  pl.BlockSpec((D, LANES), lambda i: (0, 0)),
                  pl.BlockSpec((1, LANES), lambda i: (0, 0))],
        out_specs=[pl.BlockSpec((tm, D), lambda i: (i, 0)), pl.BlockSpec((tm, LANES), lambda i: (i, 0))],
        out_shape=[jax.ShapeDtypeStruct((T, D), F32), jax.ShapeDtypeStruct((T, LANES), F32)],
        compiler_params=_cparams("parallel"),
        name="moe_router",
    )(xt, g.reshape(1, D).astype(F32), w, b)


def _gather_rows(idx_hbm_row, src_hbm, idx_smem, dst, sem_idx, sem_rows, n):
    cp = pltpu.make_async_copy(idx_hbm_row, idx_smem, sem_idx)
    cp.start()
    cp.wait()

    def row_copy(r):
        return pltpu.make_async_copy(src_hbm.at[pl.ds(idx_smem[r], 1)], dst.at[pl.ds(r, 1)], sem_rows)

    def issue(r, c):
        row_copy(r).start()
        return c

    def drain(r, c):
        row_copy(r).wait()
        return c

    lax.fori_loop(0, n, issue, 0)
    lax.fori_loop(0, n, drain, 0)


def _expert_kernel(be_ref, nu_ref, idx_hbm, h_hbm, w1_ref, w3_ref, w2_ref, y_ref, idx_smem, xbuf, sem_idx, sem_rows,
                   *, bm):
    i = pl.program_id(0)

    @pl.when(i < nu_ref[0])
    def _():
        _gather_rows(idx_hbm.at[i], h_hbm, idx_smem, xbuf, sem_idx, sem_rows, bm)
        xb = xbuf[...].astype(BF16)
        a = jnp.dot(xb, w1_ref[0], preferred_element_type=F32)
        g = jnp.dot(xb, w3_ref[0], preferred_element_type=F32)
        y_ref[...] = jnp.dot((_silu(a) * g).astype(BF16), w2_ref[0], preferred_element_type=F32)

    @pl.when(i >= nu_ref[0])
    def _():
        y_ref[...] = jnp.zeros(y_ref.shape, F32)


def _combine_kernel(x_ref, r_ref, pos_hbm, y_hbm, g_ref, o_ref, idx_smem, ybuf, sem_idx, sem_rows, *, tm, final):
    i = pl.program_id(0)
    _gather_rows(pos_hbm.at[i], y_hbm, idx_smem, ybuf, sem_idx, sem_rows, 2 * tm)
    r = r_ref[...]
    out = x_ref[...] + r[:, 2:3] * ybuf[0:tm] + r[:, 3:4] * ybuf[tm:2 * tm]
    if final:
        out = _rms_rows(out, g_ref[...])
    o_ref[...] = out


def moe_pallas(xt, g, w_group, b_group, w_expert, b_expert, w1, w3, w2, final_g=None):
    T, D = xt.shape
    bm = MOE_BM
    h, route = router_pallas(xt, g, w_group, b_group, w_expert, b_expert)
    eid = route[:, 0:2].astype(jnp.int32).reshape(-1)
    A = T * TOP_K
    n_blk = A // bm + N_EXPERTS
    P = n_blk * bm
    order = jnp.argsort(eid)
    eid_s = eid[order]
    tok_s = order // TOP_K
    counts = jnp.zeros((N_EXPERTS,), jnp.int32).at[eid].add(1)
    padded = (counts + bm - 1) // bm * bm
    pad_end = jnp.cumsum(padded)
    pad_start = pad_end - padded
    seg_start = jnp.cumsum(counts) - counts
    dest = pad_start[eid_s] + jnp.arange(A, dtype=jnp.int32) - seg_start[eid_s]
    row_tok = jnp.zeros((P,), jnp.int32).at[dest].set(tok_s.astype(jnp.int32))
    pos = jnp.zeros((A,), jnp.int32).at[order].set(dest)
    blk_expert = jnp.minimum(
        jnp.searchsorted(pad_end, jnp.arange(n_blk, dtype=jnp.int32) * bm, side='right'), N_EXPERTS - 1
    ).astype(jnp.int32)
    n_used = (pad_end[-1] // bm).astype(jnp.int32).reshape(1)

    y = pl.pallas_call(
        functools.partial(_expert_kernel, bm=bm),
        grid_spec=pltpu.PrefetchScalarGridSpec(
            num_scalar_prefetch=2,
            grid=(n_blk,),
            in_specs=[pl.BlockSpec(memory_space=pl.ANY),
                      pl.BlockSpec(memory_space=pl.ANY),
                      pl.BlockSpec((1, D, D_EXPERT), lambda i, be, nu: (be[i], 0, 0)),
                      pl.BlockSpec((1, D, D_EXPERT), lambda i, be, nu: (be[i], 0, 0)),
                      pl.BlockSpec((1, D_EXPERT, D), lambda i, be, nu: (be[i], 0, 0))],
            out_specs=pl.BlockSpec((bm, D), lambda i, be, nu: (i, 0)),
            scratch_shapes=[pltpu.SMEM((bm,), jnp.int32), pltpu.VMEM((bm, D), F32),
                            pltpu.SemaphoreType.DMA(()), pltpu.SemaphoreType.DMA(())]),
        out_shape=jax.ShapeDtypeStruct((P, D), F32),
        compiler_params=_cparams("arbitrary"),
        name="moe_experts",
    )(blk_expert, n_used, row_tok.reshape(n_blk, bm), h, w1.astype(BF16), w3.astype(BF16), w2.astype(BF16))

    tm = min(COMBINE_TM, T)
    nt = T // tm
    pos_t = pos.reshape(nt, tm, TOP_K).transpose(0, 2, 1).reshape(nt, TOP_K * tm)
    fg = (final_g if final_g is not None else jnp.ones((D,), F32)).reshape(1, D).astype(F32)
    return pl.pallas_call(
        functools.partial(_combine_kernel, tm=tm, final=final_g is not None),
        grid=(nt,),
        in_specs=[pl.BlockSpec((tm, D), lambda i: (i, 0)),
                  pl.BlockSpec((tm, LANES), lambda i: (i, 0)),
                  pl.BlockSpec(memory_space=pl.ANY),
                  pl.BlockSpec(memory_space=pl.ANY),
                  pl.BlockSpec((1, D), lambda i: (0, 0))],
        out_specs=pl.BlockSpec((tm, D), lambda i: (i, 0)),
        out_shape=jax.ShapeDtypeStruct((T, D), F32),
        scratch_shapes=[pltpu.SMEM((TOP_K * tm,), jnp.int32), pltpu.VMEM((TOP_K * tm, D), F32),
                        pltpu.SemaphoreType.DMA(()), pltpu.SemaphoreType.DMA(())],
        compiler_params=_cparams("arbitrary"),
        name="moe_combine",
    )(xt, route, pos_t, y, fg)


GDN_TB = 512
GDN_INV_PASSES = 3
GDN_UNROLL_B = 2


def _gdn_prep_kernel(prev_ref, cur_ref, next_ref, ba_ref, cw_ref, ones_ref, alog_ref, dt_ref, qkv_ref, bg_ref, *, nt):
    qkv = _silu(_conv4(prev_ref, cur_ref, next_ref, cw_ref[...], nt))
    ones_bd = ones_ref[...]
    w = GDN_QK_W
    q = qkv[:, 0:w]
    k = qkv[:, w:2 * w]
    qn = q * lax.rsqrt(_dot_split(q * q, ones_bd, n=2) + 1e-6) * GDN_DK ** -0.5
    kn = k * lax.rsqrt(_dot_split(k * k, ones_bd, n=2) + 1e-6)
    qkv_ref[0] = jnp.concatenate([qn, kn, qkv[:, 2 * w:]], axis=-1)
    ba = ba_ref[0]
    beta = _sigmoid(ba)
    log_a = -jnp.exp(alog_ref[...]) * _softplus(ba + dt_ref[...])
    lane = lax.broadcasted_iota(jnp.int32, ba.shape, 1)
    h = GDN_HEADS
    for d in range(2):
        b_d = pltpu.roll(beta, (LANES - d * h) % LANES, 1)
        a_d = pltpu.roll(log_a, (LANES - (2 * h + d * h) + h) % LANES, 1)
        bg_ref[d, 0] = jnp.where(lane < h, b_d, jnp.where(lane < 2 * h, a_d, 0.0))


def _gdn_chunk_kernel(qkv_ref, bg_ref, tri_ref, tribd_ref, o_ref, st_ref, *, nb):
    c = GDN_CHUNK
    H = GDN_HEADS
    n = H * c

    @pl.when(pl.program_id(1) == 0)
    def _():
        st_ref[...] = jnp.zeros(st_ref.shape, F32)

    tri_bf = tri_ref[0].astype(BF16)
    incl = tribd_ref[0]
    ri = lax.broadcasted_iota(jnp.int32, (n, n), 0)
    ci = lax.broadcasted_iota(jnp.int32, (n, n), 1)
    eye = jnp.where(ri == ci, 1.0, 0.0)
    strict = incl - eye
    bdmask = jnp.where(ri // c == ci // c, 1.0, 0.0)

    def stack(x):
        w = x.shape[1] // H
        return jnp.concatenate([x[:, hh * w:(hh + 1) * w] for hh in range(H)], axis=0)

    def col(x, lane0):
        return jnp.concatenate([x[:, lane0 + hh:lane0 + hh + 1] for hh in range(H)], axis=0)

    def bd(x_st):
        return jnp.concatenate([x_st] * H, axis=1) * bdmask

    def mm(a, b, passes, dot=_dot_nn):
        if passes == 1:
            return dot(a.astype(BF16), b.astype(BF16))
        return _dot_f32(a, b, dot=dot)

    def one_batch(b):
        qkv = qkv_ref[b]
        bg = bg_ref[0, b]
        q_st = stack(qkv[:, 0:GDN_QK_W])
        k_st = stack(qkv[:, GDN_QK_W:2 * GDN_QK_W])
        v_st = stack(qkv[:, 2 * GDN_QK_W:])
        acc = None
        for p in _split_bf16(bg, 3):
            t = _dot_nn(tri_bf, p)
            acc = t if acc is None else acc + t
        gam = acc
        tot = jnp.sum(bg, axis=0, keepdims=True)
        g_col = col(gam, H)
        b_col = col(bg, 0)
        gam_t = jnp.concatenate([gam, jnp.zeros_like(gam)], axis=0).T
        g_row = jnp.concatenate([gam_t[H + hh:H + hh + 1, 0:c] for hh in range(H)], axis=1)
        end_col = jnp.concatenate([jnp.broadcast_to(tot[:, H + hh:H + hh + 1], (c, 1)) for hh in range(H)], axis=0)
        exp_g = jnp.exp(g_col)
        decay = jnp.exp(jnp.minimum(g_col - g_row, 0.0)) * incl
        k_bf = k_st.astype(BF16)
        kk = _dot_nt(k_bf, k_bf)
        a_mat = strict * b_col * kk * decay
        p_inv = eye - a_mat
        x_pow = a_mat
        for _ in range(int(math.log2(c)) - 1):
            x_pow = mm(x_pow, x_pow, GDN_INV_PASSES)
            p_inv = p_inv + mm(p_inv, x_pow, GDN_INV_PASSES)
        rhs = jnp.concatenate([v_st * b_col, k_st * (b_col * exp_g)], axis=1)
        sol = mm(p_inv, rhs, GDN_INV_PASSES)
        u_st = sol[:, 0:c]
        kc_bd = bd(sol[:, c:2 * c])
        qk = _dot_nt(q_st.astype(BF16), k_bf) * decay
        qd_bd = bd(q_st * exp_g)
        kd_bd = bd(k_st * jnp.exp(end_col - g_col))
        st = st_ref[b]
        st_bf = st.astype(BF16)
        v_new = u_st - _dot_nn(kc_bd.astype(BF16), st_bf)
        v_new_bf = v_new.astype(BF16)
        o_st = _dot_nn(qd_bd.astype(BF16), st_bf) + _dot_nn(qk.astype(BF16), v_new_bf)
        st_ref[b] = jnp.exp(end_col) * st + _dot_tn(kd_bd.astype(BF16), v_new_bf)
        o_ref[0, b] = jnp.concatenate([o_st[hh * c:(hh + 1) * c] for hh in range(H)], axis=1)

    def pair(i, carry):
        for j in range(GDN_UNROLL_B):
            one_batch(i * GDN_UNROLL_B + j)
        return carry

    lax.fori_loop(0, nb // GDN_UNROLL_B, pair, 0)


def gdn_pallas(nqkv, nba, conv_w, a_log, dt_bias):
    B, S, W = nqkv.shape
    tb = min(GDN_TB, S)
    nt = S // tb
    c = GDN_CHUNK
    nc = S // c
    H = GDN_HEADS
    r = np.arange(GDN_QK_W) // GDN_DK
    ones_bd = jnp.asarray((r[:, None] == r[None, :]).astype(np.float32)).astype(BF16)
    alog = jnp.zeros((1, LANES), F32).at[0, 2 * H:4 * H].set(a_log.astype(F32).reshape(-1))
    dt = jnp.zeros((1, LANES), F32).at[0, 2 * H:4 * H].set(dt_bias.astype(F32).reshape(-1))
    full = lambda shape: pl.BlockSpec(shape, lambda b, i: (0,) * len(shape))
    qkvn, bg = pl.pallas_call(
        functools.partial(_gdn_prep_kernel, nt=nt),
        grid=(B, nt),
        in_specs=_halo_specs(tb, W, nt) + [pl.BlockSpec((1, tb, LANES), lambda b, i: (b, i, 0)),
                                           full((CONV_WIDTH, W)), full((GDN_QK_W, GDN_QK_W)),
                                           full((1, LANES)), full((1, LANES))],
        out_specs=[pl.BlockSpec((1, tb, W), lambda b, i: (b, i, 0)),
                   pl.BlockSpec((2, 1, tb, LANES), lambda b, i: (0, b, i, 0))],
        out_shape=[jax.ShapeDtypeStruct((B, S, W), F32), jax.ShapeDtypeStruct((2, B, S, LANES), F32)],
        compiler_params=_cparams("parallel", "parallel"),
        name="gdn_prep",
    )(nqkv, nqkv, nqkv, nba, conv_w.astype(F32), ones_bd, alog, dt)

    t = np.arange(c)
    tri_np = np.stack([t[:, None] >= t[None, :], t[:, None] <= t[None, :]]).astype(np.float32)
    tri = jnp.asarray(tri_np)
    tri_bd = jnp.asarray(np.stack([np.kron(np.eye(H, dtype=np.float32), tri_np[d]) for d in range(2)]))

    def tmap(d, i):
        return i + d * (nc - 1 - 2 * i)

    return pl.pallas_call(
        functools.partial(_gdn_chunk_kernel, nb=B),
        grid=(2, nc),
        in_specs=[pl.BlockSpec((B, c, W), lambda d, i: (0, tmap(d, i), 0)),
                  pl.BlockSpec((1, B, c, LANES), lambda d, i: (d, 0, tmap(d, i), 0)),
                  pl.BlockSpec((1, c, c), lambda d, i: (d, 0, 0)),
                  pl.BlockSpec((1, H * c, H * c), lambda d, i: (d, 0, 0))],
        out_specs=pl.BlockSpec((1, B, c, GDN_V_W), lambda d, i: (d, 0, tmap(d, i), 0)),
        out_shape=jax.ShapeDtypeStruct((2, B, S, GDN_V_W), F32),
        scratch_shapes=[pltpu.VMEM((B, H * GDN_DK, GDN_DV), F32)],
        compiler_params=_cparams("arbitrary", "arbitrary"),
        name="gdn_chunk",
    )(qkvn, bg, tri, tri_bd)


def gdn_scan(q, k, v, beta, log_a):
    b_, s_, h_, dk = q.shape
    dv = v.shape[-1]
    c = GDN_CHUNK
    qc, kc, vc, bc, gc = (_to_chunks(t, c) for t in (q, k, v, beta, log_a))
    gam = jnp.cumsum(gc, axis=-1)
    diff = gam[..., :, None] - gam[..., None, :]
    incl = jnp.tril(jnp.ones((c, c), dtype=bool))
    strict = jnp.tril(jnp.ones((c, c), dtype=bool), k=-1)
    decay = jnp.exp(jnp.where(incl, diff, -jnp.inf))
    kk = jnp.einsum('bhntd,bhnsd->bhnts', kc, kc)
    a_mat = jnp.where(strict, bc[..., None] * kk * decay, 0.0) + jnp.eye(c, dtype=jnp.float32)
    rhs = jnp.concatenate([vc * bc[..., None], kc * (bc * jnp.exp(gam))[..., None]], axis=-1)
    sol = lax.linalg.triangular_solve(a_mat, rhs, left_side=True, lower=True, unit_diagonal=True)
    u_val, k_cum = sol[..., :dv], sol[..., dv:]
    qk = jnp.einsum('bhntd,bhnsd->bhnts', qc, kc) * decay
    q_dec = qc * jnp.exp(gam)[..., None]
    k_dec = kc * jnp.exp(gam[..., -1:] - gam)[..., None]
    c_dec = jnp.exp(gam[..., -1])

    def step(state, inp):
        u_i, kc_i, qk_i, qd_i, kd_i, cd_i = inp
        v_new = u_i - jnp.einsum('bhtd,bhde->bhte', kc_i, state)
        o = jnp.einsum('bhtd,bhde->bhte', qd_i, state) + jnp.einsum('bhts,bhse->bhte', qk_i, v_new)
        state = cd_i[..., None, None] * state + jnp.einsum('bhsd,bhse->bhde', kd_i, v_new)
        return state, o

    xs = tuple(jnp.moveaxis(t, 2, 0) for t in (u_val, k_cum, qk, q_dec, k_dec, c_dec))
    _, o = lax.scan(step, jnp.zeros((b_, h_, dk, dv), jnp.float32), xs)
    return _from_chunks(o)


def gdn_mixer(qkv, z, ba, conv_w, a_log, dt_bias, norm_g):
    B, S, _ = qkv.shape
    f32 = jnp.float32
    qkv = jax.nn.silu(dwconv_centred(qkv, conv_w)).astype(f32)
    q, k, v = jnp.split(qkv, [GDN_QK_W, 2 * GDN_QK_W], axis=-1)
    q = l2norm(q.reshape(B, S, GDN_HEADS, GDN_DK)) * GDN_DK ** -0.5
    k = l2norm(k.reshape(B, S, GDN_HEADS, GDN_DK))
    v = v.reshape(B, S, GDN_HEADS, GDN_DV)
    ba = ba.astype(f32).reshape(B, S, 4, GDN_HEADS)
    beta = jax.nn.sigmoid(ba[:, :, :2])
    log_a = -jnp.exp(a_log.astype(f32)) * jax.nn.softplus(ba[:, :, 2:] + dt_bias.astype(f32))
    o_f = gdn_scan(q, k, v, beta[:, :, 0], log_a[:, :, 0])
    o_b = _flip(gdn_scan(_flip(q), _flip(k), _flip(v), _flip(beta[:, :, 1]), _flip(log_a[:, :, 1])))
    o = rmsnorm(o_f + o_b, norm_g) * jax.nn.silu(z.astype(f32)).reshape(B, S, GDN_HEADS, GDN_DV)
    return o.reshape(B, S, GDN_V_W).astype(z.dtype)


def gated_merge(h, branches, merge_w, merge_b, branch_up):
    out = None
    for i, y in enumerate(branches):
        term = jax.nn.sigmoid(h @ merge_w[i] + merge_b[i]) * (y @ branch_up[i])
        out = term if out is None else out + term
    return out


def memory_cross_attention(h, mem_n, w_q, w_kv, w_o):
    B, S, D = h.shape
    M = mem_n.shape[1]
    q = (h @ w_q).reshape(B, S, XATTN_HEADS, XATTN_DH)
    kv = (mem_n @ w_kv).reshape(B, M, 2, XATTN_HEADS, XATTN_DH)
    k, v = kv[:, :, 0], kv[:, :, 1]
    s = jnp.einsum('bshd,bmhd->bhsm', q, k).astype(jnp.float32) * XATTN_DH ** -0.5
    p = jax.nn.softmax(s, axis=-1)
    o = jnp.einsum('bhsm,bmhd->bshd', p.astype(v.dtype), v).reshape(B, S, D)
    return o @ w_o


def hier_moe(h, w_group, b_group, w_expert, b_expert, w1, w3, w2):
    B, S, D = h.shape
    T = B * S
    f32 = jnp.float32
    ht = h.reshape(T, D)
    g_logits = (ht @ w_group).astype(f32) + b_group.astype(f32)
    g_prob = jax.nn.softmax(g_logits, axis=-1)
    _, g_idx = lax.top_k(g_logits, 1)
    g_w = jnp.take_along_axis(g_prob, g_idx, axis=1)
    e_logits = ((ht @ w_expert).astype(f32) + b_expert.astype(f32)).reshape(T, N_GROUPS, EXPERTS_PER_GROUP)
    e_logits = jnp.take_along_axis(
        e_logits, jnp.broadcast_to(g_idx[:, :, None], (T, 1, EXPERTS_PER_GROUP)), axis=1)[:, 0]
    e_top, e_idx = lax.top_k(e_logits, TOP_K)
    gate = jax.nn.softmax(e_top, axis=-1) * g_w
    eid = (g_idx * EXPERTS_PER_GROUP + e_idx).reshape(-1)
    tok = jnp.repeat(jnp.arange(T, dtype=jnp.int32), TOP_K)
    wts = gate.reshape(-1)
    A = T * TOP_K
    n_blk = -(-A // MOE_BLOCK) + N_EXPERTS
    P = n_blk * MOE_BLOCK
    order = jnp.argsort(eid)
    eid_s, tok_s, w_s = eid[order], tok[order], wts[order]
    counts = jnp.zeros((N_EXPERTS,), jnp.int32).at[eid].add(1)
    padded = (counts + MOE_BLOCK - 1) // MOE_BLOCK * MOE_BLOCK
    pad_end = jnp.cumsum(padded)
    pad_start = pad_end - padded
    seg_start = jnp.cumsum(counts) - counts
    dest = pad_start[eid_s] + jnp.arange(A, dtype=jnp.int32) - seg_start[eid_s]
    row_tok = jnp.full((P,), T, jnp.int32).at[dest].set(tok_s)
    row_w = jnp.zeros((P,), f32).at[dest].set(w_s)
    blk_expert = jnp.minimum(
        jnp.searchsorted(pad_end, jnp.arange(n_blk, dtype=jnp.int32) * MOE_BLOCK, side='right'),
        N_EXPERTS - 1)
    h_pad = jnp.concatenate([ht, jnp.zeros((1, D), ht.dtype)], axis=0)
    xb = h_pad[row_tok].reshape(n_blk, MOE_BLOCK, D)

    def expert_block(args):
        xi, e = args
        return (jax.nn.silu(xi @ w1[e]) * (xi @ w3[e])) @ w2[e]

    yb = lax.map(expert_block, (xb, blk_expert)).reshape(P, D)
    out = jnp.zeros((T + 1, D), h.dtype).at[row_tok].add(yb * row_w[:, None].astype(yb.dtype))
    return out[:T].reshape(B, S, D)


def kernel(x, mem, mix_norm, w_in, gla_lr_up, gla_lr_bias, gla_norm, lru_conv_w, lru_conv_b, lru_w_a, lru_b_a, lru_w_x, lru_b_x, lru_lambda, diff_lq1, diff_lk1, diff_lq2, diff_lk2, diff_norm, gdn_conv_w, gdn_a_log, gdn_dt_bias, gdn_norm, merge_w, merge_b, branch_up, mix_out, xattn_norm, mem_norm, xattn_wq, xattn_wkv, xattn_wo, moe_norm, moe_w_group, moe_b_group, moe_w_expert, moe_b_expert, moe_w1, moe_w3, moe_w2, final_norm):
    B, S, D = x.shape
    T = B * S
    xt = x.reshape(T, D)
    for l in range(DEPTH):
        gla_in, gog, lx, lg, dq, dk, dv, nqkv, nz, nba = inproj_pallas(xt, mix_norm[l], w_in[l])
        gla_o = gla_pallas(gla_in.reshape(B, S, -1), gla_lr_up[l], gla_lr_bias[l])
        lru_h = lru_pallas(lx.reshape(B, S, -1), lru_conv_w[l], lru_conv_b[l], lru_w_a[l], lru_b_a[l],
                           lru_w_x[l], lru_b_x[l], lru_lambda[l])
        y_diff = diff_attention_pallas(dq.reshape(B, S, -1), dk.reshape(B, S, -1), dv.reshape(B, S, -1),
                                       diff_lq1[l], diff_lk1[l], diff_lq2[l], diff_lk2[l], diff_norm[l],
                                       0.8 - 0.6 * math.exp(-0.3 * l))
        gdn_o = gdn_pallas(nqkv.reshape(B, S, -1), nba.reshape(B, S, -1), gdn_conv_w[l], gdn_a_log[l],
                           gdn_dt_bias[l])
        xt = merge_pallas(xt, gla_o.reshape(2, T, -1), gog, lru_h.reshape(2, T, -1), lg, y_diff.reshape(T, -1),
                          gdn_o.reshape(2, T, -1), nz, mix_norm[l], gla_norm[l], gdn_norm[l],
                          merge_w[l], merge_b[l], branch_up[l], mix_out[l])
        xt = xattn_pallas(xt.reshape(B, S, D), mem, xattn_norm[l], mem_norm[l], xattn_wq[l], xattn_wkv[l],
                          xattn_wo[l]).reshape(T, D)
        xt = moe_pallas(xt, moe_norm[l], moe_w_group[l], moe_b_group[l], moe_w_expert[l], moe_b_expert[l],
                        moe_w1[l], moe_w3[l], moe_w2[l], final_g=final_norm if l == DEPTH - 1 else None)
    return xt.reshape(B, S, D)
```

```python
import functools
import math

import jax
import jax.numpy as jnp
import numpy as np
from jax import lax
from jax.experimental import pallas as pl
from jax.experimental.pallas import tpu as pltpu

D_MODEL = 1024
DEPTH = 2
N_BRANCHES = 4
BRANCH_WIDTH = D_MODEL // 4
RMS_EPS = 1e-6
CONV_WIDTH = 4

GLA_HEADS = 4
GLA_DV = BRANCH_WIDTH // GLA_HEADS
GLA_DK = GLA_DV // 2
GLA_RANK = 16
GLA_GATE_NORM = 16.0
GLA_CHUNK = 64
GLA_QK_W = GLA_HEADS * GLA_DK
GLA_V_W = GLA_HEADS * GLA_DV

LRU_WIDTH = BRANCH_WIDTH
LRU_BLOCKS = 4
LRU_C = 8.0

DIFF_HEADS = 4
DIFF_DV = BRANCH_WIDTH // DIFF_HEADS
DIFF_DH = DIFF_DV // 2
DIFF_QK_W = DIFF_HEADS * 2 * DIFF_DH
DIFF_V_W = DIFF_HEADS * DIFF_DV

GDN_HEADS = 4
GDN_DK = BRANCH_WIDTH // GDN_HEADS
GDN_DV = BRANCH_WIDTH // GDN_HEADS
GDN_CHUNK = 64
GDN_QK_W = GDN_HEADS * GDN_DK
GDN_V_W = GDN_HEADS * GDN_DV

IN_SPLITS = (GLA_QK_W, GLA_QK_W, GLA_V_W, GLA_V_W, 2 * GLA_RANK,
             LRU_WIDTH, LRU_WIDTH,
             DIFF_QK_W, DIFF_QK_W, DIFF_V_W,
             2 * GDN_QK_W + GDN_V_W, GDN_V_W, 4 * GDN_HEADS)

XATTN_HEADS = 4
XATTN_DH = D_MODEL // XATTN_HEADS

N_GROUPS = 4
EXPERTS_PER_GROUP = 8
N_EXPERTS = N_GROUPS * EXPERTS_PER_GROUP
TOP_K = 2
D_EXPERT = D_MODEL // 2
MOE_BLOCK = 128

LANES = 128
VMEM_LIMIT = 56 * 1024 * 1024

F32 = jnp.float32
BF16 = jnp.bfloat16
NEG_BIG = -1e30


def rmsnorm(x, g):
    xf = x.astype(jnp.float32)
    y = xf * lax.rsqrt(jnp.mean(xf * xf, axis=-1, keepdims=True) + RMS_EPS)
    return (y * g.astype(jnp.float32)).astype(x.dtype)


def l2norm(t):
    return t * lax.rsqrt(jnp.sum(t * t, axis=-1, keepdims=True) + 1e-6)


def _flip(t):
    return jnp.flip(t, axis=1)


def dwconv_centred(x, w):
    k = w.shape[0]
    return lax.conv_general_dilated(
        x, w[:, None, :].astype(x.dtype), window_strides=(1,),
        padding=[(k // 2, k - 1 - k // 2)],
        dimension_numbers=('NWC', 'WIO', 'NWC'),
        feature_group_count=x.shape[-1])


def _to_chunks(t, chunk):
    b, s, h = t.shape[:3]
    t = t.reshape((b, s // chunk, chunk, h) + t.shape[3:])
    return jnp.moveaxis(t, 3, 1)


def _from_chunks(o):
    nc, b, h, c, d = o.shape
    return o.transpose(1, 0, 3, 2, 4).reshape(b, nc * c, h, d)


def _linrec_combine(c1, c2):
    a1, b1 = c1
    a2, b2 = c2
    return a1 * a2, a2 * b1 + b2


def gla_scan(q, k, v, log_f):
    b_, s_, h_, dk = q.shape
    dv = v.shape[-1]
    c = GLA_CHUNK
    incl = jnp.tril(jnp.ones((c, c), dtype=bool))[:, :, None]

    def step(state, inp):
        q_i, k_i, v_i, g_i = inp
        bcum = jnp.cumsum(g_i, axis=2)
        decay = jnp.exp(jnp.where(incl, bcum[:, :, :, None, :] - bcum[:, :, None, :, :], -jnp.inf))
        scores = jnp.einsum('bhtd,bhsd,bhtsd->bhts', q_i, k_i, decay)
        o = (jnp.einsum('bhts,bhse->bhte', scores, v_i)
             + jnp.einsum('bhtd,bhde->bhte', q_i * jnp.exp(bcum), state))
        b_end = bcum[:, :, -1:, :]
        state = (jnp.exp(b_end)[:, :, 0, :, None] * state
                 + jnp.einsum('bhsd,bhse->bhde', k_i * jnp.exp(b_end - bcum), v_i))
        return state, o

    xs = tuple(jnp.moveaxis(_to_chunks(t, c), 2, 0) for t in (q, k, v, log_f))
    _, o = lax.scan(step, jnp.zeros((b_, h_, dk, dv), jnp.float32), xs)
    return _from_chunks(o)


def gla_mixer(q, k, v, og, lr, lr_up, lr_bias, norm_g):
    B, S, _ = q.shape
    f32 = jnp.float32
    qh = q.astype(f32).reshape(B, S, GLA_HEADS, GLA_DK) * GLA_DK ** -0.5
    kh = k.astype(f32).reshape(B, S, GLA_HEADS, GLA_DK)
    vh = v.astype(f32).reshape(B, S, GLA_HEADS, GLA_DV)
    lr = lr.astype(f32).reshape(B, S, 2, GLA_RANK)
    z = jnp.einsum('bsdr,drk->bsdk', lr, lr_up.astype(f32)) + lr_bias.astype(f32)
    log_f = (jax.nn.log_sigmoid(z) / GLA_GATE_NORM).reshape(B, S, 2, GLA_HEADS, GLA_DK)
    o_f = gla_scan(qh, kh, vh, log_f[:, :, 0])
    o_b = _flip(gla_scan(_flip(qh), _flip(kh), _flip(vh), _flip(log_f[:, :, 1])))
    o = rmsnorm(o_f + o_b, norm_g) * jax.nn.silu(og.astype(f32)).reshape(B, S, GLA_HEADS, GLA_DV)
    return o.reshape(B, S, GLA_V_W).astype(q.dtype)


def rglru_mixer(xb, gb, conv_w, conv_b, w_a, b_a, w_x, b_x, lam):
    B, S, W = xb.shape
    f32 = jnp.float32
    u = (dwconv_centred(xb, conv_w) + conv_b).astype(f32)
    ub = u.reshape(B, S, LRU_BLOCKS, W // LRU_BLOCKS)

    def gate(w, b):
        return jax.nn.sigmoid(jnp.einsum('bsni,nio->bsno', ub, w.astype(f32)).reshape(B, S, W) + b.astype(f32))

    def direction(d, reverse):
        r = gate(w_a[d], b_a[d])
        i = gate(w_x[d], b_x[d])
        log_a = -LRU_C * r * jax.nn.softplus(-lam[d].astype(f32))
        a = jnp.exp(log_a)
        xin = jnp.sqrt(-jnp.expm1(2.0 * log_a)) * (i * u)
        _, hs = lax.associative_scan(_linrec_combine, (a, xin), axis=1, reverse=reverse)
        return hs

    hsum = direction(0, False) + direction(1, True)
    return (hsum * jax.nn.gelu(gb.astype(f32))).astype(xb.dtype)


def _cparams(*sem):
    return pltpu.CompilerParams(dimension_semantics=sem, vmem_limit_bytes=VMEM_LIMIT)


def _split_bf16(a, n):
    parts, r = [], a
    for i in range(n):
        p = r.astype(BF16)
        parts.append(p)
        if i + 1 < n:
            r = r - p.astype(F32)
    return parts


def _split_trunc_bf16(a, n):
    parts, r = [], a
    for i in range(n):
        bits = lax.bitcast_convert_type(r, jnp.uint32) & jnp.uint32(0xFFFF0000)
        p = lax.bitcast_convert_type(bits, F32)
        parts.append(p.astype(BF16))
        if i + 1 < n:
            r = r - p
    return parts


def _dot_nn(a, b):
    return jnp.dot(a, b, preferred_element_type=F32)


def _dot_nt(a, b):
    return lax.dot_general(a, b, (((1,), (1,)), ((), ())), preferred_element_type=F32)


def _dot_tn(a, b):
    return lax.dot_general(a, b, (((0,), (0,)), ((), ())), preferred_element_type=F32)


def _dot_split(a, b_exact, n=3, dot=_dot_nn):
    acc = None
    for p in _split_bf16(a, n):
        t = dot(p, b_exact)
        acc = t if acc is None else acc + t
    return acc


def _dot_f32(a, b, dot=_dot_nn):
    a_hi, a_lo = _split_bf16(a, 2)
    b_hi, b_lo = _split_bf16(b, 2)
    return dot(a_hi, b_hi) + (dot(a_hi, b_lo) + dot(a_lo, b_hi))


def _rms_rows(x, g):
    return x * lax.rsqrt(jnp.mean(x * x, axis=-1, keepdims=True) + RMS_EPS) * g


def _sigmoid(x):
    return 1.0 / (1.0 + jnp.exp(-x))


def _softplus(x):
    return jnp.maximum(x, 0.0) + jnp.log(1.0 + jnp.exp(-jnp.abs(x)))


def _silu(x):
    return x * _sigmoid(x)


_G_GLA = (0, 640)
_G_GOG = (640, 896)
_G_LX = (896, 1152)
_G_LG = (1152, 1408)
_G_DQ = (1408, 1664)
_G_DK = (1664, 1920)
_G_DV = (1920, 2176)
_G_NQKV = (2176, 2944)
_G_NZ = (2944, 3200)
_G_NBA = (3200, 3328)
IN_PAD_W = 3328
IN_TM = 512


def _permute_w_in(w):
    z = lambda n: jnp.zeros((w.shape[0], n), w.dtype)
    return jnp.concatenate([w[:, 0:512], w[:, 768:800], z(96), w[:, 512:768], w[:, 800:3120], z(112)],
                           axis=1).astype(BF16)


def _inproj_kernel(x_ref, g_ref, w_ref, gla_ref, gog_ref, lx_ref, lg_ref, dq_ref, dk_ref, dv_ref,
                   nqkv_ref, nz_ref, nba_ref):
    h = _rms_rows(x_ref[...], g_ref[...]).astype(BF16)

    def proj(grp):
        return jnp.dot(h, w_ref[:, grp[0]:grp[1]], preferred_element_type=F32)

    gla_ref[...] = proj(_G_GLA)
    gog_ref[...] = proj(_G_GOG).astype(BF16)
    lx_ref[...] = proj(_G_LX)
    lg_ref[...] = proj(_G_LG).astype(BF16)
    dq_ref[...] = (proj(_G_DQ) * (DIFF_DH ** -0.5 * LOG2E)).astype(BF16)
    dk_ref[...] = proj(_G_DK).astype(BF16)
    dv_ref[...] = proj(_G_DV).astype(BF16)
    nqkv_ref[...] = proj(_G_NQKV)
    nz_ref[...] = proj(_G_NZ).astype(BF16)
    nba_ref[...] = proj(_G_NBA)


def inproj_pallas(xt, g, w_in):
    T, D = xt.shape
    tm = min(IN_TM, T)
    groups = [(_G_GLA, F32), (_G_GOG, BF16), (_G_LX, F32), (_G_LG, BF16), (_G_DQ, BF16), (_G_DK, BF16),
              (_G_DV, BF16), (_G_NQKV, F32), (_G_NZ, BF16), (_G_NBA, F32)]
    return pl.pallas_call(
        _inproj_kernel,
        grid=(T // tm,),
        in_specs=[pl.BlockSpec((tm, D), lambda i: (i, 0)),
                  pl.BlockSpec((1, D), lambda i: (0, 0)),
                  pl.BlockSpec((D, IN_PAD_W), lambda i: (0, 0))],
        out_specs=[pl.BlockSpec((tm, b - a), lambda i: (i, 0)) for (a, b), _ in groups],
        out_shape=[jax.ShapeDtypeStruct((T, b - a), dt) for (a, b), dt in groups],
        compiler_params=_cparams("parallel"),
        name="inproj",
    )(xt, g.reshape(1, D).astype(F32), _permute_w_in(w_in))


def _gla_kernel(x_ref, tri_ref, w_ref, b_ref, o_ref, st_ref, *, nb):
    c = GLA_CHUNK

    @pl.when(pl.program_id(1) == 0)
    def _():
        st_ref[...] = jnp.zeros(st_ref.shape, F32)

    tri = tri_ref[0]
    tri_bf = tri.astype(BF16)
    tri4 = jnp.concatenate([tri] * GLA_HEADS, axis=0)
    w = w_ref[0]
    bias = b_ref[0]
    lane_qk = lax.broadcasted_iota(jnp.int32, (c, GLA_QK_W), 1) // GLA_DK
    lane_v = lax.broadcasted_iota(jnp.int32, (c, GLA_V_W), 1) // GLA_DV
    row_s = lax.broadcasted_iota(jnp.int32, (GLA_V_W, GLA_QK_W), 0) // GLA_DV
    col_s = lax.broadcasted_iota(jnp.int32, (GLA_V_W, GLA_QK_W), 1) // GLA_DK
    st_mask = row_s == col_s

    for b in range(nb):
        blk = x_ref[b]
        q = blk[:, 0:128] * GLA_DK ** -0.5
        k = blk[:, 128:256]
        v = blk[:, 256:512].astype(BF16)
        lr = blk[:, 512:640]
        z = _dot_f32(lr, w) + bias
        g = (jnp.minimum(z, 0.0) - jnp.log(1.0 + jnp.exp(-jnp.abs(z)))) * (1.0 / GLA_GATE_NORM)
        acc = None
        for p in _split_bf16(g, 3):
            t = _dot_nn(tri_bf, p)
            acc = t if acc is None else acc + t
        bc = acc
        tot = jnp.sum(g, axis=0, keepdims=True)
        ref = 0.5 * tot
        qt = q * jnp.exp(bc - ref)
        kt = (k * jnp.exp(ref - bc)).astype(BF16)
        qd = (q * jnp.exp(bc)).astype(BF16)
        kd = (k * jnp.exp(tot - bc)).astype(BF16)
        qstack = jnp.concatenate(
            [jnp.where(lane_qk == hh, qt, 0.0) for hh in range(GLA_HEADS)], axis=0).astype(BF16)
        s = _dot_nt(qstack, kt) * tri4
        st = st_ref[b]
        o = _dot_nt(qd, st.astype(BF16))
        for hh in range(GLA_HEADS):
            oh = _dot_nn(s[hh * c:(hh + 1) * c].astype(BF16), v)
            o = o + jnp.where(lane_v == hh, oh, 0.0)
        o_ref[0, b] = o
        upd = _dot_tn(v, kd)
        st_ref[b] = st * jnp.exp(tot) + jnp.where(st_mask, upd, 0.0)


def gla_pallas(gla_in, lr_up, lr_bias):
    B, S, W = gla_in.shape
    c = GLA_CHUNK
    nc = S // c
    r = np.arange(c)
    tri = jnp.asarray(np.stack([r[:, None] >= r[None, :], r[:, None] <= r[None, :]]).astype(np.float32))
    w = jnp.zeros((2, LANES, GLA_QK_W), F32)
    w = w.at[0, 0:GLA_RANK].set(lr_up[0].astype(F32)).at[1, GLA_RANK:2 * GLA_RANK].set(lr_up[1].astype(F32))
    bias = lr_bias.astype(F32).reshape(2, 1, GLA_QK_W)

    def tmap(d, i):
        return (0, i + d * (nc - 1 - 2 * i), 0)

    return pl.pallas_call(
        functools.partial(_gla_kernel, nb=B),
        grid=(2, nc),
        in_specs=[pl.BlockSpec((B, c, W), tmap),
                  pl.BlockSpec((1, c, c), lambda d, i: (d, 0, 0)),
                  pl.BlockSpec((1, LANES, GLA_QK_W), lambda d, i: (d, 0, 0)),
                  pl.BlockSpec((1, 1, GLA_QK_W), lambda d, i: (d, 0, 0))],
        out_specs=pl.BlockSpec((1, B, c, GLA_V_W), lambda d, i: (d,) + tmap(d, i)),
        out_shape=jax.ShapeDtypeStruct((2, B, S, GLA_V_W), F32),
        scratch_shapes=[pltpu.VMEM((B, GLA_V_W, GLA_QK_W), F32)],
        compiler_params=_cparams("arbitrary", "arbitrary"),
        name="gla",
    )(gla_in, tri, w, bias)


LRU_TB = 512
HALO = 8


def _halo_specs(tb, width, nt):
    r = tb // HALO
    return [pl.BlockSpec((1, HALO, width), lambda b, i: (b, jnp.maximum(i * r - 1, 0), 0)),
            pl.BlockSpec((1, tb, width), lambda b, i: (b, i, 0)),
            pl.BlockSpec((1, HALO, width), lambda b, i: (b, jnp.minimum((i + 1) * r, nt * r - 1), 0))]


def _conv4(prev_ref, cur_ref, next_ref, w, nt):
    i = pl.program_id(1)
    cur = cur_ref[0]
    tb = cur.shape[0]
    prev = prev_ref[0] * jnp.where(i > 0, 1.0, 0.0)
    nxt = next_ref[0] * jnp.where(i < nt - 1, 1.0, 0.0)
    ext = jnp.concatenate([prev, cur, nxt], axis=0)
    out = None
    for j in range(CONV_WIDTH):
        off = HALO + j - CONV_WIDTH // 2
        t = ext[off:off + tb] * w[j:j + 1]
        out = t if out is None else out + t
    return out


def _lru_prep_kernel(prev_ref, cur_ref, next_ref, cw_ref, cb_ref, wg_ref, bg_ref, lam_ref, a_ref, x_ref, *, nt):
    u = _conv4(prev_ref, cur_ref, next_ref, cw_ref[...], nt) + cb_ref[...]
    gates = _sigmoid(jnp.dot(u.astype(BF16), wg_ref[...], preferred_element_type=F32) + bg_ref[...])
    w = LRU_WIDTH
    for d in range(2):
        r = gates[:, (2 * d) * w:(2 * d + 1) * w]
        ig = gates[:, (2 * d + 1) * w:(2 * d + 2) * w]
        log_a = -LRU_C * r * _softplus(-lam_ref[d:d + 1])
        a_ref[d, 0] = jnp.exp(log_a)
        x_ref[d, 0] = jnp.sqrt(1.0 - jnp.exp(2.0 * log_a)) * (ig * u)


def _lru_scan_kernel(a_ref, x_ref, h_ref, st_ref, *, tb):
    d = pl.program_id(0)

    @pl.when(pl.program_id(1) == 0)
    def _():
        st_ref[...] = jnp.zeros(st_ref.shape, F32)

    def body(t, h):
        tt = t + d * (tb - 1 - 2 * t)
        h = a_ref[0, :, pl.ds(tt, 1), :] * h + x_ref[0, :, pl.ds(tt, 1), :]
        h_ref[0, :, pl.ds(tt, 1), :] = h
        return h

    st_ref[...] = lax.fori_loop(0, tb, body, st_ref[...], unroll=8)


def lru_pallas(lx, conv_w, conv_b, w_a, b_a, w_x, b_x, lam):
    B, S, W = lx.shape
    tb = min(LRU_TB, S)
    nt = S // tb
    blk = W // LRU_BLOCKS

    def dense(wb):
        m = jnp.zeros((W, W), F32)
        for n in range(LRU_BLOCKS):
            m = m.at[n * blk:(n + 1) * blk, n * blk:(n + 1) * blk].set(wb[n].astype(F32))
        return m

    wg = jnp.concatenate([dense(w_a[0]), dense(w_x[0]), dense(w_a[1]), dense(w_x[1])], axis=1).astype(BF16)
    bg = jnp.concatenate([b_a[0], b_x[0], b_a[1], b_x[1]]).astype(F32).reshape(1, 4 * W)
    full = lambda shape: pl.BlockSpec(shape, lambda b, i: (0,) * len(shape))
    a, xin = pl.pallas_call(
        functools.partial(_lru_prep_kernel, nt=nt),
        grid=(B, nt),
        in_specs=_halo_specs(tb, W, nt) + [full((CONV_WIDTH, W)), full((1, W)), full((W, 4 * W)), full((1, 4 * W)),
                                           full((2, W))],
        out_specs=[pl.BlockSpec((2, 1, tb, W), lambda b, i: (0, b, i, 0))] * 2,
        out_shape=[jax.ShapeDtypeStruct((2, B, S, W), F32)] * 2,
        compiler_params=_cparams("parallel", "parallel"),
        name="lru_prep",
    )(lx, lx, lx, conv_w.astype(F32), conv_b.astype(F32).reshape(1, W), wg, bg, lam.astype(F32))

    def tmap(d, i):
        return (d, 0, i + d * (nt - 1 - 2 * i), 0)

    return pl.pallas_call(
        functools.partial(_lru_scan_kernel, tb=tb),
        grid=(2, nt),
        in_specs=[pl.BlockSpec((1, B, tb, W), tmap)] * 2,
        out_specs=pl.BlockSpec((1, B, tb, W), tmap),
        out_shape=jax.ShapeDtypeStruct((2, B, S, W), F32),
        scratch_shapes=[pltpu.VMEM((B, 1, W), F32)],
        compiler_params=_cparams("arbitrary", "arbitrary"),
        name="lru_scan",
    )(a, xin)


DIFF_TQ = 256
DIFF_TK = 512
DIFF_POS_LANE = 2 * DIFF_DH
DIFF_POS_PIECES = 3
DIFF_UNROLL = 4
DIFF_VT_ROWS = DIFF_DV + 8
LOG2E = 1.4426950408889634


def _diff_attn_kernel(slopes_ref, lam_ref, q_ref, k_ref, vt_ref, g_ref, o_ref,
                      qv_scr, rt_scr, s_scr, mb_scr, p_scr, al_scr, m_scr, acc_scr, *, tq, tk, seq, out_scale):
    h = pl.program_id(1)
    qi = pl.program_id(2)
    slope = slopes_ref[h]
    lam = lam_ref[0]
    nk = seq // tk

    q = q_ref[0, 0]
    lane = lax.broadcasted_iota(jnp.int32, q.shape, 1)
    zero = jnp.zeros_like(q)
    q1 = jnp.where(jnp.abs(2 * lane - (DIFF_DH + DIFF_POS_LANE - 1)) < DIFF_DH, zero, q)
    q2 = jnp.where(lane < DIFF_DH, zero, q)
    qs = jnp.concatenate([q1, q2], axis=0)
    lane2 = lax.broadcasted_iota(jnp.int32, qs.shape, 1)
    col = lax.broadcasted_iota(jnp.int32, (1, 2 * tq), 1)
    qpos = qi * tq + jnp.where(col >= tq, col - tq, col)
    shift_q = slope * qpos.astype(F32)
    qv_scr[0] = qs
    qv_scr[1] = jnp.where(lane2 >= DIFF_POS_LANE, -qs, qs)
    rt_scr[0] = -shift_q
    rt_scr[1] = shift_q

    kd = (qi * tq) // tk
    k0 = pl.multiple_of(kd * tk, tk)
    q_mid = jnp.where(lane2 >= DIFF_POS_LANE, jnp.zeros_like(qs), qs)
    s = _dot_nt(k_ref[0, 0, pl.ds(k0, tk), :], q_mid)
    kpos = k0 + lax.broadcasted_iota(jnp.int32, (tk, 2 * tq), 0)
    s = s - slope * jnp.abs(qpos - kpos).astype(F32)
    m0 = jnp.max(s, axis=0, keepdims=True)
    p = jnp.exp2(s - m0).astype(BF16)
    acc_scr[...] = jnp.dot(vt_ref[0, 0, :, pl.ds(k0, tk)], p, preferred_element_type=F32)
    m_scr[...] = m0

    def tile_of(t):
        return t + jnp.where(t >= kd, 1, 0)

    def scores(t, slot):
        kj = tile_of(t)
        var = jnp.where(kj > kd, 1, 0)
        kt = k_ref[0, 0, pl.ds(pl.multiple_of(kj * tk, tk), tk), :]
        sc = _dot_nt(kt, qv_scr[var])
        s_scr[slot] = sc
        mb_scr[slot] = jnp.max(sc, axis=0, keepdims=True) + rt_scr[var]

    def softmax(t, slot):
        kj = tile_of(t)
        var = jnp.where(kj > kd, 1, 0)
        m_old = m_scr[...]
        m_new = jnp.maximum(m_old, mb_scr[slot])
        al_scr[slot] = jnp.exp2(m_old - m_new)
        p_scr[slot] = jnp.exp2(s_scr[slot] - (m_new - rt_scr[var])).astype(BF16)
        m_scr[...] = m_new

    def pv(t, slot):
        vt = vt_ref[0, 0, :, pl.ds(pl.multiple_of(tile_of(t) * tk, tk), tk)]
        acc_scr[...] = al_scr[slot] * acc_scr[...] + jnp.dot(vt, p_scr[slot], preferred_element_type=F32)

    n_off = nk - 1
    if n_off == 1:
        scores(0, 0)
        softmax(0, 0)
        pv(0, 0)
    elif n_off >= 2:
        scores(0, 0)
        softmax(0, 0)
        scores(1, 1)

        def trip(t, slot):
            pv(t, slot)
            scores(t + 2, slot)
            softmax(t + 1, 1 - slot)

        def body(i, c):
            for u in range(DIFF_UNROLL):
                trip(DIFF_UNROLL * i + u, u % 2)
            return c

        n_trips = n_off - 2
        n_loop = n_trips // DIFF_UNROLL
        lax.fori_loop(0, n_loop, body, 0)
        for t in range(n_loop * DIFF_UNROLL, n_trips):
            trip(t, t % 2)
        last = n_off - 1
        pv(last - 1, (last - 1) % 2)
        softmax(last, last % 2)
        pv(last, last % 2)

    acc = acc_scr[...]
    o2 = acc[0:DIFF_DV] * (1.0 / acc[DIFF_DV:DIFF_DV + 1])
    o = o2[:, :tq] - lam * o2[:, tq:]
    ms = jnp.mean(o * o, axis=0, keepdims=True)
    o = o * lax.rsqrt(ms + RMS_EPS) * g_ref[...] * out_scale
    o_ref[0, 0] = o.astype(o_ref.dtype)


def diff_attention_pallas(dq, dk, dv, lq1, lk1, lq2, lk2, norm_g, lambda_init):
    B, S, _ = dq.shape
    H = DIFF_HEADS
    tq = min(DIFF_TQ, S)
    tk = min(DIFF_TK, S)
    slopes = jnp.exp2(-8.0 * jnp.arange(1, H + 1, dtype=F32) / H) * LOG2E
    lam = (jnp.exp(jnp.sum(lq1.astype(F32) * lk1.astype(F32)))
           - jnp.exp(jnp.sum(lq2.astype(F32) * lk2.astype(F32))) + lambda_init).reshape(1)
    npos = DIFF_POS_PIECES
    pad = LANES - 2 * DIFF_DH - npos
    q4 = dq.astype(BF16).reshape(B, S, H, 2 * DIFF_DH).transpose(0, 2, 1, 3)
    q_aug = jnp.concatenate([q4, jnp.ones((B, H, S, npos), BF16), jnp.zeros((B, H, S, pad), BF16)], axis=-1)
    k4 = dk.astype(BF16).reshape(B, S, H, 2 * DIFF_DH).transpose(0, 2, 1, 3)
    kbias = slopes[:, None] * jnp.arange(S, dtype=F32)[None, :]
    kpos = jnp.stack(_split_trunc_bf16(kbias, npos), axis=-1)[None]
    k_aug = jnp.concatenate([k4, jnp.broadcast_to(kpos, (B, H, S, npos)), jnp.zeros((B, H, S, pad), BF16)],
                            axis=-1)
    v4 = dv.astype(BF16).reshape(B, S, H, DIFF_DV).transpose(0, 2, 3, 1)
    vt = jnp.concatenate([v4, jnp.ones((B, H, 1, S), BF16),
                          jnp.zeros((B, H, DIFF_VT_ROWS - DIFF_DV - 1, S), BF16)], axis=2)
    g = norm_g.astype(F32).reshape(DIFF_DV, 1)

    kern = functools.partial(_diff_attn_kernel, tq=tq, tk=tk, seq=S, out_scale=1.0 - lambda_init)
    out = pl.pallas_call(
        kern,
        grid=(B, H, S // tq),
        in_specs=[
            pl.BlockSpec(memory_space=pltpu.SMEM),
            pl.BlockSpec(memory_space=pltpu.SMEM),
            pl.BlockSpec((1, 1, tq, LANES), lambda b, h, i: (b, h, i, 0)),
            pl.BlockSpec((1, 1, S, LANES), lambda b, h, i: (b, h, 0, 0)),
            pl.BlockSpec((1, 1, DIFF_VT_ROWS, S), lambda b, h, i: (b, h, 0, 0)),
            pl.BlockSpec((DIFF_DV, 1), lambda b, h, i: (0, 0)),
        ],
        out_specs=pl.BlockSpec((1, 1, DIFF_DV, tq), lambda b, h, i: (b, h, 0, i)),
        out_shape=jax.ShapeDtypeStruct((B, H, DIFF_DV, S), F32),
        scratch_shapes=[
            pltpu.VMEM((2, 2 * tq, LANES), BF16),
            pltpu.VMEM((2, 1, 2 * tq), F32),
            pltpu.VMEM((2, tk, 2 * tq), F32),
            pltpu.VMEM((2, 1, 2 * tq), F32),
            pltpu.VMEM((2, tk, 2 * tq), BF16),
            pltpu.VMEM((2, 1, 2 * tq), F32),
            pltpu.VMEM((1, 2 * tq), F32),
            pltpu.VMEM((DIFF_VT_ROWS, 2 * tq), F32),
        ],
        compiler_params=pltpu.CompilerParams(
            dimension_semantics=("parallel", "parallel", "arbitrary"),
            vmem_limit_bytes=VMEM_LIMIT),
        name="diff_attn",
    )(slopes, lam, q_aug, k_aug, vt, g)
    return out.transpose(0, 3, 1, 2).reshape(B, S, DIFF_V_W)


MERGE_TM = 256


def _group_rms(o, ones_bd, g, width):
    ss = _dot_split(o * o, ones_bd, n=2)
    return o * lax.rsqrt(ss * (1.0 / width) + RMS_EPS) * g


def _gelu_tanh(x):
    return 0.5 * x * (1.0 + jnp.tanh(0.7978845608028654 * (x + 0.044715 * (x * x * x))))


def _merge_kernel(x_ref, gla_ref, gog_ref, lru_ref, lg_ref, dif_ref, gdn_ref, nz_ref,
                  mg_ref, glag_ref, gdng_ref, ones_ref, mw_ref, mb_ref, up_ref, wo_ref, o_ref):
    x = x_ref[...]
    h = _rms_rows(x, mg_ref[...]).astype(BF16)
    ones_bd = ones_ref[...]
    y_gla = _group_rms(gla_ref[0] + gla_ref[1], ones_bd, glag_ref[...], GLA_DV) * _silu(gog_ref[...].astype(F32))
    y_lru = (lru_ref[0] + lru_ref[1]) * _gelu_tanh(lg_ref[...].astype(F32))
    y_dif = dif_ref[...]
    y_gdn = _group_rms(gdn_ref[0] + gdn_ref[1], ones_bd, gdng_ref[...], GDN_DV) * _silu(nz_ref[...].astype(F32))
    merged = None
    for i, y in enumerate((y_gla, y_lru, y_dif, y_gdn)):
        gate = _sigmoid(jnp.dot(h, mw_ref[i], preferred_element_type=F32) + mb_ref[i])
        term = gate * jnp.dot(y.astype(BF16), up_ref[i], preferred_element_type=F32)
        merged = term if merged is None else merged + term
    o_ref[...] = x + jnp.dot(merged.astype(BF16), wo_ref[...], preferred_element_type=F32)


def merge_pallas(xt, gla_o, gog, lru_h, lg, y_diff, gdn_o, nz, mix_g, gla_g, gdn_g, merge_w, merge_b, branch_up,
                 mix_out):
    T, D = xt.shape
    tm = min(MERGE_TM, T)
    W = BRANCH_WIDTH
    r = np.arange(W) // GLA_DV
    ones_bd = jnp.asarray((r[:, None] == r[None, :]).astype(np.float32)).astype(BF16)
    tok = lambda w: pl.BlockSpec((tm, w), lambda i: (i, 0))
    tok2 = pl.BlockSpec((2, tm, W), lambda i: (0, i, 0))
    const = lambda shape: pl.BlockSpec(shape, lambda i: (0,) * len(shape), pipeline_mode=pl.Buffered(1))
    return pl.pallas_call(
        _merge_kernel,
        grid=(T // tm,),
        in_specs=[tok(D), tok2, tok(W), tok2, tok(W), tok(W), tok2, tok(W),
                  const((1, D)), const((1, W)), const((1, W)), const((W, W)),
                  const((N_BRANCHES, D, D)), const((N_BRANCHES, 1, D)), const((N_BRANCHES, W, D)), const((D, D))],
        out_specs=tok(D),
        out_shape=jax.ShapeDtypeStruct((T, D), F32),
        compiler_params=_cparams("parallel"),
        name="merge",
    )(xt, gla_o, gog, lru_h, lg, y_diff, gdn_o, nz,
      mix_g.reshape(1, D).astype(F32), jnp.tile(gla_g.astype(F32), GLA_HEADS).reshape(1, W),
      jnp.tile(gdn_g.astype(F32), GDN_HEADS).reshape(1, W), ones_bd,
      merge_w.astype(BF16), merge_b.astype(F32).reshape(N_BRANCHES, 1, D), branch_up.astype(BF16),
      mix_out.astype(BF16))


XATTN_TM = 512


def _kv_kernel(m_ref, g_ref, w_ref, o_ref):
    mn = _rms_rows(m_ref[0], g_ref[...]).astype(BF16)
    o_ref[0] = jnp.dot(mn, w_ref[...], preferred_element_type=F32).astype(BF16)


def _xattn_kernel(x_ref, g_ref, wq_ref, kv_ref, wo_ref, o_ref):
    x = x_ref[0]
    h = _rms_rows(x, g_ref[...]).astype(BF16)
    q = jnp.dot(h, wq_ref[...], preferred_element_type=F32).astype(BF16)
    outs = []
    for hh in range(XATTN_HEADS):
        lo = hh * XATTN_DH
        k = kv_ref[0, :, lo:lo + XATTN_DH]
        v = kv_ref[0, :, D_MODEL + lo:D_MODEL + lo + XATTN_DH]
        s = _dot_nt(q[:, lo:lo + XATTN_DH], k) * XATTN_DH ** -0.5
        p = jnp.exp(s - jnp.max(s, axis=-1, keepdims=True))
        p = p / jnp.sum(p, axis=-1, keepdims=True)
        outs.append(jnp.dot(p.astype(BF16), v, preferred_element_type=F32).astype(BF16))
    o = jnp.concatenate(outs, axis=-1)
    o_ref[0] = x + jnp.dot(o, wo_ref[...], preferred_element_type=F32)


def xattn_pallas(x, mem, xg, mg, wq, wkv, wo):
    B, S, D = x.shape
    M = mem.shape[1]
    tm = min(XATTN_TM, S)
    kv = pl.pallas_call(
        _kv_kernel,
        grid=(B,),
        in_specs=[pl.BlockSpec((1, M, D), lambda b: (b, 0, 0)),
                  pl.BlockSpec((1, D), lambda b: (0, 0)),
                  pl.BlockSpec((D, 2 * D), lambda b: (0, 0))],
        out_specs=pl.BlockSpec((1, M, 2 * D), lambda b: (b, 0, 0)),
        out_shape=jax.ShapeDtypeStruct((B, M, 2 * D), BF16),
        compiler_params=_cparams("parallel"),
        name="xattn_kv",
    )(mem, mg.reshape(1, D).astype(F32), wkv.astype(BF16))
    const = lambda shape: pl.BlockSpec(shape, lambda b, i: (0,) * len(shape), pipeline_mode=pl.Buffered(1))
    return pl.pallas_call(
        _xattn_kernel,
        grid=(B, S // tm),
        in_specs=[pl.BlockSpec((1, tm, D), lambda b, i: (b, i, 0)),
                  const((1, D)), const((D, D)),
                  pl.BlockSpec((1, M, 2 * D), lambda b, i: (b, 0, 0)),
                  const((D, D))],
        out_specs=pl.BlockSpec((1, tm, D), lambda b, i: (b, i, 0)),
        out_shape=jax.ShapeDtypeStruct((B, S, D), F32),
        compiler_params=_cparams("parallel", "parallel"),
        name="xattn",
    )(x, xg.reshape(1, D).astype(F32), wq.astype(BF16), kv, wo.astype(BF16))


ROUTE_TM = 512
ROUTE_EXPERT_LANE = 32
MOE_BM = 256
COMBINE_TM = 512


def _router_kernel(x_ref, g_ref, w_ref, b_ref, h_ref, r_ref):
    h = _rms_rows(x_ref[...], g_ref[...])
    h_ref[...] = h
    logits = _dot_f32(h, w_ref[...]) + b_ref[...]
    lane = lax.broadcasted_iota(jnp.int32, logits.shape, 1)
    big = jnp.int32(1 << 20)
    neg = jnp.float32(-jnp.inf)

    def top(vals):
        m = jnp.max(vals, axis=-1, keepdims=True)
        idx = jnp.min(jnp.where(vals == m, lane, big), axis=-1, keepdims=True)
        return m, idx

    is_g = lane < N_GROUPS
    gmax, gidx = top(jnp.where(is_g, logits, neg))
    gsum = jnp.sum(jnp.where(is_g, jnp.exp(logits - gmax), 0.0), axis=-1, keepdims=True)
    g_w = 1.0 / gsum
    lo = ROUTE_EXPERT_LANE + EXPERTS_PER_GROUP * gidx
    in_grp = jnp.abs(2 * (lane - lo) - (EXPERTS_PER_GROUP - 1)) < EXPERTS_PER_GROUP
    el = jnp.where(in_grp, logits, neg)
    e1, i1 = top(el)
    e2, i2 = top(jnp.where(lane == i1, neg, el))
    t = jnp.exp(e2 - e1)
    w1 = g_w / (1.0 + t)
    w2 = g_w * t / (1.0 + t)
    f = lambda v: v.astype(F32)
    r_ref[...] = jnp.where(lane == 0, f(i1 - ROUTE_EXPERT_LANE),
                           jnp.where(lane == 1, f(i2 - ROUTE_EXPERT_LANE),
                                     jnp.where(lane == 2, w1, jnp.where(lane == 3, w2, 0.0))))


def router_pallas(xt, g, w_group, b_group, w_expert, b_expert):
    T, D = xt.shape
    tm = min(ROUTE_TM, T)
    w = jnp.zeros((D, LANES), F32).at[:, :N_GROUPS].set(w_group.astype(F32))
    w = w.at[:, ROUTE_EXPERT_LANE:ROUTE_EXPERT_LANE + N_EXPERTS].set(w_expert.astype(F32))
    b = jnp.zeros((1, LANES), F32).at[0, :N_GROUPS].set(b_group.astype(F32))
    b = b.at[0, ROUTE_EXPERT_LANE:ROUTE_EXPERT_LANE + N_EXPERTS].set(b_expert.astype(F32))
    return pl.pallas_call(
        _router_kernel,
        grid=(T // tm,),
        in_specs=[pl.BlockSpec((tm, D), lambda i: (i, 0)),
                  pl.BlockSpec((1, D), lambda i: (0, 0)),
                  pl.BlockSpec((D, LANES), lambda i: (0, 0)),
                  pl.BlockSpec((1, LANES), lambda i: (0, 0))],
        out_specs=[pl.BlockSpec((tm, D), lambda i: (i, 0)), pl.BlockSpec((tm, LANES), lambda i: (i, 0))],
        out_shape=[jax.ShapeDtypeStruct((T, D), F32), jax.ShapeDtypeStruct((T, LANES), F32)],
        compiler_params=_cparams("parallel"),
        name="moe_router",
    )(xt, g.reshape(1, D).astype(F32), w, b)


def _gather_rows(idx_hbm_row, src_hbm, idx_smem, dst, sem_idx, sem_rows, n):
    cp = pltpu.make_async_copy(idx_hbm_row, idx_smem, sem_idx)
    cp.start()
    cp.wait()

    def row_copy(r):
        return pltpu.make_async_copy(src_hbm.at[pl.ds(idx_smem[r], 1)], dst.at[pl.ds(r, 1)], sem_rows)

    def issue(r, c):
        row_copy(r).start()
        return c

    def drain(r, c):
        row_copy(r).wait()
        return c

    lax.fori_loop(0, n, issue, 0)
    lax.fori_loop(0, n, drain, 0)


def _expert_kernel(be_ref, nu_ref, idx_hbm, h_hbm, w1_ref, w3_ref, w2_ref, y_ref, idx_smem, xbuf, sem_idx, sem_rows,
                   *, bm):
    i = pl.program_id(0)

    @pl.when(i < nu_ref[0])
    def _():
        _gather_rows(idx_hbm.at[i], h_hbm, idx_smem, xbuf, sem_idx, sem_rows, bm)
        xb = xbuf[...].astype(BF16)
        a = jnp.dot(xb, w1_ref[0], preferred_element_type=F32)
        g = jnp.dot(xb, w3_ref[0], preferred_element_type=F32)
        y_ref[...] = jnp.dot((_silu(a) * g).astype(BF16), w2_ref[0], preferred_element_type=F32)

    @pl.when(i >= nu_ref[0])
    def _():
        y_ref[...] = jnp.zeros(y_ref.shape, F32)


def _combine_kernel(x_ref, r_ref, pos_hbm, y_hbm, g_ref, o_ref, idx_smem, ybuf, sem_idx, sem_rows, *, tm, final):
    i = pl.program_id(0)
    _gather_rows(pos_hbm.at[i], y_hbm, idx_smem, ybuf, sem_idx, sem_rows, 2 * tm)
    r = r_ref[...]
    out = x_ref[...] + r[:, 2:3] * ybuf[0:tm] + r[:, 3:4] * ybuf[tm:2 * tm]
    if final:
        out = _rms_rows(out, g_ref[...])
    o_ref[...] = out


def moe_pallas(xt, g, w_group, b_group, w_expert, b_expert, w1, w3, w2, final_g=None):
    T, D = xt.shape
    bm = MOE_BM
    h, route = router_pallas(xt, g, w_group, b_group, w_expert, b_expert)
    eid = route[:, 0:2].astype(jnp.int32).reshape(-1)
    A = T * TOP_K
    n_blk = A // bm + N_EXPERTS
    P = n_blk * bm
    order = jnp.argsort(eid)
    eid_s = eid[order]
    tok_s = order // TOP_K
    counts = jnp.zeros((N_EXPERTS,), jnp.int32).at[eid].add(1)
    padded = (counts + bm - 1) // bm * bm
    pad_end = jnp.cumsum(padded)
    pad_start = pad_end - padded
    seg_start = jnp.cumsum(counts) - counts
    dest = pad_start[eid_s] + jnp.arange(A, dtype=jnp.int32) - seg_start[eid_s]
    row_tok = jnp.zeros((P,), jnp.int32).at[dest].set(tok_s.astype(jnp.int32))
    pos = jnp.zeros((A,), jnp.int32).at[order].set(dest)
    blk_expert = jnp.minimum(
        jnp.searchsorted(pad_end, jnp.arange(n_blk, dtype=jnp.int32) * bm, side='right'), N_EXPERTS - 1
    ).astype(jnp.int32)
    n_used = (pad_end[-1] // bm).astype(jnp.int32).reshape(1)

    y = pl.pallas_call(
        functools.partial(_expert_kernel, bm=bm),
        grid_spec=pltpu.PrefetchScalarGridSpec(
            num_scalar_prefetch=2,
            grid=(n_blk,),
            in_specs=[pl.BlockSpec(memory_space=pl.ANY),
                      pl.BlockSpec(memory_space=pl.ANY),
                      pl.BlockSpec((1, D, D_EXPERT), lambda i, be, nu: (be[i], 0, 0)),
                      pl.BlockSpec((1, D, D_EXPERT), lambda i, be, nu: (be[i], 0, 0)),
                      pl.BlockSpec((1, D_EXPERT, D), lambda i, be, nu: (be[i], 0, 0))],
            out_specs=pl.BlockSpec((bm, D), lambda i, be, nu: (i, 0)),
            scratch_shapes=[pltpu.SMEM((bm,), jnp.int32), pltpu.VMEM((bm, D), F32),
                            pltpu.SemaphoreType.DMA(()), pltpu.SemaphoreType.DMA(())]),
        out_shape=jax.ShapeDtypeStruct((P, D), F32),
        compiler_params=_cparams("arbitrary"),
        name="moe_experts",
    )(blk_expert, n_used, row_tok.reshape(n_blk, bm), h, w1.astype(BF16), w3.astype(BF16), w2.astype(BF16))

    tm = min(COMBINE_TM, T)
    nt = T // tm
    pos_t = pos.reshape(nt, tm, TOP_K).transpose(0, 2, 1).reshape(nt, TOP_K * tm)
    fg = (final_g if final_g is not None else jnp.ones((D,), F32)).reshape(1, D).astype(F32)
    return pl.pallas_call(
        functools.partial(_combine_kernel, tm=tm, final=final_g is not None),
        grid=(nt,),
        in_specs=[pl.BlockSpec((tm, D), lambda i: (i, 0)),
                  pl.BlockSpec((tm, LANES), lambda i: (i, 0)),
                  pl.BlockSpec(memory_space=pl.ANY),
                  pl.BlockSpec(memory_space=pl.ANY),
                  pl.BlockSpec((1, D), lambda i: (0, 0))],
        out_specs=pl.BlockSpec((tm, D), lambda i: (i, 0)),
        out_shape=jax.ShapeDtypeStruct((T, D), F32),
        scratch_shapes=[pltpu.SMEM((TOP_K * tm,), jnp.int32), pltpu.VMEM((TOP_K * tm, D), F32),
                        pltpu.SemaphoreType.DMA(()), pltpu.SemaphoreType.DMA(())],
        compiler_params=_cparams("arbitrary"),
        name="moe_combine",
    )(xt, route, pos_t, y, fg)


GDN_TB = 512
GDN_INV_PASSES = 1
GDN_UNROLL_B = 2


def _gdn_prep_kernel(prev_ref, cur_ref, next_ref, ba_ref, cw_ref, ones_ref, alog_ref, dt_ref, qkv_ref, bg_ref, *, nt):
    qkv = _silu(_conv4(prev_ref, cur_ref, next_ref, cw_ref[...], nt))
    ones_bd = ones_ref[...]
    w = GDN_QK_W
    q = qkv[:, 0:w]
    k = qkv[:, w:2 * w]
    qn = q * lax.rsqrt(_dot_split(q * q, ones_bd, n=2) + 1e-6) * GDN_DK ** -0.5
    kn = k * lax.rsqrt(_dot_split(k * k, ones_bd, n=2) + 1e-6)
    qkv_ref[0] = jnp.concatenate([qn, kn, qkv[:, 2 * w:]], axis=-1)
    ba = ba_ref[0]
    beta = _sigmoid(ba)
    log_a = -jnp.exp(alog_ref[...]) * _softplus(ba + dt_ref[...])
    lane = lax.broadcasted_iota(jnp.int32, ba.shape, 1)
    h = GDN_HEADS
    for d in range(2):
        b_d = pltpu.roll(beta, (LANES - d * h) % LANES, 1)
        a_d = pltpu.roll(log_a, (LANES - (2 * h + d * h) + h) % LANES, 1)
        bg_ref[d, 0] = jnp.where(lane < h, b_d, jnp.where(lane < 2 * h, a_d, 0.0))


def _gdn_chunk_kernel(qkv_ref, bg_ref, tri_ref, tribd_ref, o_ref, st_ref, *, nb):
    c = GDN_CHUNK
    H = GDN_HEADS
    n = H * c

    @pl.when(pl.program_id(1) == 0)
    def _():
        st_ref[...] = jnp.zeros(st_ref.shape, F32)

    tri_bf = tri_ref[0].astype(BF16)
    incl = tribd_ref[0]
    ri = lax.broadcasted_iota(jnp.int32, (n, n), 0)
    ci = lax.broadcasted_iota(jnp.int32, (n, n), 1)
    eye = jnp.where(ri == ci, 1.0, 0.0)
    strict = incl - eye
    bdmask = jnp.where(ri // c == ci // c, 1.0, 0.0)

    def stack(x):
        w = x.shape[1] // H
        return jnp.concatenate([x[:, hh * w:(hh + 1) * w] for hh in range(H)], axis=0)

    def col(x, lane0):
        return jnp.concatenate([x[:, lane0 + hh:lane0 + hh + 1] for hh in range(H)], axis=0)

    def bd(x_st):
        return jnp.concatenate([x_st] * H, axis=1) * bdmask

    def mm(a, b, passes, dot=_dot_nn):
        if passes == 1:
            return dot(a.astype(BF16), b.astype(BF16))
        return _dot_f32(a, b, dot=dot)

    def one_batch(b):
        qkv = qkv_ref[b]
        bg = bg_ref[0, b]
        q_st = stack(qkv[:, 0:GDN_QK_W])
        k_st = stack(qkv[:, GDN_QK_W:2 * GDN_QK_W])
        v_st = stack(qkv[:, 2 * GDN_QK_W:])
        acc = None
        for p in _split_bf16(bg, 3):
            t = _dot_nn(tri_bf, p)
            acc = t if acc is None else acc + t
        gam = acc
        tot = jnp.sum(bg, axis=0, keepdims=True)
        g_col = col(gam, H)
        b_col = col(bg, 0)
        gam_t = jnp.concatenate([gam, jnp.zeros_like(gam)], axis=0).T
        g_row = jnp.concatenate([gam_t[H + hh:H + hh + 1, 0:c] for hh in range(H)], axis=1)
        end_col = jnp.concatenate([jnp.broadcast_to(tot[:, H + hh:H + hh + 1], (c, 1)) for hh in range(H)], axis=0)
        exp_g = jnp.exp(g_col)
        decay = jnp.exp(jnp.minimum(g_col - g_row, 0.0)) * incl
        k_bf = k_st.astype(BF16)
        kk = _dot_nt(k_bf, k_bf)
        a_mat = strict * b_col * kk * decay
        p_inv = eye - a_mat
        x_pow = a_mat
        for _ in range(int(math.log2(c)) - 1):
            x_pow = mm(x_pow, x_pow, GDN_INV_PASSES)
            p_inv = p_inv + mm(p_inv, x_pow, GDN_INV_PASSES)
        rhs = jnp.concatenate([v_st * b_col, k_st * (b_col * exp_g)], axis=1)
        p_bf = p_inv.astype(BF16)
        sol = _dot_nn(p_bf, rhs.astype(BF16))
        resid = rhs - sol - _dot_f32(a_mat, sol)
        sol = sol + _dot_nn(p_bf, resid.astype(BF16))
        u_st = sol[:, 0:c]
        kc_bd = bd(sol[:, c:2 * c])
        qk = _dot_nt(q_st.astype(BF16), k_bf) * decay
        qd_bd = bd(q_st * exp_g)
        kd_bd = bd(k_st * jnp.exp(end_col - g_col))
        st = st_ref[b]
        st_bf = st.astype(BF16)
        v_new = u_st - _dot_nn(kc_bd.astype(BF16), st_bf)
        v_new_bf = v_new.astype(BF16)
        o_st = _dot_nn(qd_bd.astype(BF16), st_bf) + _dot_nn(qk.astype(BF16), v_new_bf)
        st_ref[b] = jnp.exp(end_col) * st + _dot_tn(kd_bd.astype(BF16), v_new_bf)
        o_ref[0, b] = jnp.concatenate([o_st[hh * c:(hh + 1) * c] for hh in range(H)], axis=1)

    def pair(i, carry):
        for j in range(GDN_UNROLL_B):
            one_batch(i * GDN_UNROLL_B + j)
        return carry

    lax.fori_loop(0, nb // GDN_UNROLL_B, pair, 0)


def gdn_pallas(nqkv, nba, conv_w, a_log, dt_bias):
    B, S, W = nqkv.shape
    tb = min(GDN_TB, S)
    nt = S // tb
    c = GDN_CHUNK
    nc = S // c
    H = GDN_HEADS
    r = np.arange(GDN_QK_W) // GDN_DK
    ones_bd = jnp.asarray((r[:, None] == r[None, :]).astype(np.float32)).astype(BF16)
    alog = jnp.zeros((1, LANES), F32).at[0, 2 * H:4 * H].set(a_log.astype(F32).reshape(-1))
    dt = jnp.zeros((1, LANES), F32).at[0, 2 * H:4 * H].set(dt_bias.astype(F32).reshape(-1))
    full = lambda shape: pl.BlockSpec(shape, lambda b, i: (0,) * len(shape))
    qkvn, bg = pl.pallas_call(
        functools.partial(_gdn_prep_kernel, nt=nt),
        grid=(B, nt),
        in_specs=_halo_specs(tb, W, nt) + [pl.BlockSpec((1, tb, LANES), lambda b, i: (b, i, 0)),
                                           full((CONV_WIDTH, W)), full((GDN_QK_W, GDN_QK_W)),
                                           full((1, LANES)), full((1, LANES))],
        out_specs=[pl.BlockSpec((1, tb, W), lambda b, i: (b, i, 0)),
                   pl.BlockSpec((2, 1, tb, LANES), lambda b, i: (0, b, i, 0))],
        out_shape=[jax.ShapeDtypeStruct((B, S, W), F32), jax.ShapeDtypeStruct((2, B, S, LANES), F32)],
        compiler_params=_cparams("parallel", "parallel"),
        name="gdn_prep",
    )(nqkv, nqkv, nqkv, nba, conv_w.astype(F32), ones_bd, alog, dt)

    t = np.arange(c)
    tri_np = np.stack([t[:, None] >= t[None, :], t[:, None] <= t[None, :]]).astype(np.float32)
    tri = jnp.asarray(tri_np)
    tri_bd = jnp.asarray(np.stack([np.kron(np.eye(H, dtype=np.float32), tri_np[d]) for d in range(2)]))

    def tmap(d, i):
        return i + d * (nc - 1 - 2 * i)

    return pl.pallas_call(
        functools.partial(_gdn_chunk_kernel, nb=B),
        grid=(2, nc),
        in_specs=[pl.BlockSpec((B, c, W), lambda d, i: (0, tmap(d, i), 0)),
                  pl.BlockSpec((1, B, c, LANES), lambda d, i: (d, 0, tmap(d, i), 0)),
                  pl.BlockSpec((1, c, c), lambda d, i: (d, 0, 0)),
                  pl.BlockSpec((1, H * c, H * c), lambda d, i: (d, 0, 0))],
        out_specs=pl.BlockSpec((1, B, c, GDN_V_W), lambda d, i: (d, 0, tmap(d, i), 0)),
        out_shape=jax.ShapeDtypeStruct((2, B, S, GDN_V_W), F32),
        scratch_shapes=[pltpu.VMEM((B, H * GDN_DK, GDN_DV), F32)],
        compiler_params=_cparams("arbitrary", "arbitrary"),
        name="gdn_chunk",
    )(qkvn, bg, tri, tri_bd)


def gdn_scan(q, k, v, beta, log_a):
    b_, s_, h_, dk = q.shape
    dv = v.shape[-1]
    c = GDN_CHUNK
    qc, kc, vc, bc, gc = (_to_chunks(t, c) for t in (q, k, v, beta, log_a))
    gam = jnp.cumsum(gc, axis=-1)
    diff = gam[..., :, None] - gam[..., None, :]
    incl = jnp.tril(jnp.ones((c, c), dtype=bool))
    strict = jnp.tril(jnp.ones((c, c), dtype=bool), k=-1)
    decay = jnp.exp(jnp.where(incl, diff, -jnp.inf))
    kk = jnp.einsum('bhntd,bhnsd->bhnts', kc, kc)
    a_mat = jnp.where(strict, bc[..., None] * kk * decay, 0.0) + jnp.eye(c, dtype=jnp.float32)
    rhs = jnp.concatenate([vc * bc[..., None], kc * (bc * jnp.exp(gam))[..., None]], axis=-1)
    sol = lax.linalg.triangular_solve(a_mat, rhs, left_side=True, lower=True, unit_diagonal=True)
    u_val, k_cum = sol[..., :dv], sol[..., dv:]
    qk = jnp.einsum('bhntd,bhnsd->bhnts', qc, kc) * decay
    q_dec = qc * jnp.exp(gam)[..., None]
    k_dec = kc * jnp.exp(gam[..., -1:] - gam)[..., None]
    c_dec = jnp.exp(gam[..., -1])

    def step(state, inp):
        u_i, kc_i, qk_i, qd_i, kd_i, cd_i = inp
        v_new = u_i - jnp.einsum('bhtd,bhde->bhte', kc_i, state)
        o = jnp.einsum('bhtd,bhde->bhte', qd_i, state) + jnp.einsum('bhts,bhse->bhte', qk_i, v_new)
        state = cd_i[..., None, None] * state + jnp.einsum('bhsd,bhse->bhde', kd_i, v_new)
        return state, o

    xs = tuple(jnp.moveaxis(t, 2, 0) for t in (u_val, k_cum, qk, q_dec, k_dec, c_dec))
    _, o = lax.scan(step, jnp.zeros((b_, h_, dk, dv), jnp.float32), xs)
    return _from_chunks(o)


def gdn_mixer(qkv, z, ba, conv_w, a_log, dt_bias, norm_g):
    B, S, _ = qkv.shape
    f32 = jnp.float32
    qkv = jax.nn.silu(dwconv_centred(qkv, conv_w)).astype(f32)
    q, k, v = jnp.split(qkv, [GDN_QK_W, 2 * GDN_QK_W], axis=-1)
    q = l2norm(q.reshape(B, S, GDN_HEADS, GDN_DK)) * GDN_DK ** -0.5
    k = l2norm(k.reshape(B, S, GDN_HEADS, GDN_DK))
    v = v.reshape(B, S, GDN_HEADS, GDN_DV)
    ba = ba.astype(f32).reshape(B, S, 4, GDN_HEADS)
    beta = jax.nn.sigmoid(ba[:, :, :2])
    log_a = -jnp.exp(a_log.astype(f32)) * jax.nn.softplus(ba[:, :, 2:] + dt_bias.astype(f32))
    o_f = gdn_scan(q, k, v, beta[:, :, 0], log_a[:, :, 0])
    o_b = _flip(gdn_scan(_flip(q), _flip(k), _flip(v), _flip(beta[:, :, 1]), _flip(log_a[:, :, 1])))
    o = rmsnorm(o_f + o_b, norm_g) * jax.nn.silu(z.astype(f32)).reshape(B, S, GDN_HEADS, GDN_DV)
    return o.reshape(B, S, GDN_V_W).astype(z.dtype)


def gated_merge(h, branches, merge_w, merge_b, branch_up):
    out = None
    for i, y in enumerate(branches):
        term = jax.nn.sigmoid(h @ merge_w[i] + merge_b[i]) * (y @ branch_up[i])
        out = term if out is None else out + term
    return out


def memory_cross_attention(h, mem_n, w_q, w_kv, w_o):
    B, S, D = h.shape
    M = mem_n.shape[1]
    q = (h @ w_q).reshape(B, S, XATTN_HEADS, XATTN_DH)
    kv = (mem_n @ w_kv).reshape(B, M, 2, XATTN_HEADS, XATTN_DH)
    k, v = kv[:, :, 0], kv[:, :, 1]
    s = jnp.einsum('bshd,bmhd->bhsm', q, k).astype(jnp.float32) * XATTN_DH ** -0.5
    p = jax.nn.softmax(s, axis=-1)
    o = jnp.einsum('bhsm,bmhd->bshd', p.astype(v.dtype), v).reshape(B, S, D)
    return o @ w_o


def hier_moe(h, w_group, b_group, w_expert, b_expert, w1, w3, w2):
    B, S, D = h.shape
    T = B * S
    f32 = jnp.float32
    ht = h.reshape(T, D)
    g_logits = (ht @ w_group).astype(f32) + b_group.astype(f32)
    g_prob = jax.nn.softmax(g_logits, axis=-1)
    _, g_idx = lax.top_k(g_logits, 1)
    g_w = jnp.take_along_axis(g_prob, g_idx, axis=1)
    e_logits = ((ht @ w_expert).astype(f32) + b_expert.astype(f32)).reshape(T, N_GROUPS, EXPERTS_PER_GROUP)
    e_logits = jnp.take_along_axis(
        e_logits, jnp.broadcast_to(g_idx[:, :, None], (T, 1, EXPERTS_PER_GROUP)), axis=1)[:, 0]
    e_top, e_idx = lax.top_k(e_logits, TOP_K)
    gate = jax.nn.softmax(e_top, axis=-1) * g_w
    eid = (g_idx * EXPERTS_PER_GROUP + e_idx).reshape(-1)
    tok = jnp.repeat(jnp.arange(T, dtype=jnp.int32), TOP_K)
    wts = gate.reshape(-1)
    A = T * TOP_K
    n_blk = -(-A // MOE_BLOCK) + N_EXPERTS
    P = n_blk * MOE_BLOCK
    order = jnp.argsort(eid)
    eid_s, tok_s, w_s = eid[order], tok[order], wts[order]
    counts = jnp.zeros((N_EXPERTS,), jnp.int32).at[eid].add(1)
    padded = (counts + MOE_BLOCK - 1) // MOE_BLOCK * MOE_BLOCK
    pad_end = jnp.cumsum(padded)
    pad_start = pad_end - padded
    seg_start = jnp.cumsum(counts) - counts
    dest = pad_start[eid_s] + jnp.arange(A, dtype=jnp.int32) - seg_start[eid_s]
    row_tok = jnp.full((P,), T, jnp.int32).at[dest].set(tok_s)
    row_w = jnp.zeros((P,), f32).at[dest].set(w_s)
    blk_expert = jnp.minimum(
        jnp.searchsorted(pad_end, jnp.arange(n_blk, dtype=jnp.int32) * MOE_BLOCK, side='right'),
        N_EXPERTS - 1)
    h_pad = jnp.concatenate([ht, jnp.zeros((1, D), ht.dtype)], axis=0)
    xb = h_pad[row_tok].reshape(n_blk, MOE_BLOCK, D)

    def expert_block(args):
        xi, e = args
        return (jax.nn.silu(xi @ w1[e]) * (xi @ w3[e])) @ w2[e]

    yb = lax.map(expert_block, (xb, blk_expert)).reshape(P, D)
    out = jnp.zeros((T + 1, D), h.dtype).at[row_tok].add(yb * row_w[:, None].astype(yb.dtype))
    return out[:T].reshape(B, S, D)


def kernel(x, mem, mix_norm, w_in, gla_lr_up, gla_lr_bias, gla_norm, lru_conv_w, lru_conv_b, lru_w_a, lru_b_a, lru_w_x, lru_b_x, lru_lambda, diff_lq1, diff_lk1, diff_lq2, diff_lk2, diff_norm, gdn_conv_w, gdn_a_log, gdn_dt_bias, gdn_norm, merge_w, merge_b, branch_up, mix_out, xattn_norm, mem_norm, xattn_wq, xattn_wkv, xattn_wo, moe_norm, moe_w_group, moe_b_group, moe_w_expert, moe_b_expert, moe_w1, moe_w3, moe_w2, final_norm):
    B, S, D = x.shape
    T = B * S
    xt = x.reshape(T, D)
    for l in range(DEPTH):
        gla_in, gog, lx, lg, dq, dk, dv, nqkv, nz, nba = inproj_pallas(xt, mix_norm[l], w_in[l])
        gla_o = gla_pallas(gla_in.reshape(B, S, -1), gla_lr_up[l], gla_lr_bias[l])
        lru_h = lru_pallas(lx.reshape(B, S, -1), lru_conv_w[l], lru_conv_b[l], lru_w_a[l], lru_b_a[l],
                           lru_w_x[l], lru_b_x[l], lru_lambda[l])
        y_diff = diff_attention_pallas(dq.reshape(B, S, -1), dk.reshape(B, S, -1), dv.reshape(B, S, -1),
                                       diff_lq1[l], diff_lk1[l], diff_lq2[l], diff_lk2[l], diff_norm[l],
                                       0.8 - 0.6 * math.exp(-0.3 * l))
        gdn_o = gdn_pallas(nqkv.reshape(B, S, -1), nba.reshape(B, S, -1), gdn_conv_w[l], gdn_a_log[l],
                           gdn_dt_bias[l])
        xt = merge_pallas(xt, gla_o.reshape(2, T, -1), gog, lru_h.reshape(2, T, -1), lg, y_diff.reshape(T, -1),
                          gdn_o.reshape(2, T, -1), nz, mix_norm[l], gla_norm[l], gdn_norm[l],
                          merge_w[l], merge_b[l], branch_up[l], mix_out[l])
        xt = xattn_pallas(xt.reshape(B, S, D), mem, xattn_norm[l], mem_norm[l], xattn_wq[l], xattn_wkv[l],
                          xattn_wo[l]).reshape(T, D)
        xt = moe_pallas(xt, moe_norm[l], moe_w_group[l], moe_b_group[l], moe_w_expert[l], moe_b_expert[l],
                        moe_w1[l], moe_w3[l], moe_w2[l], final_g=final_norm if l == DEPTH - 1 else None)
    return xt.reshape(B, S, D)
```

```python
import functools
import math

import jax
import jax.numpy as jnp
import numpy as np
from jax import lax
from jax.experimental import pallas as pl
from jax.experimental.pallas import tpu as pltpu

D_MODEL = 1024
DEPTH = 2
N_BRANCHES = 4
BRANCH_WIDTH = D_MODEL // 4
RMS_EPS = 1e-6
CONV_WIDTH = 4

GLA_HEADS = 4
GLA_DV = BRANCH_WIDTH // GLA_HEADS
GLA_DK = GLA_DV // 2
GLA_RANK = 16
GLA_GATE_NORM = 16.0
GLA_CHUNK = 64
GLA_QK_W = GLA_HEADS * GLA_DK
GLA_V_W = GLA_HEADS * GLA_DV

LRU_WIDTH = BRANCH_WIDTH
LRU_BLOCKS = 4
LRU_C = 8.0

DIFF_HEADS = 4
DIFF_DV = BRANCH_WIDTH // DIFF_HEADS
DIFF_DH = DIFF_DV // 2
DIFF_QK_W = DIFF_HEADS * 2 * DIFF_DH
DIFF_V_W = DIFF_HEADS * DIFF_DV

GDN_HEADS = 4
GDN_DK = BRANCH_WIDTH // GDN_HEADS
GDN_DV = BRANCH_WIDTH // GDN_HEADS
GDN_CHUNK = 64
GDN_QK_W = GDN_HEADS * GDN_DK
GDN_V_W = GDN_HEADS * GDN_DV

IN_SPLITS = (GLA_QK_W, GLA_QK_W, GLA_V_W, GLA_V_W, 2 * GLA_RANK,
             LRU_WIDTH, LRU_WIDTH,
             DIFF_QK_W, DIFF_QK_W, DIFF_V_W,
             2 * GDN_QK_W + GDN_V_W, GDN_V_W, 4 * GDN_HEADS)

XATTN_HEADS = 4
XATTN_DH = D_MODEL // XATTN_HEADS

N_GROUPS = 4
EXPERTS_PER_GROUP = 8
N_EXPERTS = N_GROUPS * EXPERTS_PER_GROUP
TOP_K = 2
D_EXPERT = D_MODEL // 2
MOE_BLOCK = 128

LANES = 128
VMEM_LIMIT = 56 * 1024 * 1024

F32 = jnp.float32
BF16 = jnp.bfloat16
NEG_BIG = -1e30


def rmsnorm(x, g):
    xf = x.astype(jnp.float32)
    y = xf * lax.rsqrt(jnp.mean(xf * xf, axis=-1, keepdims=True) + RMS_EPS)
    return (y * g.astype(jnp.float32)).astype(x.dtype)


def l2norm(t):
    return t * lax.rsqrt(jnp.sum(t * t, axis=-1, keepdims=True) + 1e-6)


def _flip(t):
    return jnp.flip(t, axis=1)


def dwconv_centred(x, w):
    k = w.shape[0]
    return lax.conv_general_dilated(
        x, w[:, None, :].astype(x.dtype), window_strides=(1,),
        padding=[(k // 2, k - 1 - k // 2)],
        dimension_numbers=('NWC', 'WIO', 'NWC'),
        feature_group_count=x.shape[-1])


def _to_chunks(t, chunk):
    b, s, h = t.shape[:3]
    t = t.reshape((b, s // chunk, chunk, h) + t.shape[3:])
    return jnp.moveaxis(t, 3, 1)


def _from_chunks(o):
    nc, b, h, c, d = o.shape
    return o.transpose(1, 0, 3, 2, 4).reshape(b, nc * c, h, d)


def _linrec_combine(c1, c2):
    a1, b1 = c1
    a2, b2 = c2
    return a1 * a2, a2 * b1 + b2


def gla_scan(q, k, v, log_f):
    b_, s_, h_, dk = q.shape
    dv = v.shape[-1]
    c = GLA_CHUNK
    incl = jnp.tril(jnp.ones((c, c), dtype=bool))[:, :, None]

    def step(state, inp):
        q_i, k_i, v_i, g_i = inp
        bcum = jnp.cumsum(g_i, axis=2)
        decay = jnp.exp(jnp.where(incl, bcum[:, :, :, None, :] - bcum[:, :, None, :, :], -jnp.inf))
        scores = jnp.einsum('bhtd,bhsd,bhtsd->bhts', q_i, k_i, decay)
        o = (jnp.einsum('bhts,bhse->bhte', scores, v_i)
             + jnp.einsum('bhtd,bhde->bhte', q_i * jnp.exp(bcum), state))
        b_end = bcum[:, :, -1:, :]
        state = (jnp.exp(b_end)[:, :, 0, :, None] * state
                 + jnp.einsum('bhsd,bhse->bhde', k_i * jnp.exp(b_end - bcum), v_i))
        return state, o

    xs = tuple(jnp.moveaxis(_to_chunks(t, c), 2, 0) for t in (q, k, v, log_f))
    _, o = lax.scan(step, jnp.zeros((b_, h_, dk, dv), jnp.float32), xs)
    return _from_chunks(o)


def gla_mixer(q, k, v, og, lr, lr_up, lr_bias, norm_g):
    B, S, _ = q.shape
    f32 = jnp.float32
    qh = q.astype(f32).reshape(B, S, GLA_HEADS, GLA_DK) * GLA_DK ** -0.5
    kh = k.astype(f32).reshape(B, S, GLA_HEADS, GLA_DK)
    vh = v.astype(f32).reshape(B, S, GLA_HEADS, GLA_DV)
    lr = lr.astype(f32).reshape(B, S, 2, GLA_RANK)
    z = jnp.einsum('bsdr,drk->bsdk', lr, lr_up.astype(f32)) + lr_bias.astype(f32)
    log_f = (jax.nn.log_sigmoid(z) / GLA_GATE_NORM).reshape(B, S, 2, GLA_HEADS, GLA_DK)
    o_f = gla_scan(qh, kh, vh, log_f[:, :, 0])
    o_b = _flip(gla_scan(_flip(qh), _flip(kh), _flip(vh), _flip(log_f[:, :, 1])))
    o = rmsnorm(o_f + o_b, norm_g) * jax.nn.silu(og.astype(f32)).reshape(B, S, GLA_HEADS, GLA_DV)
    return o.reshape(B, S, GLA_V_W).astype(q.dtype)


def rglru_mixer(xb, gb, conv_w, conv_b, w_a, b_a, w_x, b_x, lam):
    B, S, W = xb.shape
    f32 = jnp.float32
    u = (dwconv_centred(xb, conv_w) + conv_b).astype(f32)
    ub = u.reshape(B, S, LRU_BLOCKS, W // LRU_BLOCKS)

    def gate(w, b):
        return jax.nn.sigmoid(jnp.einsum('bsni,nio->bsno', ub, w.astype(f32)).reshape(B, S, W) + b.astype(f32))

    def direction(d, reverse):
        r = gate(w_a[d], b_a[d])
        i = gate(w_x[d], b_x[d])
        log_a = -LRU_C * r * jax.nn.softplus(-lam[d].astype(f32))
        a = jnp.exp(log_a)
        xin = jnp.sqrt(-jnp.expm1(2.0 * log_a)) * (i * u)
        _, hs = lax.associative_scan(_linrec_combine, (a, xin), axis=1, reverse=reverse)
        return hs

    hsum = direction(0, False) + direction(1, True)
    return (hsum * jax.nn.gelu(gb.astype(f32))).astype(xb.dtype)


def _cparams(*sem):
    return pltpu.CompilerParams(dimension_semantics=sem, vmem_limit_bytes=VMEM_LIMIT)


def _split_bf16(a, n):
    parts, r = [], a
    for i in range(n):
        p = r.astype(BF16)
        parts.append(p)
        if i + 1 < n:
            r = r - p.astype(F32)
    return parts


def _split_trunc_bf16(a, n):
    parts, r = [], a
    for i in range(n):
        bits = lax.bitcast_convert_type(r, jnp.uint32) & jnp.uint32(0xFFFF0000)
        p = lax.bitcast_convert_type(bits, F32)
        parts.append(p.astype(BF16))
        if i + 1 < n:
            r = r - p
    return parts


def _dot_nn(a, b):
    return jnp.dot(a, b, preferred_element_type=F32)


def _dot_nt(a, b):
    return lax.dot_general(a, b, (((1,), (1,)), ((), ())), preferred_element_type=F32)


def _dot_tn(a, b):
    return lax.dot_general(a, b, (((0,), (0,)), ((), ())), preferred_element_type=F32)


def _dot_split(a, b_exact, n=3, dot=_dot_nn):
    acc = None
    for p in _split_bf16(a, n):
        t = dot(p, b_exact)
        acc = t if acc is None else acc + t
    return acc


def _dot_f32(a, b, dot=_dot_nn):
    a_hi, a_lo = _split_bf16(a, 2)
    b_hi, b_lo = _split_bf16(b, 2)
    return dot(a_hi, b_hi) + (dot(a_hi, b_lo) + dot(a_lo, b_hi))


def _rms_rows(x, g):
    return x * lax.rsqrt(jnp.mean(x * x, axis=-1, keepdims=True) + RMS_EPS) * g


def _sigmoid(x):
    return 1.0 / (1.0 + jnp.exp(-x))


def _softplus(x):
    return jnp.maximum(x, 0.0) + jnp.log(1.0 + jnp.exp(-jnp.abs(x)))


def _silu(x):
    return x * _sigmoid(x)


_G_GLA = (0, 640)
_G_GOG = (640, 896)
_G_LX = (896, 1152)
_G_LG = (1152, 1408)
_G_DQ = (1408, 1664)
_G_DK = (1664, 1920)
_G_DV = (1920, 2176)
_G_NQKV = (2176, 2944)
_G_NZ = (2944, 3200)
_G_NBA = (3200, 3328)
IN_PAD_W = 3328
IN_TM = 512


def _permute_w_in(w):
    z = lambda n: jnp.zeros((w.shape[0], n), w.dtype)
    return jnp.concatenate([w[:, 0:512], w[:, 768:800], z(96), w[:, 512:768], w[:, 800:3120], z(112)],
                           axis=1).astype(BF16)


def _inproj_kernel(x_ref, g_ref, w_ref, gla_ref, gog_ref, lx_ref, lg_ref, dq_ref, dk_ref, dv_ref,
                   nqkv_ref, nz_ref, nba_ref):
    h = _rms_rows(x_ref[...], g_ref[...]).astype(BF16)

    def proj(grp):
        return jnp.dot(h, w_ref[:, grp[0]:grp[1]], preferred_element_type=F32)

    gla_ref[...] = proj(_G_GLA)
    gog_ref[...] = proj(_G_GOG).astype(BF16)
    lx_ref[...] = proj(_G_LX)
    lg_ref[...] = proj(_G_LG).astype(BF16)
    dq_ref[...] = (proj(_G_DQ) * (DIFF_DH ** -0.5 * LOG2E)).astype(BF16)
    dk_ref[...] = proj(_G_DK).astype(BF16)
    dv_ref[...] = proj(_G_DV).astype(BF16)
    nqkv_ref[...] = proj(_G_NQKV)
    nz_ref[...] = proj(_G_NZ).astype(BF16)
    nba_ref[...] = proj(_G_NBA)


def inproj_pallas(xt, g, w_in):
    T, D = xt.shape
    tm = min(IN_TM, T)
    groups = [(_G_GLA, F32), (_G_GOG, BF16), (_G_LX, F32), (_G_LG, BF16), (_G_DQ, BF16), (_G_DK, BF16),
              (_G_DV, BF16), (_G_NQKV, F32), (_G_NZ, BF16), (_G_NBA, F32)]
    return pl.pallas_call(
        _inproj_kernel,
        grid=(T // tm,),
        in_specs=[pl.BlockSpec((tm, D), lambda i: (i, 0)),
                  pl.BlockSpec((1, D), lambda i: (0, 0)),
                  pl.BlockSpec((D, IN_PAD_W), lambda i: (0, 0))],
        out_specs=[pl.BlockSpec((tm, b - a), lambda i: (i, 0)) for (a, b), _ in groups],
        out_shape=[jax.ShapeDtypeStruct((T, b - a), dt) for (a, b), dt in groups],
        compiler_params=_cparams("parallel"),
        name="inproj",
    )(xt, g.reshape(1, D).astype(F32), _permute_w_in(w_in))


def _gla_kernel(x_ref, tri_ref, w_ref, b_ref, o_ref, st_ref, *, nb):
    c = GLA_CHUNK

    @pl.when(pl.program_id(1) == 0)
    def _():
        st_ref[...] = jnp.zeros(st_ref.shape, F32)

    tri = tri_ref[0]
    tri_bf = tri.astype(BF16)
    tri4 = jnp.concatenate([tri] * GLA_HEADS, axis=0)
    w = w_ref[0]
    bias = b_ref[0]
    lane_qk = lax.broadcasted_iota(jnp.int32, (c, GLA_QK_W), 1) // GLA_DK
    lane_v = lax.broadcasted_iota(jnp.int32, (c, GLA_V_W), 1) // GLA_DV
    row_s = lax.broadcasted_iota(jnp.int32, (GLA_V_W, GLA_QK_W), 0) // GLA_DV
    col_s = lax.broadcasted_iota(jnp.int32, (GLA_V_W, GLA_QK_W), 1) // GLA_DK
    st_mask = row_s == col_s

    for b in range(nb):
        blk = x_ref[b]
        q = blk[:, 0:128] * GLA_DK ** -0.5
        k = blk[:, 128:256]
        v = blk[:, 256:512].astype(BF16)
        lr = blk[:, 512:640]
        z = _dot_f32(lr, w) + bias
        g = (jnp.minimum(z, 0.0) - jnp.log(1.0 + jnp.exp(-jnp.abs(z)))) * (1.0 / GLA_GATE_NORM)
        acc = None
        for p in _split_bf16(g, 3):
            t = _dot_nn(tri_bf, p)
            acc = t if acc is None else acc + t
        bc = acc
        tot = jnp.sum(g, axis=0, keepdims=True)
        ref = 0.5 * tot
        qt = q * jnp.exp(bc - ref)
        kt = (k * jnp.exp(ref - bc)).astype(BF16)
        qd = (q * jnp.exp(bc)).astype(BF16)
        kd = (k * jnp.exp(tot - bc)).astype(BF16)
        qstack = jnp.concatenate(
            [jnp.where(lane_qk == hh, qt, 0.0) for hh in range(GLA_HEADS)], axis=0).astype(BF16)
        s = _dot_nt(qstack, kt) * tri4
        st = st_ref[b]
        o = _dot_nt(qd, st.astype(BF16))
        for hh in range(GLA_HEADS):
            oh = _dot_nn(s[hh * c:(hh + 1) * c].astype(BF16), v)
            o = o + jnp.where(lane_v == hh, oh, 0.0)
        o_ref[0, b] = o
        upd = _dot_tn(v, kd)
        st_ref[b] = st * jnp.exp(tot) + jnp.where(st_mask, upd, 0.0)


def gla_pallas(gla_in, lr_up, lr_bias):
    B, S, W = gla_in.shape
    c = GLA_CHUNK
    nc = S // c
    r = np.arange(c)
    tri = jnp.asarray(np.stack([r[:, None] >= r[None, :], r[:, None] <= r[None, :]]).astype(np.float32))
    w = jnp.zeros((2, LANES, GLA_QK_W), F32)
    w = w.at[0, 0:GLA_RANK].set(lr_up[0].astype(F32)).at[1, GLA_RANK:2 * GLA_RANK].set(lr_up[1].astype(F32))
    bias = lr_bias.astype(F32).reshape(2, 1, GLA_QK_W)

    def tmap(d, i):
        return (0, i + d * (nc - 1 - 2 * i), 0)

    return pl.pallas_call(
        functools.partial(_gla_kernel, nb=B),
        grid=(2, nc),
        in_specs=[pl.BlockSpec((B, c, W), tmap),
                  pl.BlockSpec((1, c, c), lambda d, i: (d, 0, 0)),
                  pl.BlockSpec((1, LANES, GLA_QK_W), lambda d, i: (d, 0, 0)),
                  pl.BlockSpec((1, 1, GLA_QK_W), lambda d, i: (d, 0, 0))],
        out_specs=pl.BlockSpec((1, B, c, GLA_V_W), lambda d, i: (d,) + tmap(d, i)),
        out_shape=jax.ShapeDtypeStruct((2, B, S, GLA_V_W), F32),
        scratch_shapes=[pltpu.VMEM((B, GLA_V_W, GLA_QK_W), F32)],
        compiler_params=_cparams("arbitrary", "arbitrary"),
        name="gla",
    )(gla_in, tri, w, bias)


LRU_TB = 512
HALO = 8


def _halo_specs(tb, width, nt):
    r = tb // HALO
    return [pl.BlockSpec((1, HALO, width), lambda b, i: (b, jnp.maximum(i * r - 1, 0), 0)),
            pl.BlockSpec((1, tb, width), lambda b, i: (b, i, 0)),
            pl.BlockSpec((1, HALO, width), lambda b, i: (b, jnp.minimum((i + 1) * r, nt * r - 1), 0))]


def _conv4(prev_ref, cur_ref, next_ref, w, nt):
    i = pl.program_id(1)
    cur = cur_ref[0]
    tb = cur.shape[0]
    prev = prev_ref[0] * jnp.where(i > 0, 1.0, 0.0)
    nxt = next_ref[0] * jnp.where(i < nt - 1, 1.0, 0.0)
    ext = jnp.concatenate([prev, cur, nxt], axis=0)
    out = None
    for j in range(CONV_WIDTH):
        off = HALO + j - CONV_WIDTH // 2
        t = ext[off:off + tb] * w[j:j + 1]
        out = t if out is None else out + t
    return out


def _lru_prep_kernel(prev_ref, cur_ref, next_ref, cw_ref, cb_ref, wg_ref, bg_ref, lam_ref, a_ref, x_ref, *, nt):
    u = _conv4(prev_ref, cur_ref, next_ref, cw_ref[...], nt) + cb_ref[...]
    gates = _sigmoid(jnp.dot(u.astype(BF16), wg_ref[...], preferred_element_type=F32) + bg_ref[...])
    w = LRU_WIDTH
    for d in range(2):
        r = gates[:, (2 * d) * w:(2 * d + 1) * w]
        ig = gates[:, (2 * d + 1) * w:(2 * d + 2) * w]
        log_a = -LRU_C * r * _softplus(-lam_ref[d:d + 1])
        a_ref[d, 0] = jnp.exp(log_a)
        x_ref[d, 0] = jnp.sqrt(1.0 - jnp.exp(2.0 * log_a)) * (ig * u)


def _lru_scan_kernel(a_ref, x_ref, h_ref, st_ref, *, tb):
    d = pl.program_id(0)

    @pl.when(pl.program_id(1) == 0)
    def _():
        st_ref[...] = jnp.zeros(st_ref.shape, F32)

    def body(t, h):
        tt = t + d * (tb - 1 - 2 * t)
        h = a_ref[0, :, pl.ds(tt, 1), :] * h + x_ref[0, :, pl.ds(tt, 1), :]
        h_ref[0, :, pl.ds(tt, 1), :] = h
        return h

    st_ref[...] = lax.fori_loop(0, tb, body, st_ref[...], unroll=8)


def lru_pallas(lx, conv_w, conv_b, w_a, b_a, w_x, b_x, lam):
    B, S, W = lx.shape
    tb = min(LRU_TB, S)
    nt = S // tb
    blk = W // LRU_BLOCKS

    def dense(wb):
        m = jnp.zeros((W, W), F32)
        for n in range(LRU_BLOCKS):
            m = m.at[n * blk:(n + 1) * blk, n * blk:(n + 1) * blk].set(wb[n].astype(F32))
        return m

    wg = jnp.concatenate([dense(w_a[0]), dense(w_x[0]), dense(w_a[1]), dense(w_x[1])], axis=1).astype(BF16)
    bg = jnp.concatenate([b_a[0], b_x[0], b_a[1], b_x[1]]).astype(F32).reshape(1, 4 * W)
    full = lambda shape: pl.BlockSpec(shape, lambda b, i: (0,) * len(shape))
    a, xin = pl.pallas_call(
        functools.partial(_lru_prep_kernel, nt=nt),
        grid=(B, nt),
        in_specs=_halo_specs(tb, W, nt) + [full((CONV_WIDTH, W)), full((1, W)), full((W, 4 * W)), full((1, 4 * W)),
                                           full((2, W))],
        out_specs=[pl.BlockSpec((2, 1, tb, W), lambda b, i: (0, b, i, 0))] * 2,
        out_shape=[jax.ShapeDtypeStruct((2, B, S, W), F32)] * 2,
        compiler_params=_cparams("parallel", "parallel"),
        name="lru_prep",
    )(lx, lx, lx, conv_w.astype(F32), conv_b.astype(F32).reshape(1, W), wg, bg, lam.astype(F32))

    def tmap(d, i):
        return (d, 0, i + d * (nt - 1 - 2 * i), 0)

    return pl.pallas_call(
        functools.partial(_lru_scan_kernel, tb=tb),
        grid=(2, nt),
        in_specs=[pl.BlockSpec((1, B, tb, W), tmap)] * 2,
        out_specs=pl.BlockSpec((1, B, tb, W), tmap),
        out_shape=jax.ShapeDtypeStruct((2, B, S, W), F32),
        scratch_shapes=[pltpu.VMEM((B, 1, W), F32)],
        compiler_params=_cparams("arbitrary", "arbitrary"),
        name="lru_scan",
    )(a, xin)


DIFF_TQ = 256
DIFF_TK = 512
DIFF_POS_LANE = 2 * DIFF_DH
DIFF_POS_PIECES = 3
DIFF_UNROLL = 4
DIFF_VT_ROWS = DIFF_DV + 8
LOG2E = 1.4426950408889634


def _diff_attn_kernel(slopes_ref, lam_ref, q_ref, k_ref, vt_ref, g_ref, o_ref,
                      qv_scr, rt_scr, s_scr, mb_scr, p_scr, al_scr, m_scr, acc_scr, *, tq, tk, seq, out_scale):
    h = pl.program_id(1)
    qi = pl.program_id(2)
    slope = slopes_ref[h]
    lam = lam_ref[0]
    nk = seq // tk

    q = q_ref[0, 0]
    lane = lax.broadcasted_iota(jnp.int32, q.shape, 1)
    zero = jnp.zeros_like(q)
    q1 = jnp.where(jnp.abs(2 * lane - (DIFF_DH + DIFF_POS_LANE - 1)) < DIFF_DH, zero, q)
    q2 = jnp.where(lane < DIFF_DH, zero, q)
    qs = jnp.concatenate([q1, q2], axis=0)
    lane2 = lax.broadcasted_iota(jnp.int32, qs.shape, 1)
    col = lax.broadcasted_iota(jnp.int32, (1, 2 * tq), 1)
    qpos = qi * tq + jnp.where(col >= tq, col - tq, col)
    shift_q = slope * qpos.astype(F32)
    qv_scr[0] = qs
    qv_scr[1] = jnp.where(lane2 >= DIFF_POS_LANE, -qs, qs)
    rt_scr[0] = -shift_q
    rt_scr[1] = shift_q

    kd = (qi * tq) // tk
    k0 = pl.multiple_of(kd * tk, tk)
    q_mid = jnp.where(lane2 >= DIFF_POS_LANE, jnp.zeros_like(qs), qs)
    s = _dot_nt(k_ref[0, 0, pl.ds(k0, tk), :], q_mid)
    kpos = k0 + lax.broadcasted_iota(jnp.int32, (tk, 2 * tq), 0)
    s = s - slope * jnp.abs(qpos - kpos).astype(F32)
    m0 = jnp.max(s, axis=0, keepdims=True)
    p = jnp.exp2(s - m0).astype(BF16)
    acc_scr[...] = jnp.dot(vt_ref[0, 0, :, pl.ds(k0, tk)], p, preferred_element_type=F32)
    m_scr[...] = m0

    def tile_of(t):
        return t + jnp.where(t >= kd, 1, 0)

    def scores(t, slot):
        kj = tile_of(t)
        var = jnp.where(kj > kd, 1, 0)
        kt = k_ref[0, 0, pl.ds(pl.multiple_of(kj * tk, tk), tk), :]
        sc = _dot_nt(kt, qv_scr[var])
        s_scr[slot] = sc
        mb_scr[slot] = jnp.max(sc, axis=0, keepdims=True) + rt_scr[var]

    def softmax(t, slot):
        kj = tile_of(t)
        var = jnp.where(kj > kd, 1, 0)
        m_old = m_scr[...]
        m_new = jnp.maximum(m_old, mb_scr[slot])
        al_scr[slot] = jnp.exp2(m_old - m_new)
        p_scr[slot] = jnp.exp2(s_scr[slot] - (m_new - rt_scr[var])).astype(BF16)
        m_scr[...] = m_new

    def pv(t, slot):
        vt = vt_ref[0, 0, :, pl.ds(pl.multiple_of(tile_of(t) * tk, tk), tk)]
        acc_scr[...] = al_scr[slot] * acc_scr[...] + jnp.dot(vt, p_scr[slot], preferred_element_type=F32)

    n_off = nk - 1
    if n_off == 1:
        scores(0, 0)
        softmax(0, 0)
        pv(0, 0)
    elif n_off >= 2:
        scores(0, 0)
        softmax(0, 0)
        scores(1, 1)

        def trip(t, slot):
            pv(t, slot)
            scores(t + 2, slot)
            softmax(t + 1, 1 - slot)

        def body(i, c):
            for u in range(DIFF_UNROLL):
                trip(DIFF_UNROLL * i + u, u % 2)
            return c

        n_trips = n_off - 2
        n_loop = n_trips // DIFF_UNROLL
        lax.fori_loop(0, n_loop, body, 0)
        for t in range(n_loop * DIFF_UNROLL, n_trips):
            trip(t, t % 2)
        last = n_off - 1
        pv(last - 1, (last - 1) % 2)
        softmax(last, last % 2)
        pv(last, last % 2)

    acc = acc_scr[...]
    o2 = acc[0:DIFF_DV] * (1.0 / acc[DIFF_DV:DIFF_DV + 1])
    o = o2[:, :tq] - lam * o2[:, tq:]
    ms = jnp.mean(o * o, axis=0, keepdims=True)
    o = o * lax.rsqrt(ms + RMS_EPS) * g_ref[...] * out_scale
    o_ref[0, 0] = o.astype(o_ref.dtype)


def diff_attention_pallas(dq, dk, dv, lq1, lk1, lq2, lk2, norm_g, lambda_init):
    B, S, _ = dq.shape
    H = DIFF_HEADS
    tq = min(DIFF_TQ, S)
    tk = min(DIFF_TK, S)
    slopes = jnp.exp2(-8.0 * jnp.arange(1, H + 1, dtype=F32) / H) * LOG2E
    lam = (jnp.exp(jnp.sum(lq1.astype(F32) * lk1.astype(F32)))
           - jnp.exp(jnp.sum(lq2.astype(F32) * lk2.astype(F32))) + lambda_init).reshape(1)
    npos = DIFF_POS_PIECES
    pad = LANES - 2 * DIFF_DH - npos
    q4 = dq.astype(BF16).reshape(B, S, H, 2 * DIFF_DH).transpose(0, 2, 1, 3)
    q_aug = jnp.concatenate([q4, jnp.ones((B, H, S, npos), BF16), jnp.zeros((B, H, S, pad), BF16)], axis=-1)
    k4 = dk.astype(BF16).reshape(B, S, H, 2 * DIFF_DH).transpose(0, 2, 1, 3)
    kbias = slopes[:, None] * jnp.arange(S, dtype=F32)[None, :]
    kpos = jnp.stack(_split_trunc_bf16(kbias, npos), axis=-1)[None]
    k_aug = jnp.concatenate([k4, jnp.broadcast_to(kpos, (B, H, S, npos)), jnp.zeros((B, H, S, pad), BF16)],
                            axis=-1)
    v4 = dv.astype(BF16).reshape(B, S, H, DIFF_DV).transpose(0, 2, 3, 1)
    vt = jnp.concatenate([v4, jnp.ones((B, H, 1, S), BF16),
                          jnp.zeros((B, H, DIFF_VT_ROWS - DIFF_DV - 1, S), BF16)], axis=2)
    g = norm_g.astype(F32).reshape(DIFF_DV, 1)

    kern = functools.partial(_diff_attn_kernel, tq=tq, tk=tk, seq=S, out_scale=1.0 - lambda_init)
    out = pl.pallas_call(
        kern,
        grid=(B, H, S // tq),
        in_specs=[
            pl.BlockSpec(memory_space=pltpu.SMEM),
            pl.BlockSpec(memory_space=pltpu.SMEM),
            pl.BlockSpec((1, 1, tq, LANES), lambda b, h, i: (b, h, i, 0)),
            pl.BlockSpec((1, 1, S, LANES), lambda b, h, i: (b, h, 0, 0)),
            pl.BlockSpec((1, 1, DIFF_VT_ROWS, S), lambda b, h, i: (b, h, 0, 0)),
            pl.BlockSpec((DIFF_DV, 1), lambda b, h, i: (0, 0)),
        ],
        out_specs=pl.BlockSpec((1, 1, DIFF_DV, tq), lambda b, h, i: (b, h, 0, i)),
        out_shape=jax.ShapeDtypeStruct((B, H, DIFF_DV, S), F32),
        scratch_shapes=[
            pltpu.VMEM((2, 2 * tq, LANES), BF16),
            pltpu.VMEM((2, 1, 2 * tq), F32),
            pltpu.VMEM((2, tk, 2 * tq), F32),
            pltpu.VMEM((2, 1, 2 * tq), F32),
            pltpu.VMEM((2, tk, 2 * tq), BF16),
            pltpu.VMEM((2, 1, 2 * tq), F32),
            pltpu.VMEM((1, 2 * tq), F32),
            pltpu.VMEM((DIFF_VT_ROWS, 2 * tq), F32),
        ],
        compiler_params=pltpu.CompilerParams(
            dimension_semantics=("parallel", "parallel", "arbitrary"),
            vmem_limit_bytes=VMEM_LIMIT),
        name="diff_attn",
    )(slopes, lam, q_aug, k_aug, vt, g)
    return out.transpose(0, 3, 1, 2).reshape(B, S, DIFF_V_W)


MERGE_TM = 256


def _group_rms(o, ones_bd, g, width):
    ss = _dot_split(o * o, ones_bd, n=2)
    return o * lax.rsqrt(ss * (1.0 / width) + RMS_EPS) * g


def _gelu_tanh(x):
    return 0.5 * x * (1.0 + jnp.tanh(0.7978845608028654 * (x + 0.044715 * (x * x * x))))


def _merge_kernel(x_ref, gla_ref, gog_ref, lru_ref, lg_ref, dif_ref, gdn_ref, nz_ref,
                  mg_ref, glag_ref, gdng_ref, ones_ref, mw_ref, mb_ref, up_ref, wo_ref, o_ref):
    x = x_ref[...]
    h = _rms_rows(x, mg_ref[...]).astype(BF16)
    ones_bd = ones_ref[...]
    y_gla = _group_rms(gla_ref[0] + gla_ref[1], ones_bd, glag_ref[...], GLA_DV) * _silu(gog_ref[...].astype(F32))
    y_lru = (lru_ref[0] + lru_ref[1]) * _gelu_tanh(lg_ref[...].astype(F32))
    y_dif = dif_ref[...]
    y_gdn = _group_rms(gdn_ref[0] + gdn_ref[1], ones_bd, gdng_ref[...], GDN_DV) * _silu(nz_ref[...].astype(F32))
    merged = None
    for i, y in enumerate((y_gla, y_lru, y_dif, y_gdn)):
        gate = _sigmoid(jnp.dot(h, mw_ref[i], preferred_element_type=F32) + mb_ref[i])
        term = gate * jnp.dot(y.astype(BF16), up_ref[i], preferred_element_type=F32)
        merged = term if merged is None else merged + term
    o_ref[...] = x + jnp.dot(merged.astype(BF16), wo_ref[...], preferred_element_type=F32)


def merge_pallas(xt, gla_o, gog, lru_h, lg, y_diff, gdn_o, nz, mix_g, gla_g, gdn_g, merge_w, merge_b, branch_up,
                 mix_out):
    T, D = xt.shape
    tm = min(MERGE_TM, T)
    W = BRANCH_WIDTH
    r = np.arange(W) // GLA_DV
    ones_bd = jnp.asarray((r[:, None] == r[None, :]).astype(np.float32)).astype(BF16)
    tok = lambda w: pl.BlockSpec((tm, w), lambda i: (i, 0))
    tok2 = pl.BlockSpec((2, tm, W), lambda i: (0, i, 0))
    const = lambda shape: pl.BlockSpec(shape, lambda i: (0,) * len(shape), pipeline_mode=pl.Buffered(1))
    return pl.pallas_call(
        _merge_kernel,
        grid=(T // tm,),
        in_specs=[tok(D), tok2, tok(W), tok2, tok(W), tok(W), tok2, tok(W),
                  const((1, D)), const((1, W)), const((1, W)), const((W, W)),
                  const((N_BRANCHES, D, D)), const((N_BRANCHES, 1, D)), const((N_BRANCHES, W, D)), const((D, D))],
        out_specs=tok(D),
        out_shape=jax.ShapeDtypeStruct((T, D), F32),
        compiler_params=_cparams("parallel"),
        name="merge",
    )(xt, gla_o, gog, lru_h, lg, y_diff, gdn_o, nz,
      mix_g.reshape(1, D).astype(F32), jnp.tile(gla_g.astype(F32), GLA_HEADS).reshape(1, W),
      jnp.tile(gdn_g.astype(F32), GDN_HEADS).reshape(1, W), ones_bd,
      merge_w.astype(BF16), merge_b.astype(F32).reshape(N_BRANCHES, 1, D), branch_up.astype(BF16),
      mix_out.astype(BF16))


XATTN_TM = 512


def _kv_kernel(m_ref, g_ref, w_ref, o_ref):
    mn = _rms_rows(m_ref[0], g_ref[...]).astype(BF16)
    o_ref[0] = jnp.dot(mn, w_ref[...], preferred_element_type=F32).astype(BF16)


def _xattn_kernel(x_ref, g_ref, wq_ref, kv_ref, wo_ref, o_ref):
    x = x_ref[0]
    h = _rms_rows(x, g_ref[...]).astype(BF16)
    q = jnp.dot(h, wq_ref[...], preferred_element_type=F32).astype(BF16)
    outs = []
    for hh in range(XATTN_HEADS):
        lo = hh * XATTN_DH
        k = kv_ref[0, :, lo:lo + XATTN_DH]
        v = kv_ref[0, :, D_MODEL + lo:D_MODEL + lo + XATTN_DH]
        s = _dot_nt(q[:, lo:lo + XATTN_DH], k) * XATTN_DH ** -0.5
        p = jnp.exp(s - jnp.max(s, axis=-1, keepdims=True))
        p = p / jnp.sum(p, axis=-1, keepdims=True)
        outs.append(jnp.dot(p.astype(BF16), v, preferred_element_type=F32).astype(BF16))
    o = jnp.concatenate(outs, axis=-1)
    o_ref[0] = x + jnp.dot(o, wo_ref[...], preferred_element_type=F32)


def xattn_pallas(x, mem, xg, mg, wq, wkv, wo):
    B, S, D = x.shape
    M = mem.shape[1]
    tm = min(XATTN_TM, S)
    kv = pl.pallas_call(
        _kv_kernel,
        grid=(B,),
        in_specs=[pl.BlockSpec((1, M, D), lambda b: (b, 0, 0)),
                  pl.BlockSpec((1, D), lambda b: (0, 0)),
                  pl.BlockSpec((D, 2 * D), lambda b: (0, 0))],
        out_specs=pl.BlockSpec((1, M, 2 * D), lambda b: (b, 0, 0)),
        out_shape=jax.ShapeDtypeStruct((B, M, 2 * D), BF16),
        compiler_params=_cparams("parallel"),
        name="xattn_kv",
    )(mem, mg.reshape(1, D).astype(F32), wkv.astype(BF16))
    const = lambda shape: pl.BlockSpec(shape, lambda b, i: (0,) * len(shape), pipeline_mode=pl.Buffered(1))
    return pl.pallas_call(
        _xattn_kernel,
        grid=(B, S // tm),
        in_specs=[pl.BlockSpec((1, tm, D), lambda b, i: (b, i, 0)),
                  const((1, D)), const((D, D)),
                  pl.BlockSpec((1, M, 2 * D), lambda b, i: (b, 0, 0)),
                  const((D, D))],
        out_specs=pl.BlockSpec((1, tm, D), lambda b, i: (b, i, 0)),
        out_shape=jax.ShapeDtypeStruct((B, S, D), F32),
        compiler_params=_cparams("parallel", "parallel"),
        name="xattn",
    )(x, xg.reshape(1, D).astype(F32), wq.astype(BF16), kv, wo.astype(BF16))


ROUTE_TM = 512
ROUTE_EXPERT_LANE = 32
MOE_BM = 256
COMBINE_TM = 512


def _router_kernel(x_ref, g_ref, w_ref, b_ref, h_ref, r_ref):
    h = _rms_rows(x_ref[...], g_ref[...])
    h_ref[...] = h
    logits = _dot_f32(h, w_ref[...]) + b_ref[...]
    lane = lax.broadcasted_iota(jnp.int32, logits.shape, 1)
    big = jnp.int32(1 << 20)
    neg = jnp.float32(-jnp.inf)

    def top(vals):
        m = jnp.max(vals, axis=-1, keepdims=True)
        idx = jnp.min(jnp.where(vals == m, lane, big), axis=-1, keepdims=True)
        return m, idx

    is_g = lane < N_GROUPS
    gmax, gidx = top(jnp.where(is_g, logits, neg))
    gsum = jnp.sum(jnp.where(is_g, jnp.exp(logits - gmax), 0.0), axis=-1, keepdims=True)
    g_w = 1.0 / gsum
    lo = ROUTE_EXPERT_LANE + EXPERTS_PER_GROUP * gidx
    in_grp = jnp.abs(2 * (lane - lo) - (EXPERTS_PER_GROUP - 1)) < EXPERTS_PER_GROUP
    el = jnp.where(in_grp, logits, neg)
    e1, i1 = top(el)
    e2, i2 = top(jnp.where(lane == i1, neg, el))
    t = jnp.exp(e2 - e1)
    w1 = g_w / (1.0 + t)
    w2 = g_w * t / (1.0 + t)
    f = lambda v: v.astype(F32)
    r_ref[...] = jnp.where(lane == 0, f(i1 - ROUTE_EXPERT_LANE),
                           jnp.where(lane == 1, f(i2 - ROUTE_EXPERT_LANE),
                                     jnp.where(lane == 2, w1, jnp.where(lane == 3, w2, 0.0))))


def router_pallas(xt, g, w_group, b_group, w_expert, b_expert):
    T, D = xt.shape
    tm = min(ROUTE_TM, T)
    w = jnp.zeros((D, LANES), F32).at[:, :N_GROUPS].set(w_group.astype(F32))
    w = w.at[:, ROUTE_EXPERT_LANE:ROUTE_EXPERT_LANE + N_EXPERTS].set(w_expert.astype(F32))
    b = jnp.zeros((1, LANES), F32).at[0, :N_GROUPS].set(b_group.astype(F32))
    b = b.at[0, ROUTE_EXPERT_LANE:ROUTE_EXPERT_LANE + N_EXPERTS].set(b_expert.astype(F32))
    return pl.pallas_call(
        _router_kernel,
        grid=(T // tm,),
        in_specs=[pl.BlockSpec((tm, D), lambda i: (i, 0)),
                  pl.BlockSpec((1, D), lambda i: (0, 0)),
                  pl.BlockSpec((D, LANES), lambda i: (0, 0)),
                  pl.BlockSpec((1, LANES), lambda i: (0, 0))],
        out_specs=[pl.BlockSpec((tm, D), lambda i: (i, 0)), pl.BlockSpec((tm, LANES), lambda i: (i, 0))],
        out_shape=[jax.ShapeDtypeStruct((T, D), F32), jax.ShapeDtypeStruct((T, LANES), F32)],
        compiler_params=_cparams("parallel"),
        name="moe_router",
    )(xt, g.reshape(1, D).astype(F32), w, b)


def _gather_rows(idx_hbm_row, src_hbm, idx_smem, dst, sem_idx, sem_rows, n):
    cp = pltpu.make_async_copy(idx_hbm_row, idx_smem, sem_idx)
    cp.start()
    cp.wait()

    def row_copy(r):
        return pltpu.make_async_copy(src_hbm.at[pl.ds(idx_smem[r], 1)], dst.at[pl.ds(r, 1)], sem_rows)

    def issue(r, c):
        row_copy(r).start()
        return c

    def drain(r, c):
        row_copy(r).wait()
        return c

    lax.fori_loop(0, n, issue, 0)
    lax.fori_loop(0, n, drain, 0)


def _expert_kernel(be_ref, nu_ref, idx_hbm, h_hbm, w1_ref, w3_ref, w2_ref, y_ref, idx_smem, xbuf, sem_idx, sem_rows,
                   *, bm):
    i = pl.program_id(0)

    @pl.when(i < nu_ref[0])
    def _():
        _gather_rows(idx_hbm.at[i], h_hbm, idx_smem, xbuf, sem_idx, sem_rows, bm)
        xb = xbuf[...].astype(BF16)
        a = jnp.dot(xb, w1_ref[0], preferred_element_type=F32)
        g = jnp.dot(xb, w3_ref[0], preferred_element_type=F32)
        y_ref[...] = jnp.dot((_silu(a) * g).astype(BF16), w2_ref[0], preferred_element_type=F32)

    @pl.when(i >= nu_ref[0])
    def _():
        y_ref[...] = jnp.zeros(y_ref.shape, F32)


def _combine_kernel(x_ref, r_ref, pos_hbm, y_hbm, g_ref, o_ref, idx_smem, ybuf, sem_idx, sem_rows, *, tm, final):
    i = pl.program_id(0)
    _gather_rows(pos_hbm.at[i], y_hbm, idx_smem, ybuf, sem_idx, sem_rows, 2 * tm)
    r = r_ref[...]
    out = x_ref[...] + r[:, 2:3] * ybuf[0:tm] + r[:, 3:4] * ybuf[tm:2 * tm]
    if final:
        out = _rms_rows(out, g_ref[...])
    o_ref[...] = out


def moe_pallas(xt, g, w_group, b_group, w_expert, b_expert, w1, w3, w2, final_g=None):
    T, D = xt.shape
    bm = MOE_BM
    h, route = router_pallas(xt, g, w_group, b_group, w_expert, b_expert)
    eid = route[:, 0:2].astype(jnp.int32).reshape(-1)
    A = T * TOP_K
    n_blk = A // bm + N_EXPERTS
    P = n_blk * bm
    order = jnp.argsort(eid)
    eid_s = eid[order]
    tok_s = order // TOP_K
    counts = jnp.zeros((N_EXPERTS,), jnp.int32).at[eid].add(1)
    padded = (counts + bm - 1) // bm * bm
    pad_end = jnp.cumsum(padded)
    pad_start = pad_end - padded
    seg_start = jnp.cumsum(counts) - counts
    dest = pad_start[eid_s] + jnp.arange(A, dtype=jnp.int32) - seg_start[eid_s]
    row_tok = jnp.zeros((P,), jnp.int32).at[dest].set(tok_s.astype(jnp.int32))
    pos = jnp.zeros((A,), jnp.int32).at[order].set(dest)
    blk_expert = jnp.minimum(
        jnp.searchsorted(pad_end, jnp.arange(n_blk, dtype=jnp.int32) * bm, side='right'), N_EXPERTS - 1
    ).astype(jnp.int32)
    n_used = (pad_end[-1] // bm).astype(jnp.int32).reshape(1)

    y = pl.pallas_call(
        functools.partial(_expert_kernel, bm=bm),
        grid_spec=pltpu.PrefetchScalarGridSpec(
            num_scalar_prefetch=2,
            grid=(n_blk,),
            in_specs=[pl.BlockSpec(memory_space=pl.ANY),
                      pl.BlockSpec(memory_space=pl.ANY),
                      pl.BlockSpec((1, D, D_EXPERT), lambda i, be, nu: (be[i], 0, 0)),
                      pl.BlockSpec((1, D, D_EXPERT), lambda i, be, nu: (be[i], 0, 0)),
                      pl.BlockSpec((1, D_EXPERT, D), lambda i, be, nu: (be[i], 0, 0))],
            out_specs=pl.BlockSpec((bm, D), lambda i, be, nu: (i, 0)),
            scratch_shapes=[pltpu.SMEM((bm,), jnp.int32), pltpu.VMEM((bm, D), F32),
                            pltpu.SemaphoreType.DMA(()), pltpu.SemaphoreType.DMA(())]),
        out_shape=jax.ShapeDtypeStruct((P, D), F32),
        compiler_params=_cparams("arbitrary"),
        name="moe_experts",
    )(blk_expert, n_used, row_tok.reshape(n_blk, bm), h, w1.astype(BF16), w3.astype(BF16), w2.astype(BF16))

    tm = min(COMBINE_TM, T)
    nt = T // tm
    pos_t = pos.reshape(nt, tm, TOP_K).transpose(0, 2, 1).reshape(nt, TOP_K * tm)
    fg = (final_g if final_g is not None else jnp.ones((D,), F32)).reshape(1, D).astype(F32)
    return pl.pallas_call(
        functools.partial(_combine_kernel, tm=tm, final=final_g is not None),
        grid=(nt,),
        in_specs=[pl.BlockSpec((tm, D), lambda i: (i, 0)),
                  pl.BlockSpec((tm, LANES), lambda i: (i, 0)),
                  pl.BlockSpec(memory_space=pl.ANY),
                  pl.BlockSpec(memory_space=pl.ANY),
                  pl.BlockSpec((1, D), lambda i: (0, 0))],
        out_specs=pl.BlockSpec((tm, D), lambda i: (i, 0)),
        out_shape=jax.ShapeDtypeStruct((T, D), F32),
        scratch_shapes=[pltpu.SMEM((TOP_K * tm,), jnp.int32), pltpu.VMEM((TOP_K * tm, D), F32),
                        pltpu.SemaphoreType.DMA(()), pltpu.SemaphoreType.DMA(())],
        compiler_params=_cparams("arbitrary"),
        name="moe_combine",
    )(xt, route, pos_t, y, fg)


PAIR_BM = 128
PAIRS_PER_GROUP = EXPERTS_PER_GROUP * (EXPERTS_PER_GROUP - 1) // 2
N_CLASSES = N_GROUPS * PAIRS_PER_GROUP


def _route_class_kernel(x_ref, g_ref, w_ref, b_ref, r_ref):
    h = _rms_rows(x_ref[...], g_ref[...])
    logits = _dot_f32(h, w_ref[...]) + b_ref[...]
    lane = lax.broadcasted_iota(jnp.int32, logits.shape, 1)
    big = jnp.int32(1 << 20)
    neg = jnp.float32(-jnp.inf)

    def top(vals):
        m = jnp.max(vals, axis=-1, keepdims=True)
        idx = jnp.min(jnp.where(vals == m, lane, big), axis=-1, keepdims=True)
        return m, idx

    _, gidx = top(jnp.where(lane < N_GROUPS, logits, neg))
    lo0 = ROUTE_EXPERT_LANE + EXPERTS_PER_GROUP * gidx
    in_grp = jnp.abs(2 * (lane - lo0) - (EXPERTS_PER_GROUP - 1)) < EXPERTS_PER_GROUP
    el = jnp.where(in_grp, logits, neg)
    _, i1 = top(el)
    _, i2 = top(jnp.where(lane == i1, neg, el))
    a = jnp.minimum(i1, i2) - lo0
    b = jnp.maximum(i1, i2) - lo0
    pair = a * (2 * EXPERTS_PER_GROUP - 1 - a) // 2 + (b - a - 1)
    r_ref[...] = jnp.broadcast_to(gidx * PAIRS_PER_GROUP + pair, r_ref.shape)


def _pair_expert_kernel(lo_ref, hi_ref, nv_ref, idx_hbm, x_hbm, g_ref, wr_ref, br_ref, fg_ref,
                        w1a_ref, w3a_ref, w2a_ref, w1b_ref, w3b_ref, w2b_ref, o_hbm,
                        idx_smem, xbuf, obuf, sem_idx, sem_in, sem_out, *, bm, final):
    i = pl.program_id(0)
    n = nv_ref[i]

    @pl.when(i == 0)
    def _():
        xbuf[...] = jnp.zeros(xbuf.shape, F32)

    @pl.when(n > 0)
    def _():
        cp = pltpu.make_async_copy(idx_hbm.at[i], idx_smem, sem_idx)
        cp.start()
        cp.wait()

        def in_copy(r):
            return pltpu.make_async_copy(x_hbm.at[pl.ds(idx_smem[r], 1)], xbuf.at[pl.ds(r, 1)], sem_in)

        def out_copy(r):
            return pltpu.make_async_copy(obuf.at[pl.ds(r, 1)], o_hbm.at[pl.ds(idx_smem[r], 1)], sem_out)

        def loop(copy_of, wait):
            def body(r, c):
                cpy = copy_of(r)
                cpy.wait() if wait else cpy.start()
                return c
            lax.fori_loop(0, n, body, 0)

        loop(in_copy, False)
        loop(in_copy, True)
        x = xbuf[...]
        h = _rms_rows(x, g_ref[...])
        logits = _dot_f32(h, wr_ref[...]) + br_ref[...]
        lane = lax.broadcasted_iota(jnp.int32, logits.shape, 1)
        lo = lo_ref[i]
        hi = hi_ref[i]
        grp = lo // EXPERTS_PER_GROUP
        is_g = lane < N_GROUPS
        gmax = jnp.max(jnp.where(is_g, logits, -jnp.inf), axis=-1, keepdims=True)
        eg = jnp.exp(logits - gmax)
        g_w = (jnp.sum(jnp.where(lane == grp, eg, 0.0), axis=-1, keepdims=True)
               / jnp.sum(jnp.where(is_g, eg, 0.0), axis=-1, keepdims=True))
        e_lo = jnp.sum(jnp.where(lane == ROUTE_EXPERT_LANE + lo, logits, 0.0), axis=-1, keepdims=True)
        e_hi = jnp.sum(jnp.where(lane == ROUTE_EXPERT_LANE + hi, logits, 0.0), axis=-1, keepdims=True)
        m = jnp.maximum(e_lo, e_hi)
        t_lo = jnp.exp(e_lo - m)
        t_hi = jnp.exp(e_hi - m)
        inv = g_w / (t_lo + t_hi)
        hb = h.astype(BF16)

        def expert(w1, w3, w2):
            a = jnp.dot(hb, w1[0], preferred_element_type=F32)
            b = jnp.dot(hb, w3[0], preferred_element_type=F32)
            return jnp.dot((_silu(a) * b).astype(BF16), w2[0], preferred_element_type=F32)

        y = expert(w1a_ref, w3a_ref, w2a_ref) * (t_lo * inv) + expert(w1b_ref, w3b_ref, w2b_ref) * (t_hi * inv)
        out = x + y
        if final:
            out = _rms_rows(out, fg_ref[...])
        obuf[...] = out
        loop(out_copy, False)
        loop(out_copy, True)


def moe_pair_pallas(xt, g, w_group, b_group, w_expert, b_expert, w1, w3, w2, final_g=None):
    T, D = xt.shape
    bm = PAIR_BM
    tm = min(ROUTE_TM, T)
    wr = jnp.zeros((D, LANES), F32).at[:, :N_GROUPS].set(w_group.astype(F32))
    wr = wr.at[:, ROUTE_EXPERT_LANE:ROUTE_EXPERT_LANE + N_EXPERTS].set(w_expert.astype(F32))
    br = jnp.zeros((1, LANES), F32).at[0, :N_GROUPS].set(b_group.astype(F32))
    br = br.at[0, ROUTE_EXPERT_LANE:ROUTE_EXPERT_LANE + N_EXPERTS].set(b_expert.astype(F32))
    g2 = g.reshape(1, D).astype(F32)
    cls = pl.pallas_call(
        _route_class_kernel,
        grid=(T // tm,),
        in_specs=[pl.BlockSpec((tm, D), lambda i: (i, 0)),
                  pl.BlockSpec((1, D), lambda i: (0, 0)),
                  pl.BlockSpec((D, LANES), lambda i: (0, 0)),
                  pl.BlockSpec((1, LANES), lambda i: (0, 0))],
        out_specs=pl.BlockSpec((tm, LANES), lambda i: (i, 0)),
        out_shape=jax.ShapeDtypeStruct((T, LANES), jnp.int32),
        compiler_params=_cparams("parallel"),
        name="moe_route",
    )(xt, g2, wr, br)[:, 0]

    n_blk = T // bm + N_CLASSES
    order = jnp.argsort(cls).astype(jnp.int32)
    cls_s = cls[order]
    bounds = jnp.searchsorted(cls_s, jnp.arange(N_CLASSES + 1, dtype=jnp.int32), side='left').astype(jnp.int32)
    seg_start = bounds[:-1]
    counts = bounds[1:] - seg_start
    nblk_c = (counts + bm - 1) // bm
    blk_end = jnp.cumsum(nblk_c)
    blk_start = blk_end - nblk_c
    bidx = jnp.arange(n_blk, dtype=jnp.int32)
    blk_cls = jnp.minimum(jnp.searchsorted(blk_end, bidx, side='right'), N_CLASSES - 1).astype(jnp.int32)
    first = seg_start[blk_cls] + (bidx - blk_start[blk_cls]) * bm
    n_valid = jnp.clip(seg_start[blk_cls] + counts[blk_cls] - first, 0, bm)
    n_valid = jnp.where(bidx < blk_end[-1], n_valid, 0).astype(jnp.int32)
    src = jnp.clip(first[:, None] + jnp.arange(bm, dtype=jnp.int32)[None, :], 0, T - 1)
    row_tok = order[src]
    pa, pb = np.triu_indices(EXPERTS_PER_GROUP, k=1)
    grp_of = np.repeat(np.arange(N_GROUPS), PAIRS_PER_GROUP) * EXPERTS_PER_GROUP
    lo_tab = jnp.asarray((grp_of + np.tile(pa, N_GROUPS)).astype(np.int32))
    hi_tab = jnp.asarray((grp_of + np.tile(pb, N_GROUPS)).astype(np.int32))
    blk_lo = lo_tab[blk_cls]
    blk_hi = hi_tab[blk_cls]

    fg = (final_g if final_g is not None else jnp.ones((D,), F32)).reshape(1, D).astype(F32)
    w1b, w3b, w2b = w1.astype(BF16), w3.astype(BF16), w2.astype(BF16)
    const = lambda shape: pl.BlockSpec(shape, lambda i, lo, hi, nv: (0,) * len(shape))
    wspec = lambda shape, which: pl.BlockSpec(
        shape, (lambda i, lo, hi, nv: (lo[i], 0, 0)) if which == 0 else (lambda i, lo, hi, nv: (hi[i], 0, 0)))
    return pl.pallas_call(
        functools.partial(_pair_expert_kernel, bm=bm, final=final_g is not None),
        grid_spec=pltpu.PrefetchScalarGridSpec(
            num_scalar_prefetch=3,
            grid=(n_blk,),
            in_specs=[pl.BlockSpec(memory_space=pl.ANY),
                      pl.BlockSpec(memory_space=pl.ANY),
                      const((1, D)), const((D, LANES)), const((1, LANES)), const((1, D)),
                      wspec((1, D, D_EXPERT), 0), wspec((1, D, D_EXPERT), 0), wspec((1, D_EXPERT, D), 0),
                      wspec((1, D, D_EXPERT), 1), wspec((1, D, D_EXPERT), 1), wspec((1, D_EXPERT, D), 1)],
            out_specs=pl.BlockSpec(memory_space=pl.ANY),
            scratch_shapes=[pltpu.SMEM((bm,), jnp.int32), pltpu.VMEM((bm, D), F32), pltpu.VMEM((bm, D), F32),
                            pltpu.SemaphoreType.DMA(()), pltpu.SemaphoreType.DMA(()), pltpu.SemaphoreType.DMA(())]),
        out_shape=jax.ShapeDtypeStruct((T, D), F32),
        compiler_params=_cparams("arbitrary"),
        name="moe_pair_experts",
    )(blk_lo, blk_hi, n_valid, row_tok, xt, g2, wr, br, fg, w1b, w3b, w2b, w1b, w3b, w2b)


GDN_TB = 512
GDN_INV_PASSES = 1
GDN_UNROLL_B = 2


def _gdn_prep_kernel(prev_ref, cur_ref, next_ref, ba_ref, cw_ref, ones_ref, alog_ref, dt_ref, qkv_ref, bg_ref, *, nt):
    qkv = _silu(_conv4(prev_ref, cur_ref, next_ref, cw_ref[...], nt))
    ones_bd = ones_ref[...]
    w = GDN_QK_W
    q = qkv[:, 0:w]
    k = qkv[:, w:2 * w]
    qn = q * lax.rsqrt(_dot_split(q * q, ones_bd, n=2) + 1e-6) * GDN_DK ** -0.5
    kn = k * lax.rsqrt(_dot_split(k * k, ones_bd, n=2) + 1e-6)
    qkv_ref[0] = jnp.concatenate([qn, kn, qkv[:, 2 * w:]], axis=-1)
    ba = ba_ref[0]
    beta = _sigmoid(ba)
    log_a = -jnp.exp(alog_ref[...]) * _softplus(ba + dt_ref[...])
    lane = lax.broadcasted_iota(jnp.int32, ba.shape, 1)
    h = GDN_HEADS
    for d in range(2):
        b_d = pltpu.roll(beta, (LANES - d * h) % LANES, 1)
        a_d = pltpu.roll(log_a, (LANES - (2 * h + d * h) + h) % LANES, 1)
        bg_ref[d, 0] = jnp.where(lane < h, b_d, jnp.where(lane < 2 * h, a_d, 0.0))


def _gdn_chunk_kernel(qkv_ref, bg_ref, tri_ref, tribd_ref, o_ref, st_ref, *, nb):
    c = GDN_CHUNK
    H = GDN_HEADS
    n = H * c

    @pl.when(pl.program_id(1) == 0)
    def _():
        st_ref[...] = jnp.zeros(st_ref.shape, F32)

    tri_bf = tri_ref[0].astype(BF16)
    incl = tribd_ref[0]
    ri = lax.broadcasted_iota(jnp.int32, (n, n), 0)
    ci = lax.broadcasted_iota(jnp.int32, (n, n), 1)
    eye = jnp.where(ri == ci, 1.0, 0.0)
    strict = incl - eye
    bdmask = jnp.where(ri // c == ci // c, 1.0, 0.0)

    def stack(x):
        w = x.shape[1] // H
        return jnp.concatenate([x[:, hh * w:(hh + 1) * w] for hh in range(H)], axis=0)

    def col(x, lane0):
        return jnp.concatenate([x[:, lane0 + hh:lane0 + hh + 1] for hh in range(H)], axis=0)

    def bd(x_st):
        return jnp.concatenate([x_st] * H, axis=1) * bdmask

    def mm(a, b, passes, dot=_dot_nn):
        if passes == 1:
            return dot(a.astype(BF16), b.astype(BF16))
        return _dot_f32(a, b, dot=dot)

    def one_batch(b):
        qkv = qkv_ref[b]
        bg = bg_ref[0, b]
        q_st = stack(qkv[:, 0:GDN_QK_W])
        k_st = stack(qkv[:, GDN_QK_W:2 * GDN_QK_W])
        v_st = stack(qkv[:, 2 * GDN_QK_W:])
        acc = None
        for p in _split_bf16(bg, 3):
            t = _dot_nn(tri_bf, p)
            acc = t if acc is None else acc + t
        gam = acc
        tot = jnp.sum(bg, axis=0, keepdims=True)
        g_col = col(gam, H)
        b_col = col(bg, 0)
        gam_t = jnp.concatenate([gam, jnp.zeros_like(gam)], axis=0).T
        g_row = jnp.concatenate([gam_t[H + hh:H + hh + 1, 0:c] for hh in range(H)], axis=1)
        end_col = jnp.concatenate([jnp.broadcast_to(tot[:, H + hh:H + hh + 1], (c, 1)) for hh in range(H)], axis=0)
        exp_g = jnp.exp(g_col)
        decay = jnp.exp(jnp.minimum(g_col - g_row, 0.0)) * incl
        k_bf = k_st.astype(BF16)
        kk = _dot_nt(k_bf, k_bf)
        a_mat = strict * b_col * kk * decay
        p_inv = eye - a_mat
        x_pow = a_mat
        for _ in range(int(math.log2(c)) - 1):
            x_pow = mm(x_pow, x_pow, GDN_INV_PASSES)
            p_inv = p_inv + mm(p_inv, x_pow, GDN_INV_PASSES)
        rhs = jnp.concatenate([v_st * b_col, k_st * (b_col * exp_g)], axis=1)
        p_bf = p_inv.astype(BF16)
        sol = _dot_nn(p_bf, rhs.astype(BF16))
        resid = rhs - sol - _dot_f32(a_mat, sol)
        sol = sol + _dot_nn(p_bf, resid.astype(BF16))
        u_st = sol[:, 0:c]
        kc_bd = bd(sol[:, c:2 * c])
        qk = _dot_nt(q_st.astype(BF16), k_bf) * decay
        qd_bd = bd(q_st * exp_g)
        kd_bd = bd(k_st * jnp.exp(end_col - g_col))
        st = st_ref[b]
        st_bf = st.astype(BF16)
        v_new = u_st - _dot_nn(kc_bd.astype(BF16), st_bf)
        v_new_bf = v_new.astype(BF16)
        o_st = _dot_nn(qd_bd.astype(BF16), st_bf) + _dot_nn(qk.astype(BF16), v_new_bf)
        st_ref[b] = jnp.exp(end_col) * st + _dot_tn(kd_bd.astype(BF16), v_new_bf)
        o_ref[0, b] = jnp.concatenate([o_st[hh * c:(hh + 1) * c] for hh in range(H)], axis=1)

    def pair(i, carry):
        for j in range(GDN_UNROLL_B):
            one_batch(i * GDN_UNROLL_B + j)
        return carry

    lax.fori_loop(0, nb // GDN_UNROLL_B, pair, 0)


def gdn_pallas(nqkv, nba, conv_w, a_log, dt_bias):
    B, S, W = nqkv.shape
    tb = min(GDN_TB, S)
    nt = S // tb
    c = GDN_CHUNK
    nc = S // c
    H = GDN_HEADS
    r = np.arange(GDN_QK_W) // GDN_DK
    ones_bd = jnp.asarray((r[:, None] == r[None, :]).astype(np.float32)).astype(BF16)
    alog = jnp.zeros((1, LANES), F32).at[0, 2 * H:4 * H].set(a_log.astype(F32).reshape(-1))
    dt = jnp.zeros((1, LANES), F32).at[0, 2 * H:4 * H].set(dt_bias.astype(F32).reshape(-1))
    full = lambda shape: pl.BlockSpec(shape, lambda b, i: (0,) * len(shape))
    qkvn, bg = pl.pallas_call(
        functools.partial(_gdn_prep_kernel, nt=nt),
        grid=(B, nt),
        in_specs=_halo_specs(tb, W, nt) + [pl.BlockSpec((1, tb, LANES), lambda b, i: (b, i, 0)),
                                           full((CONV_WIDTH, W)), full((GDN_QK_W, GDN_QK_W)),
                                           full((1, LANES)), full((1, LANES))],
        out_specs=[pl.BlockSpec((1, tb, W), lambda b, i: (b, i, 0)),
                   pl.BlockSpec((2, 1, tb, LANES), lambda b, i: (0, b, i, 0))],
        out_shape=[jax.ShapeDtypeStruct((B, S, W), F32), jax.ShapeDtypeStruct((2, B, S, LANES), F32)],
        compiler_params=_cparams("parallel", "parallel"),
        name="gdn_prep",
    )(nqkv, nqkv, nqkv, nba, conv_w.astype(F32), ones_bd, alog, dt)

    t = np.arange(c)
    tri_np = np.stack([t[:, None] >= t[None, :], t[:, None] <= t[None, :]]).astype(np.float32)
    tri = jnp.asarray(tri_np)
    tri_bd = jnp.asarray(np.stack([np.kron(np.eye(H, dtype=np.float32), tri_np[d]) for d in range(2)]))

    def tmap(d, i):
        return i + d * (nc - 1 - 2 * i)

    return pl.pallas_call(
        functools.partial(_gdn_chunk_kernel, nb=B),
        grid=(2, nc),
        in_specs=[pl.BlockSpec((B, c, W), lambda d, i: (0, tmap(d, i), 0)),
                  pl.BlockSpec((1, B, c, LANES), lambda d, i: (d, 0, tmap(d, i), 0)),
                  pl.BlockSpec((1, c, c), lambda d, i: (d, 0, 0)),
                  pl.BlockSpec((1, H * c, H * c), lambda d, i: (d, 0, 0))],
        out_specs=pl.BlockSpec((1, B, c, GDN_V_W), lambda d, i: (d, 0, tmap(d, i), 0)),
        out_shape=jax.ShapeDtypeStruct((2, B, S, GDN_V_W), F32),
        scratch_shapes=[pltpu.VMEM((B, H * GDN_DK, GDN_DV), F32)],
        compiler_params=_cparams("arbitrary", "arbitrary"),
        name="gdn_chunk",
    )(qkvn, bg, tri, tri_bd)


def gdn_scan(q, k, v, beta, log_a):
    b_, s_, h_, dk = q.shape
    dv = v.shape[-1]
    c = GDN_CHUNK
    qc, kc, vc, bc, gc = (_to_chunks(t, c) for t in (q, k, v, beta, log_a))
    gam = jnp.cumsum(gc, axis=-1)
    diff = gam[..., :, None] - gam[..., None, :]
    incl = jnp.tril(jnp.ones((c, c), dtype=bool))
    strict = jnp.tril(jnp.ones((c, c), dtype=bool), k=-1)
    decay = jnp.exp(jnp.where(incl, diff, -jnp.inf))
    kk = jnp.einsum('bhntd,bhnsd->bhnts', kc, kc)
    a_mat = jnp.where(strict, bc[..., None] * kk * decay, 0.0) + jnp.eye(c, dtype=jnp.float32)
    rhs = jnp.concatenate([vc * bc[..., None], kc * (bc * jnp.exp(gam))[..., None]], axis=-1)
    sol = lax.linalg.triangular_solve(a_mat, rhs, left_side=True, lower=True, unit_diagonal=True)
    u_val, k_cum = sol[..., :dv], sol[..., dv:]
    qk = jnp.einsum('bhntd,bhnsd->bhnts', qc, kc) * decay
    q_dec = qc * jnp.exp(gam)[..., None]
    k_dec = kc * jnp.exp(gam[..., -1:] - gam)[..., None]
    c_dec = jnp.exp(gam[..., -1])

    def step(state, inp):
        u_i, kc_i, qk_i, qd_i, kd_i, cd_i = inp
        v_new = u_i - jnp.einsum('bhtd,bhde->bhte', kc_i, state)
        o = jnp.einsum('bhtd,bhde->bhte', qd_i, state) + jnp.einsum('bhts,bhse->bhte', qk_i, v_new)
        state = cd_i[..., None, None] * state + jnp.einsum('bhsd,bhse->bhde', kd_i, v_new)
        return state, o

    xs = tuple(jnp.moveaxis(t, 2, 0) for t in (u_val, k_cum, qk, q_dec, k_dec, c_dec))
    _, o = lax.scan(step, jnp.zeros((b_, h_, dk, dv), jnp.float32), xs)
    return _from_chunks(o)


def gdn_mixer(qkv, z, ba, conv_w, a_log, dt_bias, norm_g):
    B, S, _ = qkv.shape
    f32 = jnp.float32
    qkv = jax.nn.silu(dwconv_centred(qkv, conv_w)).astype(f32)
    q, k, v = jnp.split(qkv, [GDN_QK_W, 2 * GDN_QK_W], axis=-1)
    q = l2norm(q.reshape(B, S, GDN_HEADS, GDN_DK)) * GDN_DK ** -0.5
    k = l2norm(k.reshape(B, S, GDN_HEADS, GDN_DK))
    v = v.reshape(B, S, GDN_HEADS, GDN_DV)
    ba = ba.astype(f32).reshape(B, S, 4, GDN_HEADS)
    beta = jax.nn.sigmoid(ba[:, :, :2])
    log_a = -jnp.exp(a_log.astype(f32)) * jax.nn.softplus(ba[:, :, 2:] + dt_bias.astype(f32))
    o_f = gdn_scan(q, k, v, beta[:, :, 0], log_a[:, :, 0])
    o_b = _flip(gdn_scan(_flip(q), _flip(k), _flip(v), _flip(beta[:, :, 1]), _flip(log_a[:, :, 1])))
    o = rmsnorm(o_f + o_b, norm_g) * jax.nn.silu(z.astype(f32)).reshape(B, S, GDN_HEADS, GDN_DV)
    return o.reshape(B, S, GDN_V_W).astype(z.dtype)


def gated_merge(h, branches, merge_w, merge_b, branch_up):
    out = None
    for i, y in enumerate(branches):
        term = jax.nn.sigmoid(h @ merge_w[i] + merge_b[i]) * (y @ branch_up[i])
        out = term if out is None else out + term
    return out


def memory_cross_attention(h, mem_n, w_q, w_kv, w_o):
    B, S, D = h.shape
    M = mem_n.shape[1]
    q = (h @ w_q).reshape(B, S, XATTN_HEADS, XATTN_DH)
    kv = (mem_n @ w_kv).reshape(B, M, 2, XATTN_HEADS, XATTN_DH)
    k, v = kv[:, :, 0], kv[:, :, 1]
    s = jnp.einsum('bshd,bmhd->bhsm', q, k).astype(jnp.float32) * XATTN_DH ** -0.5
    p = jax.nn.softmax(s, axis=-1)
    o = jnp.einsum('bhsm,bmhd->bshd', p.astype(v.dtype), v).reshape(B, S, D)
    return o @ w_o


def hier_moe(h, w_group, b_group, w_expert, b_expert, w1, w3, w2):
    B, S, D = h.shape
    T = B * S
    f32 = jnp.float32
    ht = h.reshape(T, D)
    g_logits = (ht @ w_group).astype(f32) + b_group.astype(f32)
    g_prob = jax.nn.softmax(g_logits, axis=-1)
    _, g_idx = lax.top_k(g_logits, 1)
    g_w = jnp.take_along_axis(g_prob, g_idx, axis=1)
    e_logits = ((ht @ w_expert).astype(f32) + b_expert.astype(f32)).reshape(T, N_GROUPS, EXPERTS_PER_GROUP)
    e_logits = jnp.take_along_axis(
        e_logits, jnp.broadcast_to(g_idx[:, :, None], (T, 1, EXPERTS_PER_GROUP)), axis=1)[:, 0]
    e_top, e_idx = lax.top_k(e_logits, TOP_K)
    gate = jax.nn.softmax(e_top, axis=-1) * g_w
    eid = (g_idx * EXPERTS_PER_GROUP + e_idx).reshape(-1)
    tok = jnp.repeat(jnp.arange(T, dtype=jnp.int32), TOP_K)
    wts = gate.reshape(-1)
    A = T * TOP_K
    n_blk = -(-A // MOE_BLOCK) + N_EXPERTS
    P = n_blk * MOE_BLOCK
    order = jnp.argsort(eid)
    eid_s, tok_s, w_s = eid[order], tok[order], wts[order]
    counts = jnp.zeros((N_EXPERTS,), jnp.int32).at[eid].add(1)
    padded = (counts + MOE_BLOCK - 1) // MOE_BLOCK * MOE_BLOCK
    pad_end = jnp.cumsum(padded)
    pad_start = pad_end - padded
    seg_start = jnp.cumsum(counts) - counts
    dest = pad_start[eid_s] + jnp.arange(A, dtype=jnp.int32) - seg_start[eid_s]
    row_tok = jnp.full((P,), T, jnp.int32).at[dest].set(tok_s)
    row_w = jnp.zeros((P,), f32).at[dest].set(w_s)
    blk_expert = jnp.minimum(
        jnp.searchsorted(pad_end, jnp.arange(n_blk, dtype=jnp.int32) * MOE_BLOCK, side='right'),
        N_EXPERTS - 1)
    h_pad = jnp.concatenate([ht, jnp.zeros((1, D), ht.dtype)], axis=0)
    xb = h_pad[row_tok].reshape(n_blk, MOE_BLOCK, D)

    def expert_block(args):
        xi, e = args
        return (jax.nn.silu(xi @ w1[e]) * (xi @ w3[e])) @ w2[e]

    yb = lax.map(expert_block, (xb, blk_expert)).reshape(P, D)
    out = jnp.zeros((T + 1, D), h.dtype).at[row_tok].add(yb * row_w[:, None].astype(yb.dtype))
    return out[:T].reshape(B, S, D)


def kernel(x, mem, mix_norm, w_in, gla_lr_up, gla_lr_bias, gla_norm, lru_conv_w, lru_conv_b, lru_w_a, lru_b_a, lru_w_x, lru_b_x, lru_lambda, diff_lq1, diff_lk1, diff_lq2, diff_lk2, diff_norm, gdn_conv_w, gdn_a_log, gdn_dt_bias, gdn_norm, merge_w, merge_b, branch_up, mix_out, xattn_norm, mem_norm, xattn_wq, xattn_wkv, xattn_wo, moe_norm, moe_w_group, moe_b_group, moe_w_expert, moe_b_expert, moe_w1, moe_w3, moe_w2, final_norm):
    B, S, D = x.shape
    T = B * S
    xt = x.reshape(T, D)
    for l in range(DEPTH):
        gla_in, gog, lx, lg, dq, dk, dv, nqkv, nz, nba = inproj_pallas(xt, mix_norm[l], w_in[l])
        gla_o = gla_pallas(gla_in.reshape(B, S, -1), gla_lr_up[l], gla_lr_bias[l])
        lru_h = lru_pallas(lx.reshape(B, S, -1), lru_conv_w[l], lru_conv_b[l], lru_w_a[l], lru_b_a[l],
                           lru_w_x[l], lru_b_x[l], lru_lambda[l])
        y_diff = diff_attention_pallas(dq.reshape(B, S, -1), dk.reshape(B, S, -1), dv.reshape(B, S, -1),
                                       diff_lq1[l], diff_lk1[l], diff_lq2[l], diff_lk2[l], diff_norm[l],
                                       0.8 - 0.6 * math.exp(-0.3 * l))
        gdn_o = gdn_pallas(nqkv.reshape(B, S, -1), nba.reshape(B, S, -1), gdn_conv_w[l], gdn_a_log[l],
                           gdn_dt_bias[l])
        xt = merge_pallas(xt, gla_o.reshape(2, T, -1), gog, lru_h.reshape(2, T, -1), lg, y_diff.reshape(T, -1),
                          gdn_o.reshape(2, T, -1), nz, mix_norm[l], gla_norm[l], gdn_norm[l],
                          merge_w[l], merge_b[l], branch_up[l], mix_out[l])
        xt = xattn_pallas(xt.reshape(B, S, D), mem, xattn_norm[l], mem_norm[l], xattn_wq[l], xattn_wkv[l],
                          xattn_wo[l]).reshape(T, D)
        xt = moe_pair_pallas(xt, moe_norm[l], moe_w_group[l], moe_b_group[l], moe_w_expert[l], moe_b_expert[l],
                             moe_w1[l], moe_w3[l], moe_w2[l], final_g=final_norm if l == DEPTH - 1 else None)
    return xt.reshape(B, S, D)
```

```python
import functools
import math

import jax
import jax.numpy as jnp
import numpy as np
from jax import lax
from jax.experimental import pallas as pl
from jax.experimental.pallas import tpu as pltpu

D_MODEL = 1024
DEPTH = 2
N_BRANCHES = 4
BRANCH_WIDTH = D_MODEL // 4
RMS_EPS = 1e-6
CONV_WIDTH = 4

GLA_HEADS = 4
GLA_DV = BRANCH_WIDTH // GLA_HEADS
GLA_DK = GLA_DV // 2
GLA_RANK = 16
GLA_GATE_NORM = 16.0
GLA_CHUNK = 64
GLA_QK_W = GLA_HEADS * GLA_DK
GLA_V_W = GLA_HEADS * GLA_DV

LRU_WIDTH = BRANCH_WIDTH
LRU_BLOCKS = 4
LRU_C = 8.0

DIFF_HEADS = 4
DIFF_DV = BRANCH_WIDTH // DIFF_HEADS
DIFF_DH = DIFF_DV // 2
DIFF_QK_W = DIFF_HEADS * 2 * DIFF_DH
DIFF_V_W = DIFF_HEADS * DIFF_DV

GDN_HEADS = 4
GDN_DK = BRANCH_WIDTH // GDN_HEADS
GDN_DV = BRANCH_WIDTH // GDN_HEADS
GDN_CHUNK = 64
GDN_QK_W = GDN_HEADS * GDN_DK
GDN_V_W = GDN_HEADS * GDN_DV

IN_SPLITS = (GLA_QK_W, GLA_QK_W, GLA_V_W, GLA_V_W, 2 * GLA_RANK,
             LRU_WIDTH, LRU_WIDTH,
             DIFF_QK_W, DIFF_QK_W, DIFF_V_W,
             2 * GDN_QK_W + GDN_V_W, GDN_V_W, 4 * GDN_HEADS)

XATTN_HEADS = 4
XATTN_DH = D_MODEL // XATTN_HEADS

N_GROUPS = 4
EXPERTS_PER_GROUP = 8
N_EXPERTS = N_GROUPS * EXPERTS_PER_GROUP
TOP_K = 2
D_EXPERT = D_MODEL // 2
MOE_BLOCK = 128

LANES = 128
VMEM_LIMIT = 56 * 1024 * 1024

F32 = jnp.float32
BF16 = jnp.bfloat16
NEG_BIG = -1e30


def rmsnorm(x, g):
    xf = x.astype(jnp.float32)
    y = xf * lax.rsqrt(jnp.mean(xf * xf, axis=-1, keepdims=True) + RMS_EPS)
    return (y * g.astype(jnp.float32)).astype(x.dtype)


def l2norm(t):
    return t * lax.rsqrt(jnp.sum(t * t, axis=-1, keepdims=True) + 1e-6)


def _flip(t):
    return jnp.flip(t, axis=1)


def dwconv_centred(x, w):
    k = w.shape[0]
    return lax.conv_general_dilated(
        x, w[:, None, :].astype(x.dtype), window_strides=(1,),
        padding=[(k // 2, k - 1 - k // 2)],
        dimension_numbers=('NWC', 'WIO', 'NWC'),
        feature_group_count=x.shape[-1])


def _to_chunks(t, chunk):
    b, s, h = t.shape[:3]
    t = t.reshape((b, s // chunk, chunk, h) + t.shape[3:])
    return jnp.moveaxis(t, 3, 1)


def _from_chunks(o):
    nc, b, h, c, d = o.shape
    return o.transpose(1, 0, 3, 2, 4).reshape(b, nc * c, h, d)


def _linrec_combine(c1, c2):
    a1, b1 = c1
    a2, b2 = c2
    return a1 * a2, a2 * b1 + b2


def gla_scan(q, k, v, log_f):
    b_, s_, h_, dk = q.shape
    dv = v.shape[-1]
    c = GLA_CHUNK
    incl = jnp.tril(jnp.ones((c, c), dtype=bool))[:, :, None]

    def step(state, inp):
        q_i, k_i, v_i, g_i = inp
        bcum = jnp.cumsum(g_i, axis=2)
        decay = jnp.exp(jnp.where(incl, bcum[:, :, :, None, :] - bcum[:, :, None, :, :], -jnp.inf))
        scores = jnp.einsum('bhtd,bhsd,bhtsd->bhts', q_i, k_i, decay)
        o = (jnp.einsum('bhts,bhse->bhte', scores, v_i)
             + jnp.einsum('bhtd,bhde->bhte', q_i * jnp.exp(bcum), state))
        b_end = bcum[:, :, -1:, :]
        state = (jnp.exp(b_end)[:, :, 0, :, None] * state
                 + jnp.einsum('bhsd,bhse->bhde', k_i * jnp.exp(b_end - bcum), v_i))
        return state, o

    xs = tuple(jnp.moveaxis(_to_chunks(t, c), 2, 0) for t in (q, k, v, log_f))
    _, o = lax.scan(step, jnp.zeros((b_, h_, dk, dv), jnp.float32), xs)
    return _from_chunks(o)


def gla_mixer(q, k, v, og, lr, lr_up, lr_bias, norm_g):
    B, S, _ = q.shape
    f32 = jnp.float32
    qh = q.astype(f32).reshape(B, S, GLA_HEADS, GLA_DK) * GLA_DK ** -0.5
    kh = k.astype(f32).reshape(B, S, GLA_HEADS, GLA_DK)
    vh = v.astype(f32).reshape(B, S, GLA_HEADS, GLA_DV)
    lr = lr.astype(f32).reshape(B, S, 2, GLA_RANK)
    z = jnp.einsum('bsdr,drk->bsdk', lr, lr_up.astype(f32)) + lr_bias.astype(f32)
    log_f = (jax.nn.log_sigmoid(z) / GLA_GATE_NORM).reshape(B, S, 2, GLA_HEADS, GLA_DK)
    o_f = gla_scan(qh, kh, vh, log_f[:, :, 0])
    o_b = _flip(gla_scan(_flip(qh), _flip(kh), _flip(vh), _flip(log_f[:, :, 1])))
    o = rmsnorm(o_f + o_b, norm_g) * jax.nn.silu(og.astype(f32)).reshape(B, S, GLA_HEADS, GLA_DV)
    return o.reshape(B, S, GLA_V_W).astype(q.dtype)


def rglru_mixer(xb, gb, conv_w, conv_b, w_a, b_a, w_x, b_x, lam):
    B, S, W = xb.shape
    f32 = jnp.float32
    u = (dwconv_centred(xb, conv_w) + conv_b).astype(f32)
    ub = u.reshape(B, S, LRU_BLOCKS, W // LRU_BLOCKS)

    def gate(w, b):
        return jax.nn.sigmoid(jnp.einsum('bsni,nio->bsno', ub, w.astype(f32)).reshape(B, S, W) + b.astype(f32))

    def direction(d, reverse):
        r = gate(w_a[d], b_a[d])
        i = gate(w_x[d], b_x[d])
        log_a = -LRU_C * r * jax.nn.softplus(-lam[d].astype(f32))
        a = jnp.exp(log_a)
        xin = jnp.sqrt(-jnp.expm1(2.0 * log_a)) * (i * u)
        _, hs = lax.associative_scan(_linrec_combine, (a, xin), axis=1, reverse=reverse)
        return hs

    hsum = direction(0, False) + direction(1, True)
    return (hsum * jax.nn.gelu(gb.astype(f32))).astype(xb.dtype)


def _cparams(*sem):
    return pltpu.CompilerParams(dimension_semantics=sem, vmem_limit_bytes=VMEM_LIMIT)


def _split_bf16(a, n):
    parts, r = [], a
    for i in range(n):
        p = r.astype(BF16)
        parts.append(p)
        if i + 1 < n:
            r = r - p.astype(F32)
    return parts


def _split_trunc_bf16(a, n):
    parts, r = [], a
    for i in range(n):
        bits = lax.bitcast_convert_type(r, jnp.uint32) & jnp.uint32(0xFFFF0000)
        p = lax.bitcast_convert_type(bits, F32)
        parts.append(p.astype(BF16))
        if i + 1 < n:
            r = r - p
    return parts


def _dot_nn(a, b):
    return jnp.dot(a, b, preferred_element_type=F32)


def _dot_nt(a, b):
    return lax.dot_general(a, b, (((1,), (1,)), ((), ())), preferred_element_type=F32)


def _dot_tn(a, b):
    return lax.dot_general(a, b, (((0,), (0,)), ((), ())), preferred_element_type=F32)


def _dot_split(a, b_exact, n=3, dot=_dot_nn):
    acc = None
    for p in _split_bf16(a, n):
        t = dot(p, b_exact)
        acc = t if acc is None else acc + t
    return acc


def _dot_f32(a, b, dot=_dot_nn):
    a_hi, a_lo = _split_bf16(a, 2)
    b_hi, b_lo = _split_bf16(b, 2)
    return dot(a_hi, b_hi) + (dot(a_hi, b_lo) + dot(a_lo, b_hi))


def _rms_rows(x, g):
    return x * lax.rsqrt(jnp.mean(x * x, axis=-1, keepdims=True) + RMS_EPS) * g


def _sigmoid(x):
    return 1.0 / (1.0 + jnp.exp(-x))


def _softplus(x):
    return jnp.maximum(x, 0.0) + jnp.log(1.0 + jnp.exp(-jnp.abs(x)))


def _silu(x):
    return x * _sigmoid(x)


_G_GLA = (0, 640)
_G_GOG = (640, 896)
_G_LX = (896, 1152)
_G_LG = (1152, 1408)
_G_DQ = (1408, 1664)
_G_DK = (1664, 1920)
_G_DV = (1920, 2176)
_G_NQKV = (2176, 2944)
_G_NZ = (2944, 3200)
_G_NBA = (3200, 3328)
IN_PAD_W = 3328
IN_TM = 512


def _permute_w_in(w):
    z = lambda n: jnp.zeros((w.shape[0], n), w.dtype)
    return jnp.concatenate([w[:, 0:512], w[:, 768:800], z(96), w[:, 512:768], w[:, 800:3120], z(112)],
                           axis=1).astype(BF16)


def _inproj_kernel(x_ref, g_ref, w_ref, gla_ref, gog_ref, lx_ref, lg_ref, dq_ref, dk_ref, dv_ref,
                   nqkv_ref, nz_ref, nba_ref):
    h = _rms_rows(x_ref[...], g_ref[...]).astype(BF16)

    def proj(grp):
        return jnp.dot(h, w_ref[:, grp[0]:grp[1]], preferred_element_type=F32)

    gla_ref[...] = proj(_G_GLA)
    gog_ref[...] = proj(_G_GOG).astype(BF16)
    lx_ref[...] = proj(_G_LX)
    lg_ref[...] = proj(_G_LG).astype(BF16)
    dq_ref[...] = (proj(_G_DQ) * (DIFF_DH ** -0.5 * LOG2E)).astype(BF16)
    dk_ref[...] = proj(_G_DK).astype(BF16)
    dv_ref[...] = proj(_G_DV).astype(BF16)
    nqkv_ref[...] = proj(_G_NQKV)
    nz_ref[...] = proj(_G_NZ).astype(BF16)
    nba_ref[...] = proj(_G_NBA)


def inproj_pallas(xt, g, w_in):
    T, D = xt.shape
    tm = min(IN_TM, T)
    groups = [(_G_GLA, F32), (_G_GOG, BF16), (_G_LX, F32), (_G_LG, BF16), (_G_DQ, BF16), (_G_DK, BF16),
              (_G_DV, BF16), (_G_NQKV, F32), (_G_NZ, BF16), (_G_NBA, F32)]
    return pl.pallas_call(
        _inproj_kernel,
        grid=(T // tm,),
        in_specs=[pl.BlockSpec((tm, D), lambda i: (i, 0)),
                  pl.BlockSpec((1, D), lambda i: (0, 0)),
                  pl.BlockSpec((D, IN_PAD_W), lambda i: (0, 0))],
        out_specs=[pl.BlockSpec((tm, b - a), lambda i: (i, 0)) for (a, b), _ in groups],
        out_shape=[jax.ShapeDtypeStruct((T, b - a), dt) for (a, b), dt in groups],
        compiler_params=_cparams("parallel"),
        name="inproj",
    )(xt, g.reshape(1, D).astype(F32), _permute_w_in(w_in))


def _gla_kernel(x_ref, tri_ref, w_ref, b_ref, o_ref, st_ref, *, nb):
    c = GLA_CHUNK

    @pl.when(pl.program_id(1) == 0)
    def _():
        st_ref[...] = jnp.zeros(st_ref.shape, F32)

    tri = tri_ref[0]
    tri_bf = tri.astype(BF16)
    tri4 = jnp.concatenate([tri] * GLA_HEADS, axis=0)
    w = w_ref[0]
    bias = b_ref[0]
    lane_qk = lax.broadcasted_iota(jnp.int32, (c, GLA_QK_W), 1) // GLA_DK
    lane_v = lax.broadcasted_iota(jnp.int32, (c, GLA_V_W), 1) // GLA_DV
    row_s = lax.broadcasted_iota(jnp.int32, (GLA_V_W, GLA_QK_W), 0) // GLA_DV
    col_s = lax.broadcasted_iota(jnp.int32, (GLA_V_W, GLA_QK_W), 1) // GLA_DK
    st_mask = row_s == col_s

    def one_batch(b):
        blk = x_ref[b]
        q = blk[:, 0:128] * GLA_DK ** -0.5
        k = blk[:, 128:256]
        v = blk[:, 256:512].astype(BF16)
        lr = blk[:, 512:640]
        z = _dot_f32(lr, w) + bias
        yield
        g = (jnp.minimum(z, 0.0) - jnp.log(1.0 + jnp.exp(-jnp.abs(z)))) * (1.0 / GLA_GATE_NORM)
        acc = None
        for p in _split_bf16(g, 3):
            t = _dot_nn(tri_bf, p)
            acc = t if acc is None else acc + t
        yield
        bc = acc
        tot = jnp.sum(g, axis=0, keepdims=True)
        ref = 0.5 * tot
        qt = q * jnp.exp(bc - ref)
        kt = (k * jnp.exp(ref - bc)).astype(BF16)
        qd = (q * jnp.exp(bc)).astype(BF16)
        kd = (k * jnp.exp(tot - bc)).astype(BF16)
        qstack = jnp.concatenate(
            [jnp.where(lane_qk == hh, qt, 0.0) for hh in range(GLA_HEADS)], axis=0).astype(BF16)
        s = _dot_nt(qstack, kt)
        st = st_ref[b]
        o = _dot_nt(qd, st.astype(BF16))
        upd = _dot_tn(v, kd)
        yield
        s = (s * tri4).astype(BF16)
        ohs = [_dot_nn(s[hh * c:(hh + 1) * c], v) for hh in range(GLA_HEADS)]
        yield
        for hh in range(GLA_HEADS):
            o = o + jnp.where(lane_v == hh, ohs[hh], 0.0)
        o_ref[0, b] = o
        st_ref[b] = st * jnp.exp(tot) + jnp.where(st_mask, upd, 0.0)

    chains = [one_batch(b) for b in range(nb)]
    live = True
    while live:
        live = False
        for ch in chains:
            try:
                next(ch)
                live = True
            except StopIteration:
                pass


def gla_pallas(gla_in, lr_up, lr_bias):
    B, S, W = gla_in.shape
    c = GLA_CHUNK
    nc = S // c
    r = np.arange(c)
    tri = jnp.asarray(np.stack([r[:, None] >= r[None, :], r[:, None] <= r[None, :]]).astype(np.float32))
    w = jnp.zeros((2, LANES, GLA_QK_W), F32)
    w = w.at[0, 0:GLA_RANK].set(lr_up[0].astype(F32)).at[1, GLA_RANK:2 * GLA_RANK].set(lr_up[1].astype(F32))
    bias = lr_bias.astype(F32).reshape(2, 1, GLA_QK_W)

    def tmap(d, i):
        return (0, i + d * (nc - 1 - 2 * i), 0)

    return pl.pallas_call(
        functools.partial(_gla_kernel, nb=B),
        grid=(2, nc),
        in_specs=[pl.BlockSpec((B, c, W), tmap),
                  pl.BlockSpec((1, c, c), lambda d, i: (d, 0, 0)),
                  pl.BlockSpec((1, LANES, GLA_QK_W), lambda d, i: (d, 0, 0)),
                  pl.BlockSpec((1, 1, GLA_QK_W), lambda d, i: (d, 0, 0))],
        out_specs=pl.BlockSpec((1, B, c, GLA_V_W), lambda d, i: (d,) + tmap(d, i)),
        out_shape=jax.ShapeDtypeStruct((2, B, S, GLA_V_W), F32),
        scratch_shapes=[pltpu.VMEM((B, GLA_V_W, GLA_QK_W), F32)],
        compiler_params=_cparams("arbitrary", "arbitrary"),
        name="gla",
    )(gla_in, tri, w, bias)


LRU_TB = 512
HALO = 8


def _halo_specs(tb, width, nt):
    r = tb // HALO
    return [pl.BlockSpec((1, HALO, width), lambda b, i: (b, jnp.maximum(i * r - 1, 0), 0)),
            pl.BlockSpec((1, tb, width), lambda b, i: (b, i, 0)),
            pl.BlockSpec((1, HALO, width), lambda b, i: (b, jnp.minimum((i + 1) * r, nt * r - 1), 0))]


def _conv4(prev_ref, cur_ref, next_ref, w, nt):
    i = pl.program_id(1)
    cur = cur_ref[0]
    tb = cur.shape[0]
    prev = prev_ref[0] * jnp.where(i > 0, 1.0, 0.0)
    nxt = next_ref[0] * jnp.where(i < nt - 1, 1.0, 0.0)
    ext = jnp.concatenate([prev, cur, nxt], axis=0)
    out = None
    for j in range(CONV_WIDTH):
        off = HALO + j - CONV_WIDTH // 2
        t = ext[off:off + tb] * w[j:j + 1]
        out = t if out is None else out + t
    return out


def _lru_prep_kernel(prev_ref, cur_ref, next_ref, cw_ref, cb_ref, wg_ref, bg_ref, lam_ref, a_ref, x_ref, *, nt):
    u = _conv4(prev_ref, cur_ref, next_ref, cw_ref[...], nt) + cb_ref[...]
    gates = _sigmoid(jnp.dot(u.astype(BF16), wg_ref[...], preferred_element_type=F32) + bg_ref[...])
    w = LRU_WIDTH
    for d in range(2):
        r = gates[:, (2 * d) * w:(2 * d + 1) * w]
        ig = gates[:, (2 * d + 1) * w:(2 * d + 2) * w]
        log_a = -LRU_C * r * _softplus(-lam_ref[d:d + 1])
        a_ref[d, 0] = jnp.exp(log_a)
        x_ref[d, 0] = jnp.sqrt(1.0 - jnp.exp(2.0 * log_a)) * (ig * u)


def _lru_scan_kernel(a_ref, x_ref, h_ref, st_ref, *, tb):
    d = pl.program_id(0)

    @pl.when(pl.program_id(1) == 0)
    def _():
        st_ref[...] = jnp.zeros(st_ref.shape, F32)

    def body(t, h):
        tt = t + d * (tb - 1 - 2 * t)
        h = a_ref[0, :, pl.ds(tt, 1), :] * h + x_ref[0, :, pl.ds(tt, 1), :]
        h_ref[0, :, pl.ds(tt, 1), :] = h
        return h

    st_ref[...] = lax.fori_loop(0, tb, body, st_ref[...], unroll=8)


def lru_pallas(lx, conv_w, conv_b, w_a, b_a, w_x, b_x, lam):
    B, S, W = lx.shape
    tb = min(LRU_TB, S)
    nt = S // tb
    blk = W // LRU_BLOCKS

    def dense(wb):
        m = jnp.zeros((W, W), F32)
        for n in range(LRU_BLOCKS):
            m = m.at[n * blk:(n + 1) * blk, n * blk:(n + 1) * blk].set(wb[n].astype(F32))
        return m

    wg = jnp.concatenate([dense(w_a[0]), dense(w_x[0]), dense(w_a[1]), dense(w_x[1])], axis=1).astype(BF16)
    bg = jnp.concatenate([b_a[0], b_x[0], b_a[1], b_x[1]]).astype(F32).reshape(1, 4 * W)
    full = lambda shape: pl.BlockSpec(shape, lambda b, i: (0,) * len(shape))
    a, xin = pl.pallas_call(
        functools.partial(_lru_prep_kernel, nt=nt),
        grid=(B, nt),
        in_specs=_halo_specs(tb, W, nt) + [full((CONV_WIDTH, W)), full((1, W)), full((W, 4 * W)), full((1, 4 * W)),
                                           full((2, W))],
        out_specs=[pl.BlockSpec((2, 1, tb, W), lambda b, i: (0, b, i, 0))] * 2,
        out_shape=[jax.ShapeDtypeStruct((2, B, S, W), F32)] * 2,
        compiler_params=_cparams("parallel", "parallel"),
        name="lru_prep",
    )(lx, lx, lx, conv_w.astype(F32), conv_b.astype(F32).reshape(1, W), wg, bg, lam.astype(F32))

    def tmap(d, i):
        return (d, 0, i + d * (nt - 1 - 2 * i), 0)

    return pl.pallas_call(
        functools.partial(_lru_scan_kernel, tb=tb),
        grid=(2, nt),
        in_specs=[pl.BlockSpec((1, B, tb, W), tmap)] * 2,
        out_specs=pl.BlockSpec((1, B, tb, W), tmap),
        out_shape=jax.ShapeDtypeStruct((2, B, S, W), F32),
        scratch_shapes=[pltpu.VMEM((B, 1, W), F32)],
        compiler_params=_cparams("arbitrary", "arbitrary"),
        name="lru_scan",
    )(a, xin)


DIFF_TQ = 256
DIFF_TK = 512
DIFF_POS_LANE = 2 * DIFF_DH
DIFF_POS_PIECES = 3
DIFF_UNROLL = 4
DIFF_VT_ROWS = DIFF_DV + 8
LOG2E = 1.4426950408889634


def _diff_attn_kernel(slopes_ref, lam_ref, q_ref, k_ref, vt_ref, g_ref, o_ref,
                      qv_scr, rt_scr, s_scr, mb_scr, p_scr, al_scr, m_scr, acc_scr, *, tq, tk, seq, out_scale):
    h = pl.program_id(1)
    qi = pl.program_id(2)
    slope = slopes_ref[h]
    lam = lam_ref[0]
    nk = seq // tk

    q = q_ref[0, 0]
    lane = lax.broadcasted_iota(jnp.int32, q.shape, 1)
    zero = jnp.zeros_like(q)
    q1 = jnp.where(jnp.abs(2 * lane - (DIFF_DH + DIFF_POS_LANE - 1)) < DIFF_DH, zero, q)
    q2 = jnp.where(lane < DIFF_DH, zero, q)
    qs = jnp.concatenate([q1, q2], axis=0)
    lane2 = lax.broadcasted_iota(jnp.int32, qs.shape, 1)
    col = lax.broadcasted_iota(jnp.int32, (1, 2 * tq), 1)
    qpos = qi * tq + jnp.where(col >= tq, col - tq, col)
    shift_q = slope * qpos.astype(F32)
    qv_scr[0] = qs
    qv_scr[1] = jnp.where(lane2 >= DIFF_POS_LANE, -qs, qs)
    rt_scr[0] = -shift_q
    rt_scr[1] = shift_q

    kd = (qi * tq) // tk
    k0 = pl.multiple_of(kd * tk, tk)
    q_mid = jnp.where(lane2 >= DIFF_POS_LANE, jnp.zeros_like(qs), qs)
    s = _dot_nt(k_ref[0, 0, pl.ds(k0, tk), :], q_mid)
    kpos = k0 + lax.broadcasted_iota(jnp.int32, (tk, 2 * tq), 0)
    s = s - slope * jnp.abs(qpos - kpos).astype(F32)
    m0 = jnp.max(s, axis=0, keepdims=True)
    p = jnp.exp2(s - m0).astype(BF16)
    acc_scr[...] = jnp.dot(vt_ref[0, 0, :, pl.ds(k0, tk)], p, preferred_element_type=F32)
    m_scr[...] = m0

    def tile_of(t):
        return t + jnp.where(t >= kd, 1, 0)

    def scores(t, slot):
        kj = tile_of(t)
        var = jnp.where(kj > kd, 1, 0)
        kt = k_ref[0, 0, pl.ds(pl.multiple_of(kj * tk, tk), tk), :]
        sc = _dot_nt(kt, qv_scr[var])
        s_scr[slot] = sc
        mb_scr[slot] = jnp.max(sc, axis=0, keepdims=True) + rt_scr[var]

    def softmax(t, slot):
        kj = tile_of(t)
        var = jnp.where(kj > kd, 1, 0)
        m_old = m_scr[...]
        m_new = jnp.maximum(m_old, mb_scr[slot])
        al_scr[slot] = jnp.exp2(m_old - m_new)
        p_scr[slot] = jnp.exp2(s_scr[slot] - (m_new - rt_scr[var])).astype(BF16)
        m_scr[...] = m_new

    def pv(t, slot):
        vt = vt_ref[0, 0, :, pl.ds(pl.multiple_of(tile_of(t) * tk, tk), tk)]
        acc_scr[...] = al_scr[slot] * acc_scr[...] + jnp.dot(vt, p_scr[slot], preferred_element_type=F32)

    n_off = nk - 1
    if n_off == 1:
        scores(0, 0)
        softmax(0, 0)
        pv(0, 0)
    elif n_off >= 2:
        scores(0, 0)
        softmax(0, 0)
        scores(1, 1)

        def trip(t, slot):
            pv(t, slot)
            scores(t + 2, slot)
            softmax(t + 1, 1 - slot)

        def body(i, c):
            for u in range(DIFF_UNROLL):
                trip(DIFF_UNROLL * i + u, u % 2)
            return c

        n_trips = n_off - 2
        n_loop = n_trips // DIFF_UNROLL
        lax.fori_loop(0, n_loop, body, 0)
        for t in range(n_loop * DIFF_UNROLL, n_trips):
            trip(t, t % 2)
        last = n_off - 1
        pv(last - 1, (last - 1) % 2)
        softmax(last, last % 2)
        pv(last, last % 2)

    acc = acc_scr[...]
    o2 = acc[0:DIFF_DV] * (1.0 / acc[DIFF_DV:DIFF_DV + 1])
    o = o2[:, :tq] - lam * o2[:, tq:]
    ms = jnp.mean(o * o, axis=0, keepdims=True)
    o = o * lax.rsqrt(ms + RMS_EPS) * g_ref[...] * out_scale
    o_ref[0, 0] = o.astype(o_ref.dtype)


def diff_attention_pallas(dq, dk, dv, lq1, lk1, lq2, lk2, norm_g, lambda_init):
    B, S, _ = dq.shape
    H = DIFF_HEADS
    tq = min(DIFF_TQ, S)
    tk = min(DIFF_TK, S)
    slopes = jnp.exp2(-8.0 * jnp.arange(1, H + 1, dtype=F32) / H) * LOG2E
    lam = (jnp.exp(jnp.sum(lq1.astype(F32) * lk1.astype(F32)))
           - jnp.exp(jnp.sum(lq2.astype(F32) * lk2.astype(F32))) + lambda_init).reshape(1)
    npos = DIFF_POS_PIECES
    pad = LANES - 2 * DIFF_DH - npos
    q4 = dq.astype(BF16).reshape(B, S, H, 2 * DIFF_DH).transpose(0, 2, 1, 3)
    q_aug = jnp.concatenate([q4, jnp.ones((B, H, S, npos), BF16), jnp.zeros((B, H, S, pad), BF16)], axis=-1)
    k4 = dk.astype(BF16).reshape(B, S, H, 2 * DIFF_DH).transpose(0, 2, 1, 3)
    kbias = slopes[:, None] * jnp.arange(S, dtype=F32)[None, :]
    kpos = jnp.stack(_split_trunc_bf16(kbias, npos), axis=-1)[None]
    k_aug = jnp.concatenate([k4, jnp.broadcast_to(kpos, (B, H, S, npos)), jnp.zeros((B, H, S, pad), BF16)],
                            axis=-1)
    v4 = dv.astype(BF16).reshape(B, S, H, DIFF_DV).transpose(0, 2, 3, 1)
    vt = jnp.concatenate([v4, jnp.ones((B, H, 1, S), BF16),
                          jnp.zeros((B, H, DIFF_VT_ROWS - DIFF_DV - 1, S), BF16)], axis=2)
    g = norm_g.astype(F32).reshape(DIFF_DV, 1)

    kern = functools.partial(_diff_attn_kernel, tq=tq, tk=tk, seq=S, out_scale=1.0 - lambda_init)
    out = pl.pallas_call(
        kern,
        grid=(B, H, S // tq),
        in_specs=[
            pl.BlockSpec(memory_space=pltpu.SMEM),
            pl.BlockSpec(memory_space=pltpu.SMEM),
            pl.BlockSpec((1, 1, tq, LANES), lambda b, h, i: (b, h, i, 0)),
            pl.BlockSpec((1, 1, S, LANES), lambda b, h, i: (b, h, 0, 0)),
            pl.BlockSpec((1, 1, DIFF_VT_ROWS, S), lambda b, h, i: (b, h, 0, 0)),
            pl.BlockSpec((DIFF_DV, 1), lambda b, h, i: (0, 0)),
        ],
        out_specs=pl.BlockSpec((1, 1, DIFF_DV, tq), lambda b, h, i: (b, h, 0, i)),
        out_shape=jax.ShapeDtypeStruct((B, H, DIFF_DV, S), F32),
        scratch_shapes=[
            pltpu.VMEM((2, 2 * tq, LANES), BF16),
            pltpu.VMEM((2, 1, 2 * tq), F32),
            pltpu.VMEM((2, tk, 2 * tq), F32),
            pltpu.VMEM((2, 1, 2 * tq), F32),
            pltpu.VMEM((2, tk, 2 * tq), BF16),
            pltpu.VMEM((2, 1, 2 * tq), F32),
            pltpu.VMEM((1, 2 * tq), F32),
            pltpu.VMEM((DIFF_VT_ROWS, 2 * tq), F32),
        ],
        compiler_params=pltpu.CompilerParams(
            dimension_semantics=("parallel", "parallel", "arbitrary"),
            vmem_limit_bytes=VMEM_LIMIT),
        name="diff_attn",
    )(slopes, lam, q_aug, k_aug, vt, g)
    return out.transpose(0, 3, 1, 2).reshape(B, S, DIFF_V_W)


MERGE_TM = 256


def _group_rms(o, ones_bd, g, width):
    ss = _dot_split(o * o, ones_bd, n=2)
    return o * lax.rsqrt(ss * (1.0 / width) + RMS_EPS) * g


def _gelu_tanh(x):
    return 0.5 * x * (1.0 + jnp.tanh(0.7978845608028654 * (x + 0.044715 * (x * x * x))))


def _merge_kernel(x_ref, gla_ref, gog_ref, lru_ref, lg_ref, dif_ref, gdn_ref, nz_ref,
                  mg_ref, glag_ref, gdng_ref, ones_ref, mw_ref, mb_ref, up_ref, wo_ref, o_ref):
    x = x_ref[...]
    h = _rms_rows(x, mg_ref[...]).astype(BF16)
    ones_bd = ones_ref[...]
    y_gla = _group_rms(gla_ref[0] + gla_ref[1], ones_bd, glag_ref[...], GLA_DV) * _silu(gog_ref[...].astype(F32))
    y_lru = (lru_ref[0] + lru_ref[1]) * _gelu_tanh(lg_ref[...].astype(F32))
    y_dif = dif_ref[...]
    y_gdn = _group_rms(gdn_ref[0] + gdn_ref[1], ones_bd, gdng_ref[...], GDN_DV) * _silu(nz_ref[...].astype(F32))
    merged = None
    for i, y in enumerate((y_gla, y_lru, y_dif, y_gdn)):
        gate = _sigmoid(jnp.dot(h, mw_ref[i], preferred_element_type=F32) + mb_ref[i])
        term = gate * jnp.dot(y.astype(BF16), up_ref[i], preferred_element_type=F32)
        merged = term if merged is None else merged + term
    o_ref[...] = x + jnp.dot(merged.astype(BF16), wo_ref[...], preferred_element_type=F32)


def merge_pallas(xt, gla_o, gog, lru_h, lg, y_diff, gdn_o, nz, mix_g, gla_g, gdn_g, merge_w, merge_b, branch_up,
                 mix_out):
    T, D = xt.shape
    tm = min(MERGE_TM, T)
    W = BRANCH_WIDTH
    r = np.arange(W) // GLA_DV
    ones_bd = jnp.asarray((r[:, None] == r[None, :]).astype(np.float32)).astype(BF16)
    tok = lambda w: pl.BlockSpec((tm, w), lambda i: (i, 0))
    tok2 = pl.BlockSpec((2, tm, W), lambda i: (0, i, 0))
    const = lambda shape: pl.BlockSpec(shape, lambda i: (0,) * len(shape), pipeline_mode=pl.Buffered(1))
    return pl.pallas_call(
        _merge_kernel,
        grid=(T // tm,),
        in_specs=[tok(D), tok2, tok(W), tok2, tok(W), tok(W), tok2, tok(W),
                  const((1, D)), const((1, W)), const((1, W)), const((W, W)),
                  const((N_BRANCHES, D, D)), const((N_BRANCHES, 1, D)), const((N_BRANCHES, W, D)), const((D, D))],
        out_specs=tok(D),
        out_shape=jax.ShapeDtypeStruct((T, D), F32),
        compiler_params=_cparams("parallel"),
        name="merge",
    )(xt, gla_o, gog, lru_h, lg, y_diff, gdn_o, nz,
      mix_g.reshape(1, D).astype(F32), jnp.tile(gla_g.astype(F32), GLA_HEADS).reshape(1, W),
      jnp.tile(gdn_g.astype(F32), GDN_HEADS).reshape(1, W), ones_bd,
      merge_w.astype(BF16), merge_b.astype(F32).reshape(N_BRANCHES, 1, D), branch_up.astype(BF16),
      mix_out.astype(BF16))


XATTN_TM = 512


def _kv_kernel(m_ref, g_ref, w_ref, o_ref):
    mn = _rms_rows(m_ref[0], g_ref[...]).astype(BF16)
    o_ref[0] = jnp.dot(mn, w_ref[...], preferred_element_type=F32).astype(BF16)


def _xattn_kernel(x_ref, g_ref, wq_ref, kv_ref, wo_ref, o_ref):
    x = x_ref[0]
    h = _rms_rows(x, g_ref[...]).astype(BF16)
    q = jnp.dot(h, wq_ref[...], preferred_element_type=F32).astype(BF16)
    outs = []
    for hh in range(XATTN_HEADS):
        lo = hh * XATTN_DH
        k = kv_ref[0, :, lo:lo + XATTN_DH]
        v = kv_ref[0, :, D_MODEL + lo:D_MODEL + lo + XATTN_DH]
        s = _dot_nt(q[:, lo:lo + XATTN_DH], k) * XATTN_DH ** -0.5
        p = jnp.exp(s - jnp.max(s, axis=-1, keepdims=True))
        p = p / jnp.sum(p, axis=-1, keepdims=True)
        outs.append(jnp.dot(p.astype(BF16), v, preferred_element_type=F32).astype(BF16))
    o = jnp.concatenate(outs, axis=-1)
    o_ref[0] = x + jnp.dot(o, wo_ref[...], preferred_element_type=F32)


def xattn_pallas(x, mem, xg, mg, wq, wkv, wo):
    B, S, D = x.shape
    M = mem.shape[1]
    tm = min(XATTN_TM, S)
    kv = pl.pallas_call(
        _kv_kernel,
        grid=(B,),
        in_specs=[pl.BlockSpec((1, M, D), lambda b: (b, 0, 0)),
                  pl.BlockSpec((1, D), lambda b: (0, 0)),
                  pl.BlockSpec((D, 2 * D), lambda b: (0, 0))],
        out_specs=pl.BlockSpec((1, M, 2 * D), lambda b: (b, 0, 0)),
        out_shape=jax.ShapeDtypeStruct((B, M, 2 * D), BF16),
        compiler_params=_cparams("parallel"),
        name="xattn_kv",
    )(mem, mg.reshape(1, D).astype(F32), wkv.astype(BF16))
    const = lambda shape: pl.BlockSpec(shape, lambda b, i: (0,) * len(shape), pipeline_mode=pl.Buffered(1))
    return pl.pallas_call(
        _xattn_kernel,
        grid=(B, S // tm),
        in_specs=[pl.BlockSpec((1, tm, D), lambda b, i: (b, i, 0)),
                  const((1, D)), const((D, D)),
                  pl.BlockSpec((1, M, 2 * D), lambda b, i: (b, 0, 0)),
                  const((D, D))],
        out_specs=pl.BlockSpec((1, tm, D), lambda b, i: (b, i, 0)),
        out_shape=jax.ShapeDtypeStruct((B, S, D), F32),
        compiler_params=_cparams("parallel", "parallel"),
        name="xattn",
    )(x, xg.reshape(1, D).astype(F32), wq.astype(BF16), kv, wo.astype(BF16))


ROUTE_TM = 512
ROUTE_EXPERT_LANE = 32
MOE_BM = 256
COMBINE_TM = 512


def _router_kernel(x_ref, g_ref, w_ref, b_ref, h_ref, r_ref):
    h = _rms_rows(x_ref[...], g_ref[...])
    h_ref[...] = h
    logits = _dot_f32(h, w_ref[...]) + b_ref[...]
    lane = lax.broadcasted_iota(jnp.int32, logits.shape, 1)
    big = jnp.int32(1 << 20)
    neg = jnp.float32(-jnp.inf)

    def top(vals):
        m = jnp.max(vals, axis=-1, keepdims=True)
        idx = jnp.min(jnp.where(vals == m, lane, big), axis=-1, keepdims=True)
        return m, idx

    is_g = lane < N_GROUPS
    gmax, gidx = top(jnp.where(is_g, logits, neg))
    gsum = jnp.sum(jnp.where(is_g, jnp.exp(logits - gmax), 0.0), axis=-1, keepdims=True)
    g_w = 1.0 / gsum
    lo = ROUTE_EXPERT_LANE + EXPERTS_PER_GROUP * gidx
    in_grp = jnp.abs(2 * (lane - lo) - (EXPERTS_PER_GROUP - 1)) < EXPERTS_PER_GROUP
    el = jnp.where(in_grp, logits, neg)
    e1, i1 = top(el)
    e2, i2 = top(jnp.where(lane == i1, neg, el))
    t = jnp.exp(e2 - e1)
    w1 = g_w / (1.0 + t)
    w2 = g_w * t / (1.0 + t)
    f = lambda v: v.astype(F32)
    r_ref[...] = jnp.where(lane == 0, f(i1 - ROUTE_EXPERT_LANE),
                           jnp.where(lane == 1, f(i2 - ROUTE_EXPERT_LANE),
                                     jnp.where(lane == 2, w1, jnp.where(lane == 3, w2, 0.0))))


def router_pallas(xt, g, w_group, b_group, w_expert, b_expert):
    T, D = xt.shape
    tm = min(ROUTE_TM, T)
    w = jnp.zeros((D, LANES), F32).at[:, :N_GROUPS].set(w_group.astype(F32))
    w = w.at[:, ROUTE_EXPERT_LANE:ROUTE_EXPERT_LANE + N_EXPERTS].set(w_expert.astype(F32))
    b = jnp.zeros((1, LANES), F32).at[0, :N_GROUPS].set(b_group.astype(F32))
    b = b.at[0, ROUTE_EXPERT_LANE:ROUTE_EXPERT_LANE + N_EXPERTS].set(b_expert.astype(F32))
    return pl.pallas_call(
        _router_kernel,
        grid=(T // tm,),
        in_specs=[pl.BlockSpec((tm, D), lambda i: (i, 0)),
                  pl.BlockSpec((1, D), lambda i: (0, 0)),
                  pl.BlockSpec((D, LANES), lambda i: (0, 0)),
                  pl.BlockSpec((1, LANES), lambda i: (0, 0))],
        out_specs=[pl.BlockSpec((tm, D), lambda i: (i, 0)), pl.BlockSpec((tm, LANES), lambda i: (i, 0))],
        out_shape=[jax.ShapeDtypeStruct((T, D), F32), jax.ShapeDtypeStruct((T, LANES), F32)],
        compiler_params=_cparams("parallel"),
        name="moe_router",
    )(xt, g.reshape(1, D).astype(F32), w, b)


def _gather_rows(idx_hbm_row, src_hbm, idx_smem, dst, sem_idx, sem_rows, n):
    cp = pltpu.make_async_copy(idx_hbm_row, idx_smem, sem_idx)
    cp.start()
    cp.wait()

    def row_copy(r):
        return pltpu.make_async_copy(src_hbm.at[pl.ds(idx_smem[r], 1)], dst.at[pl.ds(r, 1)], sem_rows)

    def issue(r, c):
        row_copy(r).start()
        return c

    def drain(r, c):
        row_copy(r).wait()
        return c

    lax.fori_loop(0, n, issue, 0)
    lax.fori_loop(0, n, drain, 0)


def _expert_kernel(be_ref, nu_ref, idx_hbm, h_hbm, w1_ref, w3_ref, w2_ref, y_ref, idx_smem, xbuf, sem_idx, sem_rows,
                   *, bm):
    i = pl.program_id(0)

    @pl.when(i < nu_ref[0])
    def _():
        _gather_rows(idx_hbm.at[i], h_hbm, idx_smem, xbuf, sem_idx, sem_rows, bm)
        xb = xbuf[...].astype(BF16)
        a = jnp.dot(xb, w1_ref[0], preferred_element_type=F32)
        g = jnp.dot(xb, w3_ref[0], preferred_element_type=F32)
        y_ref[...] = jnp.dot((_silu(a) * g).astype(BF16), w2_ref[0], preferred_element_type=F32)

    @pl.when(i >= nu_ref[0])
    def _():
        y_ref[...] = jnp.zeros(y_ref.shape, F32)


def _combine_kernel(x_ref, r_ref, pos_hbm, y_hbm, g_ref, o_ref, idx_smem, ybuf, sem_idx, sem_rows, *, tm, final):
    i = pl.program_id(0)
    _gather_rows(pos_hbm.at[i], y_hbm, idx_smem, ybuf, sem_idx, sem_rows, 2 * tm)
    r = r_ref[...]
    out = x_ref[...] + r[:, 2:3] * ybuf[0:tm] + r[:, 3:4] * ybuf[tm:2 * tm]
    if final:
        out = _rms_rows(out, g_ref[...])
    o_ref[...] = out


def moe_pallas(xt, g, w_group, b_group, w_expert, b_expert, w1, w3, w2, final_g=None):
    T, D = xt.shape
    bm = MOE_BM
    h, route = router_pallas(xt, g, w_group, b_group, w_expert, b_expert)
    eid = route[:, 0:2].astype(jnp.int32).reshape(-1)
    A = T * TOP_K
    n_blk = A // bm + N_EXPERTS
    P = n_blk * bm
    order = jnp.argsort(eid)
    eid_s = eid[order]
    tok_s = order // TOP_K
    counts = jnp.zeros((N_EXPERTS,), jnp.int32).at[eid].add(1)
    padded = (counts + bm - 1) // bm * bm
    pad_end = jnp.cumsum(padded)
    pad_start = pad_end - padded
    seg_start = jnp.cumsum(counts) - counts
    dest = pad_start[eid_s] + jnp.arange(A, dtype=jnp.int32) - seg_start[eid_s]
    row_tok = jnp.zeros((P,), jnp.int32).at[dest].set(tok_s.astype(jnp.int32))
    pos = jnp.zeros((A,), jnp.int32).at[order].set(dest)
    blk_expert = jnp.minimum(
        jnp.searchsorted(pad_end, jnp.arange(n_blk, dtype=jnp.int32) * bm, side='right'), N_EXPERTS - 1
    ).astype(jnp.int32)
    n_used = (pad_end[-1] // bm).astype(jnp.int32).reshape(1)

    y = pl.pallas_call(
        functools.partial(_expert_kernel, bm=bm),
        grid_spec=pltpu.PrefetchScalarGridSpec(
            num_scalar_prefetch=2,
            grid=(n_blk,),
            in_specs=[pl.BlockSpec(memory_space=pl.ANY),
                      pl.BlockSpec(memory_space=pl.ANY),
                      pl.BlockSpec((1, D, D_EXPERT), lambda i, be, nu: (be[i], 0, 0)),
                      pl.BlockSpec((1, D, D_EXPERT), lambda i, be, nu: (be[i], 0, 0)),
                      pl.BlockSpec((1, D_EXPERT, D), lambda i, be, nu: (be[i], 0, 0))],
            out_specs=pl.BlockSpec((bm, D), lambda i, be, nu: (i, 0)),
            scratch_shapes=[pltpu.SMEM((bm,), jnp.int32), pltpu.VMEM((bm, D), F32),
                            pltpu.SemaphoreType.DMA(()), pltpu.SemaphoreType.DMA(())]),
        out_shape=jax.ShapeDtypeStruct((P, D), F32),
        compiler_params=_cparams("arbitrary"),
        name="moe_experts",
    )(blk_expert, n_used, row_tok.reshape(n_blk, bm), h, w1.astype(BF16), w3.astype(BF16), w2.astype(BF16))

    tm = min(COMBINE_TM, T)
    nt = T // tm
    pos_t = pos.reshape(nt, tm, TOP_K).transpose(0, 2, 1).reshape(nt, TOP_K * tm)
    fg = (final_g if final_g is not None else jnp.ones((D,), F32)).reshape(1, D).astype(F32)
    return pl.pallas_call(
        functools.partial(_combine_kernel, tm=tm, final=final_g is not None),
        grid=(nt,),
        in_specs=[pl.BlockSpec((tm, D), lambda i: (i, 0)),
                  pl.BlockSpec((tm, LANES), lambda i: (i, 0)),
                  pl.BlockSpec(memory_space=pl.ANY),
                  pl.BlockSpec(memory_space=pl.ANY),
                  pl.BlockSpec((1, D), lambda i: (0, 0))],
        out_specs=pl.BlockSpec((tm, D), lambda i: (i, 0)),
        out_shape=jax.ShapeDtypeStruct((T, D), F32),
        scratch_shapes=[pltpu.SMEM((TOP_K * tm,), jnp.int32), pltpu.VMEM((TOP_K * tm, D), F32),
                        pltpu.SemaphoreType.DMA(()), pltpu.SemaphoreType.DMA(())],
        compiler_params=_cparams("arbitrary"),
        name="moe_combine",
    )(xt, route, pos_t, y, fg)


PAIR_BM = 256
PAIRS_PER_GROUP = EXPERTS_PER_GROUP * (EXPERTS_PER_GROUP - 1) // 2
N_CLASSES = N_GROUPS * PAIRS_PER_GROUP


def _route_class_kernel(x_ref, g_ref, w_ref, b_ref, r_ref):
    h = _rms_rows(x_ref[...], g_ref[...])
    logits = _dot_f32(h, w_ref[...]) + b_ref[...]
    lane = lax.broadcasted_iota(jnp.int32, logits.shape, 1)
    big = jnp.int32(1 << 20)
    neg = jnp.float32(-jnp.inf)

    def top(vals):
        m = jnp.max(vals, axis=-1, keepdims=True)
        idx = jnp.min(jnp.where(vals == m, lane, big), axis=-1, keepdims=True)
        return m, idx

    _, gidx = top(jnp.where(lane < N_GROUPS, logits, neg))
    lo0 = ROUTE_EXPERT_LANE + EXPERTS_PER_GROUP * gidx
    in_grp = jnp.abs(2 * (lane - lo0) - (EXPERTS_PER_GROUP - 1)) < EXPERTS_PER_GROUP
    el = jnp.where(in_grp, logits, neg)
    _, i1 = top(el)
    _, i2 = top(jnp.where(lane == i1, neg, el))
    a = jnp.minimum(i1, i2) - lo0
    b = jnp.maximum(i1, i2) - lo0
    pair = a * (2 * EXPERTS_PER_GROUP - 1 - a) // 2 + (b - a - 1)
    r_ref[...] = jnp.broadcast_to(gidx * PAIRS_PER_GROUP + pair, r_ref.shape)


def _pair_expert_kernel(lo_ref, hi_ref, nv_ref, idx_hbm, x_hbm, g_ref, wr_ref, br_ref, fg_ref,
                        w1a_ref, w3a_ref, w2a_ref, w1b_ref, w3b_ref, w2b_ref, o_hbm,
                        idx_smem, xbuf, obuf, sem_idx, sem_in, sem_out, *, bm, final):
    i = pl.program_id(0)
    n = nv_ref[i]

    @pl.when(i == 0)
    def _():
        xbuf[...] = jnp.zeros(xbuf.shape, F32)

    @pl.when(n > 0)
    def _():
        cp = pltpu.make_async_copy(idx_hbm.at[i], idx_smem, sem_idx)
        cp.start()
        cp.wait()

        def in_copy(r):
            return pltpu.make_async_copy(x_hbm.at[pl.ds(idx_smem[r], 1)], xbuf.at[pl.ds(r, 1)], sem_in)

        def out_copy(r):
            return pltpu.make_async_copy(obuf.at[pl.ds(r, 1)], o_hbm.at[pl.ds(idx_smem[r], 1)], sem_out)

        def loop(copy_of, wait):
            def body(r, c):
                cpy = copy_of(r)
                cpy.wait() if wait else cpy.start()
                return c
            lax.fori_loop(0, n, body, 0)

        loop(in_copy, False)
        loop(in_copy, True)
        x = xbuf[...]
        h = _rms_rows(x, g_ref[...])
        logits = _dot_f32(h, wr_ref[...]) + br_ref[...]
        lane = lax.broadcasted_iota(jnp.int32, logits.shape, 1)
        lo = lo_ref[i]
        hi = hi_ref[i]
        grp = lo // EXPERTS_PER_GROUP
        is_g = lane < N_GROUPS
        gmax = jnp.max(jnp.where(is_g, logits, -jnp.inf), axis=-1, keepdims=True)
        eg = jnp.exp(logits - gmax)
        g_w = (jnp.sum(jnp.where(lane == grp, eg, 0.0), axis=-1, keepdims=True)
               / jnp.sum(jnp.where(is_g, eg, 0.0), axis=-1, keepdims=True))
        e_lo = jnp.sum(jnp.where(lane == ROUTE_EXPERT_LANE + lo, logits, 0.0), axis=-1, keepdims=True)
        e_hi = jnp.sum(jnp.where(lane == ROUTE_EXPERT_LANE + hi, logits, 0.0), axis=-1, keepdims=True)
        m = jnp.maximum(e_lo, e_hi)
        t_lo = jnp.exp(e_lo - m)
        t_hi = jnp.exp(e_hi - m)
        inv = g_w / (t_lo + t_hi)
        hb = h.astype(BF16)

        def expert(w1, w3, w2):
            a = jnp.dot(hb, w1[0], preferred_element_type=F32)
            b = jnp.dot(hb, w3[0], preferred_element_type=F32)
            return jnp.dot((_silu(a) * b).astype(BF16), w2[0], preferred_element_type=F32)

        y = expert(w1a_ref, w3a_ref, w2a_ref) * (t_lo * inv) + expert(w1b_ref, w3b_ref, w2b_ref) * (t_hi * inv)
        out = x + y
        if final:
            out = _rms_rows(out, fg_ref[...])
        obuf[...] = out
        loop(out_copy, False)
        loop(out_copy, True)


def moe_pair_pallas(xt, g, w_group, b_group, w_expert, b_expert, w1, w3, w2, final_g=None):
    T, D = xt.shape
    bm = PAIR_BM
    tm = min(ROUTE_TM, T)
    wr = jnp.zeros((D, LANES), F32).at[:, :N_GROUPS].set(w_group.astype(F32))
    wr = wr.at[:, ROUTE_EXPERT_LANE:ROUTE_EXPERT_LANE + N_EXPERTS].set(w_expert.astype(F32))
    br = jnp.zeros((1, LANES), F32).at[0, :N_GROUPS].set(b_group.astype(F32))
    br = br.at[0, ROUTE_EXPERT_LANE:ROUTE_EXPERT_LANE + N_EXPERTS].set(b_expert.astype(F32))
    g2 = g.reshape(1, D).astype(F32)
    cls = pl.pallas_call(
        _route_class_kernel,
        grid=(T // tm,),
        in_specs=[pl.BlockSpec((tm, D), lambda i: (i, 0)),
                  pl.BlockSpec((1, D), lambda i: (0, 0)),
                  pl.BlockSpec((D, LANES), lambda i: (0, 0)),
                  pl.BlockSpec((1, LANES), lambda i: (0, 0))],
        out_specs=pl.BlockSpec((tm, LANES), lambda i: (i, 0)),
        out_shape=jax.ShapeDtypeStruct((T, LANES), jnp.int32),
        compiler_params=_cparams("parallel"),
        name="moe_route",
    )(xt, g2, wr, br)[:, 0]

    n_blk = T // bm + N_CLASSES
    order = jnp.argsort(cls).astype(jnp.int32)
    cls_s = cls[order]
    bounds = jnp.searchsorted(cls_s, jnp.arange(N_CLASSES + 1, dtype=jnp.int32), side='left').astype(jnp.int32)
    seg_start = bounds[:-1]
    counts = bounds[1:] - seg_start
    nblk_c = (counts + bm - 1) // bm
    blk_end = jnp.cumsum(nblk_c)
    blk_start = blk_end - nblk_c
    bidx = jnp.arange(n_blk, dtype=jnp.int32)
    blk_cls = jnp.minimum(jnp.searchsorted(blk_end, bidx, side='right'), N_CLASSES - 1).astype(jnp.int32)
    first = seg_start[blk_cls] + (bidx - blk_start[blk_cls]) * bm
    n_valid = jnp.clip(seg_start[blk_cls] + counts[blk_cls] - first, 0, bm)
    n_valid = jnp.where(bidx < blk_end[-1], n_valid, 0).astype(jnp.int32)
    src = jnp.clip(first[:, None] + jnp.arange(bm, dtype=jnp.int32)[None, :], 0, T - 1)
    row_tok = order[src]
    pa, pb = np.triu_indices(EXPERTS_PER_GROUP, k=1)
    grp_of = np.repeat(np.arange(N_GROUPS), PAIRS_PER_GROUP) * EXPERTS_PER_GROUP
    lo_tab = jnp.asarray((grp_of + np.tile(pa, N_GROUPS)).astype(np.int32))
    hi_tab = jnp.asarray((grp_of + np.tile(pb, N_GROUPS)).astype(np.int32))
    blk_lo = lo_tab[blk_cls]
    blk_hi = hi_tab[blk_cls]

    fg = (final_g if final_g is not None else jnp.ones((D,), F32)).reshape(1, D).astype(F32)
    w1b, w3b, w2b = w1.astype(BF16), w3.astype(BF16), w2.astype(BF16)
    const = lambda shape: pl.BlockSpec(shape, lambda i, lo, hi, nv: (0,) * len(shape))
    wspec = lambda shape, which: pl.BlockSpec(
        shape, (lambda i, lo, hi, nv: (lo[i], 0, 0)) if which == 0 else (lambda i, lo, hi, nv: (hi[i], 0, 0)))
    return pl.pallas_call(
        functools.partial(_pair_expert_kernel, bm=bm, final=final_g is not None),
        grid_spec=pltpu.PrefetchScalarGridSpec(
            num_scalar_prefetch=3,
            grid=(n_blk,),
            in_specs=[pl.BlockSpec(memory_space=pl.ANY),
                      pl.BlockSpec(memory_space=pl.ANY),
                      const((1, D)), const((D, LANES)), const((1, LANES)), const((1, D)),
                      wspec((1, D, D_EXPERT), 0), wspec((1, D, D_EXPERT), 0), wspec((1, D_EXPERT, D), 0),
                      wspec((1, D, D_EXPERT), 1), wspec((1, D, D_EXPERT), 1), wspec((1, D_EXPERT, D), 1)],
            out_specs=pl.BlockSpec(memory_space=pl.ANY),
            scratch_shapes=[pltpu.SMEM((bm,), jnp.int32), pltpu.VMEM((bm, D), F32), pltpu.VMEM((bm, D), F32),
                            pltpu.SemaphoreType.DMA(()), pltpu.SemaphoreType.DMA(()), pltpu.SemaphoreType.DMA(())]),
        out_shape=jax.ShapeDtypeStruct((T, D), F32),
        compiler_params=_cparams("arbitrary"),
        name="moe_pair_experts",
    )(blk_lo, blk_hi, n_valid, row_tok, xt, g2, wr, br, fg, w1b, w3b, w2b, w1b, w3b, w2b)


GDN_TB = 512
GDN_INV_PASSES = 1
GDN_UNROLL_B = 8


def _gdn_prep_kernel(prev_ref, cur_ref, next_ref, ba_ref, cw_ref, ones_ref, alog_ref, dt_ref, qkv_ref, bg_ref, *, nt):
    qkv = _silu(_conv4(prev_ref, cur_ref, next_ref, cw_ref[...], nt))
    ones_bd = ones_ref[...]
    w = GDN_QK_W
    q = qkv[:, 0:w]
    k = qkv[:, w:2 * w]
    qn = q * lax.rsqrt(_dot_split(q * q, ones_bd, n=2) + 1e-6) * GDN_DK ** -0.5
    kn = k * lax.rsqrt(_dot_split(k * k, ones_bd, n=2) + 1e-6)
    qkv_ref[0] = jnp.concatenate([qn, kn, qkv[:, 2 * w:]], axis=-1)
    ba = ba_ref[0]
    beta = _sigmoid(ba)
    log_a = -jnp.exp(alog_ref[...]) * _softplus(ba + dt_ref[...])
    lane = lax.broadcasted_iota(jnp.int32, ba.shape, 1)
    h = GDN_HEADS
    for d in range(2):
        b_d = pltpu.roll(beta, (LANES - d * h) % LANES, 1)
        a_d = pltpu.roll(log_a, (LANES - (2 * h + d * h) + h) % LANES, 1)
        bg_ref[d, 0] = jnp.where(lane < h, b_d, jnp.where(lane < 2 * h, a_d, 0.0))


def _gdn_chunk_kernel(qkv_ref, bg_ref, tri_ref, tribd_ref, o_ref, st_ref, *, nb):
    c = GDN_CHUNK
    H = GDN_HEADS
    n = H * c

    @pl.when(pl.program_id(1) == 0)
    def _():
        st_ref[...] = jnp.zeros(st_ref.shape, F32)

    tri_bf = tri_ref[0].astype(BF16)
    incl = tribd_ref[0]
    ri = lax.broadcasted_iota(jnp.int32, (n, n), 0)
    ci = lax.broadcasted_iota(jnp.int32, (n, n), 1)
    eye = jnp.where(ri == ci, 1.0, 0.0)
    strict = incl - eye
    bdmask = jnp.where(ri // c == ci // c, 1.0, 0.0)

    def stack(x):
        w = x.shape[1] // H
        return jnp.concatenate([x[:, hh * w:(hh + 1) * w] for hh in range(H)], axis=0)

    def col(x, lane0):
        return jnp.concatenate([x[:, lane0 + hh:lane0 + hh + 1] for hh in range(H)], axis=0)

    def bd(x_st):
        return jnp.concatenate([x_st] * H, axis=1) * bdmask

    def mm(a, b, passes, dot=_dot_nn):
        if passes == 1:
            return dot(a.astype(BF16), b.astype(BF16))
        return _dot_f32(a, b, dot=dot)

    def one_batch(b):
        qkv = qkv_ref[b]
        bg = bg_ref[0, b]
        q_st = stack(qkv[:, 0:GDN_QK_W])
        k_st = stack(qkv[:, GDN_QK_W:2 * GDN_QK_W])
        v_st = stack(qkv[:, 2 * GDN_QK_W:])
        acc = None
        for p in _split_bf16(bg, 3):
            t = _dot_nn(tri_bf, p)
            acc = t if acc is None else acc + t
        gam = acc
        tot = jnp.sum(bg, axis=0, keepdims=True)
        g_col = col(gam, H)
        b_col = col(bg, 0)
        gam_t = jnp.concatenate([gam, jnp.zeros_like(gam)], axis=0).T
        g_row = jnp.concatenate([gam_t[H + hh:H + hh + 1, 0:c] for hh in range(H)], axis=1)
        end_col = jnp.concatenate([jnp.broadcast_to(tot[:, H + hh:H + hh + 1], (c, 1)) for hh in range(H)], axis=0)
        exp_g = jnp.exp(g_col)
        decay = jnp.exp(jnp.minimum(g_col - g_row, 0.0)) * incl
        k_bf = k_st.astype(BF16)
        kk = _dot_nt(k_bf, k_bf)
        a_mat = strict * b_col * kk * decay
        p_inv = eye - a_mat
        x_pow = a_mat
        for _ in range(int(math.log2(c)) - 1):
            x_pow = mm(x_pow, x_pow, GDN_INV_PASSES)
            p_inv = p_inv + mm(p_inv, x_pow, GDN_INV_PASSES)
        rhs = jnp.concatenate([v_st * b_col, k_st * (b_col * exp_g)], axis=1)
        p_bf = p_inv.astype(BF16)
        sol = _dot_nn(p_bf, rhs.astype(BF16))
        resid = rhs - sol - _dot_f32(a_mat, sol)
        sol = sol + _dot_nn(p_bf, resid.astype(BF16))
        u_st = sol[:, 0:c]
        kc_bd = bd(sol[:, c:2 * c])
        qk = _dot_nt(q_st.astype(BF16), k_bf) * decay
        qd_bd = bd(q_st * exp_g)
        kd_bd = bd(k_st * jnp.exp(end_col - g_col))
        st = st_ref[b]
        st_bf = st.astype(BF16)
        v_new = u_st - _dot_nn(kc_bd.astype(BF16), st_bf)
        v_new_bf = v_new.astype(BF16)
        o_st = _dot_nn(qd_bd.astype(BF16), st_bf) + _dot_nn(qk.astype(BF16), v_new_bf)
        st_ref[b] = jnp.exp(end_col) * st + _dot_tn(kd_bd.astype(BF16), v_new_bf)
        o_ref[0, b] = jnp.concatenate([o_st[hh * c:(hh + 1) * c] for hh in range(H)], axis=1)

    def pair(i, carry):
        for j in range(GDN_UNROLL_B):
            one_batch(i * GDN_UNROLL_B + j)
        return carry

    lax.fori_loop(0, nb // GDN_UNROLL_B, pair, 0)


def _gdn_chunk_compact_kernel(qkv_ref, bg_ref, tri_ref, tril_ref, sel_ref, o_ref, st_ref, *, nb):
    c = GDN_CHUNK
    H = GDN_HEADS
    n = H * c

    @pl.when(pl.program_id(1) == 0)
    def _():
        st_ref[...] = jnp.zeros(st_ref.shape, F32)

    tri_bf = tri_ref[0].astype(BF16)
    incl = tril_ref[0]
    ri = lax.broadcasted_iota(jnp.int32, (c, n), 0)
    ci = lax.broadcasted_iota(jnp.int32, (c, n), 1)
    eye = jnp.where(ri == ci % c, 1.0, 0.0)
    strict = incl - eye
    rb = lax.broadcasted_iota(jnp.int32, (n, n), 0) // c
    cb = lax.broadcasted_iota(jnp.int32, (n, n), 1) // c
    bdmask = rb == cb
    sel_g = sel_ref[0]
    sel_b = sel_ref[1]

    def bd(x):
        return jnp.where(bdmask, jnp.concatenate([x] * H, axis=0), 0.0)

    def one_batch(b):
        qkv = qkv_ref[b]
        bg = bg_ref[0, b]
        q = qkv[:, 0:n]
        k = qkv[:, n:2 * n]
        v = qkv[:, 2 * n:]
        gam = _dot_split(bg, None, dot=lambda p, _: _dot_nn(tri_bf, p))
        k_bf = k.astype(BF16)
        kq = _dot_nt(jnp.concatenate([k_bf, q.astype(BF16)], axis=0), bd(k).astype(BF16))
        bexp = _dot_split(bg, sel_b)
        yield
        tot = jnp.sum(bg, axis=0, keepdims=True)
        gt = jnp.concatenate([gam, jnp.broadcast_to(tot, (8, LANES))], axis=0)
        ge = _dot_split(gt, sel_g)
        yield
        gexp = ge[0:c]
        end_row = ge[c:c + 1]
        gam_t = jnp.concatenate([gam, jnp.zeros_like(gam)], axis=0).T
        g_row = jnp.concatenate([gam_t[H + hh:H + hh + 1, 0:c] for hh in range(H)], axis=1)
        decay = jnp.exp(jnp.minimum(gexp - g_row, 0.0)) * incl
        exp_g = jnp.exp(gexp)
        a_c = strict * bexp * kq[0:c] * decay
        qk = kq[c:2 * c] * decay
        p_c = eye - a_c
        x_c = _dot_nn(a_c.astype(BF16), bd(a_c).astype(BF16))
        yield
        n_sq = int(math.log2(c)) - 1
        for it in range(n_sq):
            r_bd = bd(x_c).astype(BF16)
            if it + 1 < n_sq:
                both = _dot_nn(jnp.concatenate([x_c, p_c], axis=0).astype(BF16), r_bd)
                yield
                x_c = both[0:c]
                p_c = p_c + both[c:2 * c]
            else:
                dp = _dot_nn(p_c.astype(BF16), r_bd)
                yield
                p_c = p_c + dp
        ap = _dot_f32(a_c, bd(p_c))
        yield
        resid = eye - p_c - ap
        p_bf = p_c.astype(BF16)
        dp = _dot_nn(p_bf, bd(resid).astype(BF16))
        yield
        p_c = p_c + dp
        p_bf = p_c.astype(BF16)
        u = _dot_nn(p_bf, bd(v * bexp).astype(BF16))
        kc = _dot_nn(p_bf, bd(k * (bexp * exp_g)).astype(BF16))
        yield
        st = st_ref[b]
        st_bf = st.astype(BF16)
        sq = _dot_nn(jnp.concatenate([kc, q * exp_g], axis=0).astype(BF16), st_bf)
        yield
        v_new = u - sq[0:c]
        o_ref[0, b] = sq[c:2 * c] + _dot_nn(qk.astype(BF16), bd(v_new).astype(BF16))
        kd = (k * jnp.exp(end_row - gexp)).astype(BF16)
        upd = _dot_tn(kd, v_new.astype(BF16))
        st_ref[b] = st * jnp.exp(end_row) + jnp.where(bdmask, upd, 0.0)

    n_side = math.gcd(nb, GDN_UNROLL_B)

    def group(i, carry):
        chains = [one_batch(i * n_side + j) for j in range(n_side)]
        live = True
        while live:
            live = False
            for ch in chains:
                try:
                    next(ch)
                    live = True
                except StopIteration:
                    pass
        return carry

    if nb == n_side:
        group(0, 0)
    else:
        lax.fori_loop(0, nb // n_side, group, 0)


def gdn_pallas(nqkv, nba, conv_w, a_log, dt_bias):
    B, S, W = nqkv.shape
    tb = min(GDN_TB, S)
    nt = S // tb
    c = GDN_CHUNK
    nc = S // c
    H = GDN_HEADS
    r = np.arange(GDN_QK_W) // GDN_DK
    ones_bd = jnp.asarray((r[:, None] == r[None, :]).astype(np.float32)).astype(BF16)
    alog = jnp.zeros((1, LANES), F32).at[0, 2 * H:4 * H].set(a_log.astype(F32).reshape(-1))
    dt = jnp.zeros((1, LANES), F32).at[0, 2 * H:4 * H].set(dt_bias.astype(F32).reshape(-1))
    full = lambda shape: pl.BlockSpec(shape, lambda b, i: (0,) * len(shape))
    qkvn, bg = pl.pallas_call(
        functools.partial(_gdn_prep_kernel, nt=nt),
        grid=(B, nt),
        in_specs=_halo_specs(tb, W, nt) + [pl.BlockSpec((1, tb, LANES), lambda b, i: (b, i, 0)),
                                           full((CONV_WIDTH, W)), full((GDN_QK_W, GDN_QK_W)),
                                           full((1, LANES)), full((1, LANES))],
        out_specs=[pl.BlockSpec((1, tb, W), lambda b, i: (b, i, 0)),
                   pl.BlockSpec((2, 1, tb, LANES), lambda b, i: (0, b, i, 0))],
        out_shape=[jax.ShapeDtypeStruct((B, S, W), F32), jax.ShapeDtypeStruct((2, B, S, LANES), F32)],
        compiler_params=_cparams("parallel", "parallel"),
        name="gdn_prep",
    )(nqkv, nqkv, nqkv, nba, conv_w.astype(F32), ones_bd, alog, dt)

    t = np.arange(c)
    tri_np = np.stack([t[:, None] >= t[None, :], t[:, None] <= t[None, :]]).astype(np.float32)
    tri = jnp.asarray(tri_np)
    tri_lanes = jnp.asarray(np.tile(tri_np, (1, 1, H)))
    lane = np.arange(LANES)[:, None]
    head = (np.arange(H * c) // c)[None, :]
    sel = jnp.asarray(np.stack([lane == H + head, lane == head]).astype(np.float32)).astype(BF16)

    def tmap(d, i):
        return i + d * (nc - 1 - 2 * i)

    return pl.pallas_call(
        functools.partial(_gdn_chunk_compact_kernel, nb=B),
        grid=(2, nc),
        in_specs=[pl.BlockSpec((B, c, W), lambda d, i: (0, tmap(d, i), 0)),
                  pl.BlockSpec((1, B, c, LANES), lambda d, i: (d, 0, tmap(d, i), 0)),
                  pl.BlockSpec((1, c, c), lambda d, i: (d, 0, 0)),
                  pl.BlockSpec((1, c, H * c), lambda d, i: (d, 0, 0)),
                  pl.BlockSpec((2, LANES, H * c), lambda d, i: (0, 0, 0))],
        out_specs=pl.BlockSpec((1, B, c, GDN_V_W), lambda d, i: (d, 0, tmap(d, i), 0)),
        out_shape=jax.ShapeDtypeStruct((2, B, S, GDN_V_W), F32),
        scratch_shapes=[pltpu.VMEM((B, H * GDN_DK, H * GDN_DV), F32)],
        compiler_params=_cparams("arbitrary", "arbitrary"),
        name="gdn_chunk",
    )(qkvn, bg, tri, tri_lanes, sel)


def gdn_scan(q, k, v, beta, log_a):
    b_, s_, h_, dk = q.shape
    dv = v.shape[-1]
    c = GDN_CHUNK
    qc, kc, vc, bc, gc = (_to_chunks(t, c) for t in (q, k, v, beta, log_a))
    gam = jnp.cumsum(gc, axis=-1)
    diff = gam[..., :, None] - gam[..., None, :]
    incl = jnp.tril(jnp.ones((c, c), dtype=bool))
    strict = jnp.tril(jnp.ones((c, c), dtype=bool), k=-1)
    decay = jnp.exp(jnp.where(incl, diff, -jnp.inf))
    kk = jnp.einsum('bhntd,bhnsd->bhnts', kc, kc)
    a_mat = jnp.where(strict, bc[..., None] * kk * decay, 0.0) + jnp.eye(c, dtype=jnp.float32)
    rhs = jnp.concatenate([vc * bc[..., None], kc * (bc * jnp.exp(gam))[..., None]], axis=-1)
    sol = lax.linalg.triangular_solve(a_mat, rhs, left_side=True, lower=True, unit_diagonal=True)
    u_val, k_cum = sol[..., :dv], sol[..., dv:]
    qk = jnp.einsum('bhntd,bhnsd->bhnts', qc, kc) * decay
    q_dec = qc * jnp.exp(gam)[..., None]
    k_dec = kc * jnp.exp(gam[..., -1:] - gam)[..., None]
    c_dec = jnp.exp(gam[..., -1])

    def step(state, inp):
        u_i, kc_i, qk_i, qd_i, kd_i, cd_i = inp
        v_new = u_i - jnp.einsum('bhtd,bhde->bhte', kc_i, state)
        o = jnp.einsum('bhtd,bhde->bhte', qd_i, state) + jnp.einsum('bhts,bhse->bhte', qk_i, v_new)
        state = cd_i[..., None, None] * state + jnp.einsum('bhsd,bhse->bhde', kd_i, v_new)
        return state, o

    xs = tuple(jnp.moveaxis(t, 2, 0) for t in (u_val, k_cum, qk, q_dec, k_dec, c_dec))
    _, o = lax.scan(step, jnp.zeros((b_, h_, dk, dv), jnp.float32), xs)
    return _from_chunks(o)


def gdn_mixer(qkv, z, ba, conv_w, a_log, dt_bias, norm_g):
    B, S, _ = qkv.shape
    f32 = jnp.float32
    qkv = jax.nn.silu(dwconv_centred(qkv, conv_w)).astype(f32)
    q, k, v = jnp.split(qkv, [GDN_QK_W, 2 * GDN_QK_W], axis=-1)
    q = l2norm(q.reshape(B, S, GDN_HEADS, GDN_DK)) * GDN_DK ** -0.5
    k = l2norm(k.reshape(B, S, GDN_HEADS, GDN_DK))
    v = v.reshape(B, S, GDN_HEADS, GDN_DV)
    ba = ba.astype(f32).reshape(B, S, 4, GDN_HEADS)
    beta = jax.nn.sigmoid(ba[:, :, :2])
    log_a = -jnp.exp(a_log.astype(f32)) * jax.nn.softplus(ba[:, :, 2:] + dt_bias.astype(f32))
    o_f = gdn_scan(q, k, v, beta[:, :, 0], log_a[:, :, 0])
    o_b = _flip(gdn_scan(_flip(q), _flip(k), _flip(v), _flip(beta[:, :, 1]), _flip(log_a[:, :, 1])))
    o = rmsnorm(o_f + o_b, norm_g) * jax.nn.silu(z.astype(f32)).reshape(B, S, GDN_HEADS, GDN_DV)
    return o.reshape(B, S, GDN_V_W).astype(z.dtype)


def gated_merge(h, branches, merge_w, merge_b, branch_up):
    out = None
    for i, y in enumerate(branches):
        term = jax.nn.sigmoid(h @ merge_w[i] + merge_b[i]) * (y @ branch_up[i])
        out = term if out is None else out + term
    return out


def memory_cross_attention(h, mem_n, w_q, w_kv, w_o):
    B, S, D = h.shape
    M = mem_n.shape[1]
    q = (h @ w_q).reshape(B, S, XATTN_HEADS, XATTN_DH)
    kv = (mem_n @ w_kv).reshape(B, M, 2, XATTN_HEADS, XATTN_DH)
    k, v = kv[:, :, 0], kv[:, :, 1]
    s = jnp.einsum('bshd,bmhd->bhsm', q, k).astype(jnp.float32) * XATTN_DH ** -0.5
    p = jax.nn.softmax(s, axis=-1)
    o = jnp.einsum('bhsm,bmhd->bshd', p.astype(v.dtype), v).reshape(B, S, D)
    return o @ w_o


def hier_moe(h, w_group, b_group, w_expert, b_expert, w1, w3, w2):
    B, S, D = h.shape
    T = B * S
    f32 = jnp.float32
    ht = h.reshape(T, D)
    g_logits = (ht @ w_group).astype(f32) + b_group.astype(f32)
    g_prob = jax.nn.softmax(g_logits, axis=-1)
    _, g_idx = lax.top_k(g_logits, 1)
    g_w = jnp.take_along_axis(g_prob, g_idx, axis=1)
    e_logits = ((ht @ w_expert).astype(f32) + b_expert.astype(f32)).reshape(T, N_GROUPS, EXPERTS_PER_GROUP)
    e_logits = jnp.take_along_axis(
        e_logits, jnp.broadcast_to(g_idx[:, :, None], (T, 1, EXPERTS_PER_GROUP)), axis=1)[:, 0]
    e_top, e_idx = lax.top_k(e_logits, TOP_K)
    gate = jax.nn.softmax(e_top, axis=-1) * g_w
    eid = (g_idx * EXPERTS_PER_GROUP + e_idx).reshape(-1)
    tok = jnp.repeat(jnp.arange(T, dtype=jnp.int32), TOP_K)
    wts = gate.reshape(-1)
    A = T * TOP_K
    n_blk = -(-A // MOE_BLOCK) + N_EXPERTS
    P = n_blk * MOE_BLOCK
    order = jnp.argsort(eid)
    eid_s, tok_s, w_s = eid[order], tok[order], wts[order]
    counts = jnp.zeros((N_EXPERTS,), jnp.int32).at[eid].add(1)
    padded = (counts + MOE_BLOCK - 1) // MOE_BLOCK * MOE_BLOCK
    pad_end = jnp.cumsum(padded)
    pad_start = pad_end - padded
    seg_start = jnp.cumsum(counts) - counts
    dest = pad_start[eid_s] + jnp.arange(A, dtype=jnp.int32) - seg_start[eid_s]
    row_tok = jnp.full((P,), T, jnp.int32).at[dest].set(tok_s)
    row_w = jnp.zeros((P,), f32).at[dest].set(w_s)
    blk_expert = jnp.minimum(
        jnp.searchsorted(pad_end, jnp.arange(n_blk, dtype=jnp.int32) * MOE_BLOCK, side='right'),
        N_EXPERTS - 1)
    h_pad = jnp.concatenate([ht, jnp.zeros((1, D), ht.dtype)], axis=0)
    xb = h_pad[row_tok].reshape(n_blk, MOE_BLOCK, D)

    def expert_block(args):
        xi, e = args
        return (jax.nn.silu(xi @ w1[e]) * (xi @ w3[e])) @ w2[e]

    yb = lax.map(expert_block, (xb, blk_expert)).reshape(P, D)
    out = jnp.zeros((T + 1, D), h.dtype).at[row_tok].add(yb * row_w[:, None].astype(yb.dtype))
    return out[:T].reshape(B, S, D)


def kernel(x, mem, mix_norm, w_in, gla_lr_up, gla_lr_bias, gla_norm, lru_conv_w, lru_conv_b, lru_w_a, lru_b_a, lru_w_x, lru_b_x, lru_lambda, diff_lq1, diff_lk1, diff_lq2, diff_lk2, diff_norm, gdn_conv_w, gdn_a_log, gdn_dt_bias, gdn_norm, merge_w, merge_b, branch_up, mix_out, xattn_norm, mem_norm, xattn_wq, xattn_wkv, xattn_wo, moe_norm, moe_w_group, moe_b_group, moe_w_expert, moe_b_expert, moe_w1, moe_w3, moe_w2, final_norm):
    B, S, D = x.shape
    T = B * S
    xt = x.reshape(T, D)
    for l in range(DEPTH):
        gla_in, gog, lx, lg, dq, dk, dv, nqkv, nz, nba = inproj_pallas(xt, mix_norm[l], w_in[l])
        gla_o = gla_pallas(gla_in.reshape(B, S, -1), gla_lr_up[l], gla_lr_bias[l])
        lru_h = lru_pallas(lx.reshape(B, S, -1), lru_conv_w[l], lru_conv_b[l], lru_w_a[l], lru_b_a[l],
                           lru_w_x[l], lru_b_x[l], lru_lambda[l])
        y_diff = diff_attention_pallas(dq.reshape(B, S, -1), dk.reshape(B, S, -1), dv.reshape(B, S, -1),
                                       diff_lq1[l], diff_lk1[l], diff_lq2[l], diff_lk2[l], diff_norm[l],
                                       0.8 - 0.6 * math.exp(-0.3 * l))
        gdn_o = gdn_pallas(nqkv.reshape(B, S, -1), nba.reshape(B, S, -1), gdn_conv_w[l], gdn_a_log[l],
                           gdn_dt_bias[l])
        xt = merge_pallas(xt, gla_o.reshape(2, T, -1), gog, lru_h.reshape(2, T, -1), lg, y_diff.reshape(T, -1),
                          gdn_o.reshape(2, T, -1), nz, mix_norm[l], gla_norm[l], gdn_norm[l],
                          merge_w[l], merge_b[l], branch_up[l], mix_out[l])
        xt = xattn_pallas(xt.reshape(B, S, D), mem, xattn_norm[l], mem_norm[l], xattn_wq[l], xattn_wkv[l],
                          xattn_wo[l]).reshape(T, D)
        xt = moe_pair_pallas(xt, moe_norm[l], moe_w_group[l], moe_b_group[l], moe_w_expert[l], moe_b_expert[l],
                             moe_w1[l], moe_w3[l], moe_w2[l], final_g=final_norm if l == DEPTH - 1 else None)
    return xt.reshape(B, S, D)
```

```python
import functools
import math

import jax
import jax.numpy as jnp
import numpy as np
from jax import lax
from jax.experimental import pallas as pl
from jax.experimental.pallas import tpu as pltpu

D_MODEL = 1024
DEPTH = 2
N_BRANCHES = 4
BRANCH_WIDTH = D_MODEL // 4
RMS_EPS = 1e-6
CONV_WIDTH = 4

GLA_HEADS = 4
GLA_DV = BRANCH_WIDTH // GLA_HEADS
GLA_DK = GLA_DV // 2
GLA_RANK = 16
GLA_GATE_NORM = 16.0
GLA_CHUNK = 64
GLA_QK_W = GLA_HEADS * GLA_DK
GLA_V_W = GLA_HEADS * GLA_DV

LRU_WIDTH = BRANCH_WIDTH
LRU_BLOCKS = 4
LRU_C = 8.0

DIFF_HEADS = 4
DIFF_DV = BRANCH_WIDTH // DIFF_HEADS
DIFF_DH = DIFF_DV // 2
DIFF_QK_W = DIFF_HEADS * 2 * DIFF_DH
DIFF_V_W = DIFF_HEADS * DIFF_DV

GDN_HEADS = 4
GDN_DK = BRANCH_WIDTH // GDN_HEADS
GDN_DV = BRANCH_WIDTH // GDN_HEADS
GDN_CHUNK = 64
GDN_QK_W = GDN_HEADS * GDN_DK
GDN_V_W = GDN_HEADS * GDN_DV

IN_SPLITS = (GLA_QK_W, GLA_QK_W, GLA_V_W, GLA_V_W, 2 * GLA_RANK,
             LRU_WIDTH, LRU_WIDTH,
             DIFF_QK_W, DIFF_QK_W, DIFF_V_W,
             2 * GDN_QK_W + GDN_V_W, GDN_V_W, 4 * GDN_HEADS)

XATTN_HEADS = 4
XATTN_DH = D_MODEL // XATTN_HEADS

N_GROUPS = 4
EXPERTS_PER_GROUP = 8
N_EXPERTS = N_GROUPS * EXPERTS_PER_GROUP
TOP_K = 2
D_EXPERT = D_MODEL // 2
MOE_BLOCK = 128

LANES = 128
VMEM_LIMIT = 56 * 1024 * 1024

F32 = jnp.float32
BF16 = jnp.bfloat16
NEG_BIG = -1e30


def rmsnorm(x, g):
    xf = x.astype(jnp.float32)
    y = xf * lax.rsqrt(jnp.mean(xf * xf, axis=-1, keepdims=True) + RMS_EPS)
    return (y * g.astype(jnp.float32)).astype(x.dtype)


def l2norm(t):
    return t * lax.rsqrt(jnp.sum(t * t, axis=-1, keepdims=True) + 1e-6)


def _flip(t):
    return jnp.flip(t, axis=1)


def dwconv_centred(x, w):
    k = w.shape[0]
    return lax.conv_general_dilated(
        x, w[:, None, :].astype(x.dtype), window_strides=(1,),
        padding=[(k // 2, k - 1 - k // 2)],
        dimension_numbers=('NWC', 'WIO', 'NWC'),
        feature_group_count=x.shape[-1])


def _to_chunks(t, chunk):
    b, s, h = t.shape[:3]
    t = t.reshape((b, s // chunk, chunk, h) + t.shape[3:])
    return jnp.moveaxis(t, 3, 1)


def _from_chunks(o):
    nc, b, h, c, d = o.shape
    return o.transpose(1, 0, 3, 2, 4).reshape(b, nc * c, h, d)


def _linrec_combine(c1, c2):
    a1, b1 = c1
    a2, b2 = c2
    return a1 * a2, a2 * b1 + b2


def gla_scan(q, k, v, log_f):
    b_, s_, h_, dk = q.shape
    dv = v.shape[-1]
    c = GLA_CHUNK
    incl = jnp.tril(jnp.ones((c, c), dtype=bool))[:, :, None]

    def step(state, inp):
        q_i, k_i, v_i, g_i = inp
        bcum = jnp.cumsum(g_i, axis=2)
        decay = jnp.exp(jnp.where(incl, bcum[:, :, :, None, :] - bcum[:, :, None, :, :], -jnp.inf))
        scores = jnp.einsum('bhtd,bhsd,bhtsd->bhts', q_i, k_i, decay)
        o = (jnp.einsum('bhts,bhse->bhte', scores, v_i)
             + jnp.einsum('bhtd,bhde->bhte', q_i * jnp.exp(bcum), state))
        b_end = bcum[:, :, -1:, :]
        state = (jnp.exp(b_end)[:, :, 0, :, None] * state
                 + jnp.einsum('bhsd,bhse->bhde', k_i * jnp.exp(b_end - bcum), v_i))
        return state, o

    xs = tuple(jnp.moveaxis(_to_chunks(t, c), 2, 0) for t in (q, k, v, log_f))
    _, o = lax.scan(step, jnp.zeros((b_, h_, dk, dv), jnp.float32), xs)
    return _from_chunks(o)


def gla_mixer(q, k, v, og, lr, lr_up, lr_bias, norm_g):
    B, S, _ = q.shape
    f32 = jnp.float32
    qh = q.astype(f32).reshape(B, S, GLA_HEADS, GLA_DK) * GLA_DK ** -0.5
    kh = k.astype(f32).reshape(B, S, GLA_HEADS, GLA_DK)
    vh = v.astype(f32).reshape(B, S, GLA_HEADS, GLA_DV)
    lr = lr.astype(f32).reshape(B, S, 2, GLA_RANK)
    z = jnp.einsum('bsdr,drk->bsdk', lr, lr_up.astype(f32)) + lr_bias.astype(f32)
    log_f = (jax.nn.log_sigmoid(z) / GLA_GATE_NORM).reshape(B, S, 2, GLA_HEADS, GLA_DK)
    o_f = gla_scan(qh, kh, vh, log_f[:, :, 0])
    o_b = _flip(gla_scan(_flip(qh), _flip(kh), _flip(vh), _flip(log_f[:, :, 1])))
    o = rmsnorm(o_f + o_b, norm_g) * jax.nn.silu(og.astype(f32)).reshape(B, S, GLA_HEADS, GLA_DV)
    return o.reshape(B, S, GLA_V_W).astype(q.dtype)


def rglru_mixer(xb, gb, conv_w, conv_b, w_a, b_a, w_x, b_x, lam):
    B, S, W = xb.shape
    f32 = jnp.float32
    u = (dwconv_centred(xb, conv_w) + conv_b).astype(f32)
    ub = u.reshape(B, S, LRU_BLOCKS, W // LRU_BLOCKS)

    def gate(w, b):
        return jax.nn.sigmoid(jnp.einsum('bsni,nio->bsno', ub, w.astype(f32)).reshape(B, S, W) + b.astype(f32))

    def direction(d, reverse):
        r = gate(w_a[d], b_a[d])
        i = gate(w_x[d], b_x[d])
        log_a = -LRU_C * r * jax.nn.softplus(-lam[d].astype(f32))
        a = jnp.exp(log_a)
        xin = jnp.sqrt(-jnp.expm1(2.0 * log_a)) * (i * u)
        _, hs = lax.associative_scan(_linrec_combine, (a, xin), axis=1, reverse=reverse)
        return hs

    hsum = direction(0, False) + direction(1, True)
    return (hsum * jax.nn.gelu(gb.astype(f32))).astype(xb.dtype)


def _cparams(*sem):
    return pltpu.CompilerParams(dimension_semantics=sem, vmem_limit_bytes=VMEM_LIMIT)


def _split_bf16(a, n):
    parts, r = [], a
    for i in range(n):
        p = r.astype(BF16)
        parts.append(p)
        if i + 1 < n:
            r = r - p.astype(F32)
    return parts


def _split_trunc_bf16(a, n):
    parts, r = [], a
    for i in range(n):
        bits = lax.bitcast_convert_type(r, jnp.uint32) & jnp.uint32(0xFFFF0000)
        p = lax.bitcast_convert_type(bits, F32)
        parts.append(p.astype(BF16))
        if i + 1 < n:
            r = r - p
    return parts


def _dot_nn(a, b):
    return jnp.dot(a, b, preferred_element_type=F32)


def _dot_nt(a, b):
    return lax.dot_general(a, b, (((1,), (1,)), ((), ())), preferred_element_type=F32)


def _dot_tn(a, b):
    return lax.dot_general(a, b, (((0,), (0,)), ((), ())), preferred_element_type=F32)


def _dot_split(a, b_exact, n=3, dot=_dot_nn):
    acc = None
    for p in _split_bf16(a, n):
        t = dot(p, b_exact)
        acc = t if acc is None else acc + t
    return acc


def _dot_f32(a, b, dot=_dot_nn):
    a_hi, a_lo = _split_bf16(a, 2)
    b_hi, b_lo = _split_bf16(b, 2)
    return dot(a_hi, b_hi) + (dot(a_hi, b_lo) + dot(a_lo, b_hi))


def _rms_rows(x, g):
    return x * lax.rsqrt(jnp.mean(x * x, axis=-1, keepdims=True) + RMS_EPS) * g


def _sigmoid(x):
    return 1.0 / (1.0 + jnp.exp(-x))


def _softplus(x):
    return jnp.maximum(x, 0.0) + jnp.log(1.0 + jnp.exp(-jnp.abs(x)))


def _silu(x):
    return x * _sigmoid(x)


_G_GLA = (0, 640)
_G_GOG = (640, 896)
_G_LX = (896, 1152)
_G_LG = (1152, 1408)
_G_DQ = (1408, 1664)
_G_DK = (1664, 1920)
_G_DV = (1920, 2176)
_G_NQKV = (2176, 2944)
_G_NZ = (2944, 3200)
_G_NBA = (3200, 3328)
IN_PAD_W = 3328
IN_TM = 512


def _permute_w_in(w):
    z = lambda n: jnp.zeros((w.shape[0], n), w.dtype)
    return jnp.concatenate([w[:, 0:512], w[:, 768:800], z(96), w[:, 512:768], w[:, 800:3120], z(112)],
                           axis=1).astype(BF16)


def _inproj_kernel(x_ref, g_ref, w_ref, gla_ref, gog_ref, lx_ref, lg_ref, dq_ref, dk_ref, dv_ref,
                   nqkv_ref, nz_ref, nba_ref):
    h = _rms_rows(x_ref[...], g_ref[...]).astype(BF16)

    def proj(grp):
        return jnp.dot(h, w_ref[:, grp[0]:grp[1]], preferred_element_type=F32)

    gla_ref[...] = proj(_G_GLA)
    gog_ref[...] = proj(_G_GOG).astype(BF16)
    lx_ref[...] = proj(_G_LX)
    lg_ref[...] = proj(_G_LG).astype(BF16)
    dq_ref[...] = (proj(_G_DQ) * (DIFF_DH ** -0.5 * LOG2E)).astype(BF16)
    dk_ref[...] = proj(_G_DK).astype(BF16)
    dv_ref[...] = proj(_G_DV).astype(BF16)
    nqkv_ref[...] = proj(_G_NQKV)
    nz_ref[...] = proj(_G_NZ).astype(BF16)
    nba_ref[...] = proj(_G_NBA)


def inproj_pallas(xt, g, w_in):
    T, D = xt.shape
    tm = min(IN_TM, T)
    groups = [(_G_GLA, F32), (_G_GOG, BF16), (_G_LX, F32), (_G_LG, BF16), (_G_DQ, BF16), (_G_DK, BF16),
              (_G_DV, BF16), (_G_NQKV, F32), (_G_NZ, BF16), (_G_NBA, F32)]
    return pl.pallas_call(
        _inproj_kernel,
        grid=(T // tm,),
        in_specs=[pl.BlockSpec((tm, D), lambda i: (i, 0)),
                  pl.BlockSpec((1, D), lambda i: (0, 0)),
                  pl.BlockSpec((D, IN_PAD_W), lambda i: (0, 0))],
        out_specs=[pl.BlockSpec((tm, b - a), lambda i: (i, 0)) for (a, b), _ in groups],
        out_shape=[jax.ShapeDtypeStruct((T, b - a), dt) for (a, b), dt in groups],
        compiler_params=_cparams("parallel"),
        name="inproj",
    )(xt, g.reshape(1, D).astype(F32), _permute_w_in(w_in))


def _gla_kernel(x_ref, tri_ref, w_ref, b_ref, o_ref, st_ref, *, nb):
    c = GLA_CHUNK

    @pl.when(pl.program_id(1) == 0)
    def _():
        st_ref[...] = jnp.zeros(st_ref.shape, F32)

    tri = tri_ref[0]
    tri_bf = tri.astype(BF16)
    tri4 = jnp.concatenate([tri] * GLA_HEADS, axis=0)
    w = w_ref[0]
    bias = b_ref[0]
    lane_qk = lax.broadcasted_iota(jnp.int32, (c, GLA_QK_W), 1) // GLA_DK
    lane_v = lax.broadcasted_iota(jnp.int32, (c, GLA_V_W), 1) // GLA_DV
    row_s = lax.broadcasted_iota(jnp.int32, (GLA_V_W, GLA_QK_W), 0) // GLA_DV
    col_s = lax.broadcasted_iota(jnp.int32, (GLA_V_W, GLA_QK_W), 1) // GLA_DK
    st_mask = row_s == col_s

    def one_batch(b):
        blk = x_ref[b]
        q = blk[:, 0:128] * GLA_DK ** -0.5
        k = blk[:, 128:256]
        v = blk[:, 256:512].astype(BF16)
        lr = blk[:, 512:640]
        z = _dot_f32(lr, w) + bias
        yield
        g = (jnp.minimum(z, 0.0) - jnp.log(1.0 + jnp.exp(-jnp.abs(z)))) * (1.0 / GLA_GATE_NORM)
        acc = None
        for p in _split_bf16(g, 3):
            t = _dot_nn(tri_bf, p)
            acc = t if acc is None else acc + t
        yield
        bc = acc
        tot = jnp.sum(g, axis=0, keepdims=True)
        ref = 0.5 * tot
        qt = q * jnp.exp(bc - ref)
        kt = (k * jnp.exp(ref - bc)).astype(BF16)
        qd = (q * jnp.exp(bc)).astype(BF16)
        kd = (k * jnp.exp(tot - bc)).astype(BF16)
        qstack = jnp.concatenate(
            [jnp.where(lane_qk == hh, qt, 0.0) for hh in range(GLA_HEADS)], axis=0).astype(BF16)
        s = _dot_nt(qstack, kt)
        st = st_ref[b]
        o = _dot_nt(qd, st.astype(BF16))
        upd = _dot_tn(v, kd)
        yield
        s = (s * tri4).astype(BF16)
        ohs = [_dot_nn(s[hh * c:(hh + 1) * c], v) for hh in range(GLA_HEADS)]
        yield
        for hh in range(GLA_HEADS):
            o = o + jnp.where(lane_v == hh, ohs[hh], 0.0)
        o_ref[0, b] = o
        st_ref[b] = st * jnp.exp(tot) + jnp.where(st_mask, upd, 0.0)

    chains = [one_batch(b) for b in range(nb)]
    live = True
    while live:
        live = False
        for ch in chains:
            try:
                next(ch)
                live = True
            except StopIteration:
                pass


def gla_pallas(gla_in, lr_up, lr_bias):
    B, S, W = gla_in.shape
    c = GLA_CHUNK
    nc = S // c
    r = np.arange(c)
    tri = jnp.asarray(np.stack([r[:, None] >= r[None, :], r[:, None] <= r[None, :]]).astype(np.float32))
    w = jnp.zeros((2, LANES, GLA_QK_W), F32)
    w = w.at[0, 0:GLA_RANK].set(lr_up[0].astype(F32)).at[1, GLA_RANK:2 * GLA_RANK].set(lr_up[1].astype(F32))
    bias = lr_bias.astype(F32).reshape(2, 1, GLA_QK_W)

    def tmap(d, i):
        return (0, i + d * (nc - 1 - 2 * i), 0)

    return pl.pallas_call(
        functools.partial(_gla_kernel, nb=B),
        grid=(2, nc),
        in_specs=[pl.BlockSpec((B, c, W), tmap),
                  pl.BlockSpec((1, c, c), lambda d, i: (d, 0, 0)),
                  pl.BlockSpec((1, LANES, GLA_QK_W), lambda d, i: (d, 0, 0)),
                  pl.BlockSpec((1, 1, GLA_QK_W), lambda d, i: (d, 0, 0))],
        out_specs=pl.BlockSpec((1, B, c, GLA_V_W), lambda d, i: (d,) + tmap(d, i)),
        out_shape=jax.ShapeDtypeStruct((2, B, S, GLA_V_W), F32),
        scratch_shapes=[pltpu.VMEM((B, GLA_V_W, GLA_QK_W), F32)],
        compiler_params=_cparams("arbitrary", "arbitrary"),
        name="gla",
    )(gla_in, tri, w, bias)


LRU_TB = 512
HALO = 8


def _halo_specs(tb, width, nt):
    r = tb // HALO
    return [pl.BlockSpec((1, HALO, width), lambda b, i: (b, jnp.maximum(i * r - 1, 0), 0)),
            pl.BlockSpec((1, tb, width), lambda b, i: (b, i, 0)),
            pl.BlockSpec((1, HALO, width), lambda b, i: (b, jnp.minimum((i + 1) * r, nt * r - 1), 0))]


def _conv4(prev_ref, cur_ref, next_ref, w, nt):
    i = pl.program_id(1)
    cur = cur_ref[0]
    tb = cur.shape[0]
    prev = prev_ref[0] * jnp.where(i > 0, 1.0, 0.0)
    nxt = next_ref[0] * jnp.where(i < nt - 1, 1.0, 0.0)
    ext = jnp.concatenate([prev, cur, nxt], axis=0)
    out = None
    for j in range(CONV_WIDTH):
        off = HALO + j - CONV_WIDTH // 2
        t = ext[off:off + tb] * w[j:j + 1]
        out = t if out is None else out + t
    return out


def _lru_prep_kernel(prev_ref, cur_ref, next_ref, cw_ref, cb_ref, wg_ref, bg_ref, lam_ref, a_ref, x_ref, *, nt):
    u = _conv4(prev_ref, cur_ref, next_ref, cw_ref[...], nt) + cb_ref[...]
    gates = _sigmoid(jnp.dot(u.astype(BF16), wg_ref[...], preferred_element_type=F32) + bg_ref[...])
    w = LRU_WIDTH
    for d in range(2):
        r = gates[:, (2 * d) * w:(2 * d + 1) * w]
        ig = gates[:, (2 * d + 1) * w:(2 * d + 2) * w]
        log_a = -LRU_C * r * _softplus(-lam_ref[d:d + 1])
        a_ref[d, 0] = jnp.exp(log_a)
        x_ref[d, 0] = jnp.sqrt(1.0 - jnp.exp(2.0 * log_a)) * (ig * u)


def _lru_scan_kernel(a_ref, x_ref, h_ref, st_ref, *, tb):
    d = pl.program_id(0)

    @pl.when(pl.program_id(1) == 0)
    def _():
        st_ref[...] = jnp.zeros(st_ref.shape, F32)

    def body(t, h):
        tt = t + d * (tb - 1 - 2 * t)
        h = a_ref[0, :, pl.ds(tt, 1), :] * h + x_ref[0, :, pl.ds(tt, 1), :]
        h_ref[0, :, pl.ds(tt, 1), :] = h
        return h

    st_ref[...] = lax.fori_loop(0, tb, body, st_ref[...], unroll=8)


def lru_pallas(lx, conv_w, conv_b, w_a, b_a, w_x, b_x, lam):
    B, S, W = lx.shape
    tb = min(LRU_TB, S)
    nt = S // tb
    blk = W // LRU_BLOCKS

    def dense(wb):
        m = jnp.zeros((W, W), F32)
        for n in range(LRU_BLOCKS):
            m = m.at[n * blk:(n + 1) * blk, n * blk:(n + 1) * blk].set(wb[n].astype(F32))
        return m

    wg = jnp.concatenate([dense(w_a[0]), dense(w_x[0]), dense(w_a[1]), dense(w_x[1])], axis=1).astype(BF16)
    bg = jnp.concatenate([b_a[0], b_x[0], b_a[1], b_x[1]]).astype(F32).reshape(1, 4 * W)
    full = lambda shape: pl.BlockSpec(shape, lambda b, i: (0,) * len(shape))
    a, xin = pl.pallas_call(
        functools.partial(_lru_prep_kernel, nt=nt),
        grid=(B, nt),
        in_specs=_halo_specs(tb, W, nt) + [full((CONV_WIDTH, W)), full((1, W)), full((W, 4 * W)), full((1, 4 * W)),
                                           full((2, W))],
        out_specs=[pl.BlockSpec((2, 1, tb, W), lambda b, i: (0, b, i, 0))] * 2,
        out_shape=[jax.ShapeDtypeStruct((2, B, S, W), F32)] * 2,
        compiler_params=_cparams("parallel", "parallel"),
        name="lru_prep",
    )(lx, lx, lx, conv_w.astype(F32), conv_b.astype(F32).reshape(1, W), wg, bg, lam.astype(F32))

    def tmap(d, i):
        return (d, 0, i + d * (nt - 1 - 2 * i), 0)

    return pl.pallas_call(
        functools.partial(_lru_scan_kernel, tb=tb),
        grid=(2, nt),
        in_specs=[pl.BlockSpec((1, B, tb, W), tmap)] * 2,
        out_specs=pl.BlockSpec((1, B, tb, W), tmap),
        out_shape=jax.ShapeDtypeStruct((2, B, S, W), F32),
        scratch_shapes=[pltpu.VMEM((B, 1, W), F32)],
        compiler_params=_cparams("arbitrary", "arbitrary"),
        name="lru_scan",
    )(a, xin)


DIFF_TQ = 256
DIFF_TK = 512
DIFF_POS_LANE = 2 * DIFF_DH
DIFF_POS_PIECES = 3
DIFF_UNROLL = 2
DIFF_UNDERFLOW = 160.0
DIFF_VT_ROWS = DIFF_DV + 8
LOG2E = 1.4426950408889634


def _diff_attn_kernel(slopes_ref, islopes_ref, lam_ref, q_ref, k_ref, vt_ref, g_ref, o_ref,
                      qv_scr, rt_scr, s_scr, mb_scr, ro_scr, p_scr, al_scr, m_scr, acc_scr, kn_scr,
                      *, tq, tk, seq, out_scale):
    h = pl.program_id(1)
    qi = pl.program_id(2)
    slope = slopes_ref[h]
    lam = lam_ref[0]
    nk = seq // tk

    q = q_ref[0, 0]
    lane = lax.broadcasted_iota(jnp.int32, q.shape, 1)
    zero = jnp.zeros_like(q)
    q1 = jnp.where(jnp.abs(2 * lane - (DIFF_DH + DIFF_POS_LANE - 1)) < DIFF_DH, zero, q)
    q2 = jnp.where(lane < DIFF_DH, zero, q)
    qs = jnp.concatenate([q1, q2], axis=0)
    lane2 = lax.broadcasted_iota(jnp.int32, qs.shape, 1)
    col = lax.broadcasted_iota(jnp.int32, (1, 2 * tq), 1)
    qpos = qi * tq + jnp.where(col >= tq, col - tq, col)
    shift_q = slope * qpos.astype(F32)
    qv_scr[0] = qs
    qv_scr[1] = jnp.where(lane2 >= DIFF_POS_LANE, -qs, qs)
    rt_scr[0] = -shift_q
    rt_scr[1] = shift_q

    @pl.when(qi == 0)
    def _():
        def chunk(i, best):
            kc = k_ref[0, 0, pl.ds(pl.multiple_of(i * tk, tk), tk), :].astype(F32)
            lk = lax.broadcasted_iota(jnp.int32, kc.shape, 1)
            k2 = kc * kc
            n1 = jnp.sum(jnp.where(lk < DIFF_DH, k2, 0.0), axis=1, keepdims=True)
            n2 = jnp.sum(jnp.where(jnp.abs(2 * lk - (DIFF_DH + DIFF_POS_LANE - 1)) < DIFF_DH, k2, 0.0),
                         axis=1, keepdims=True)
            return jnp.maximum(best, jnp.max(jnp.maximum(n1, n2), axis=0, keepdims=True))

        best = lax.fori_loop(0, nk, chunk, jnp.zeros((1, 1), F32))
        kn_scr[...] = jnp.broadcast_to(jnp.sqrt(best), kn_scr.shape)

    kd = (qi * tq) // tk
    k0 = pl.multiple_of(kd * tk, tk)
    q_mid = jnp.where(lane2 >= DIFF_POS_LANE, jnp.zeros_like(qs), qs)
    s = _dot_nt(k_ref[0, 0, pl.ds(k0, tk), :], q_mid)
    kpos = k0 + lax.broadcasted_iota(jnp.int32, (tk, 2 * tq), 0)
    s = s - slope * jnp.abs(qpos - kpos).astype(F32)
    m0 = jnp.max(s, axis=0, keepdims=True)
    p = jnp.exp2(s - m0).astype(BF16)
    acc_scr[...] = jnp.dot(vt_ref[0, 0, :, pl.ds(k0, tk)], p, preferred_element_type=F32)
    m_scr[...] = m0

    m_min = jnp.min(m0, axis=1, keepdims=True)
    qf = q.astype(F32)
    sq = qf * qf
    qn1 = jnp.sum(jnp.where(lane < DIFF_DH, sq, 0.0), axis=1, keepdims=True)
    qn2 = jnp.sum(jnp.where(jnp.abs(2 * lane - (DIFF_DH + DIFF_POS_LANE - 1)) < DIFF_DH, sq, 0.0),
                  axis=1, keepdims=True)
    qn = jnp.sqrt(jnp.max(jnp.maximum(qn1, qn2), axis=0, keepdims=True))
    reach = (qn * kn_scr[0:1, 0:1] * 1.001 - m_min + DIFF_UNDERFLOW) * islopes_ref[h]
    q_first = jnp.full((1, 1), qi * tq, jnp.int32).astype(F32)
    kd_f = jnp.full((1, 1), kd, jnp.int32).astype(F32)
    lo_f = jnp.ceil((q_first - (tk - 1) - reach) * (1.0 / tk))
    hi_f = jnp.floor((q_first + (tq - 1) + reach) * (1.0 / tk))
    k_lo = jnp.max(jnp.clip(lo_f, 0.0, kd_f).astype(jnp.int32))
    k_hi = jnp.max(jnp.clip(hi_f, kd_f, float(nk - 1)).astype(jnp.int32))
    n_act = k_hi - k_lo

    def tile_of(t):
        kj = k_lo + t
        return jnp.minimum(kj + jnp.where(kj >= kd, 1, 0), nk - 1)

    def scores(t, slot):
        kj = tile_of(t)
        var = jnp.where(kj > kd, 1, 0)
        kt = k_ref[0, 0, pl.ds(pl.multiple_of(kj * tk, tk), tk), :]
        sc = _dot_nt(kt, qv_scr[var])
        s_scr[slot] = sc
        off = rt_scr[var] + jnp.where(t < n_act, 0.0, NEG_BIG)
        ro_scr[slot] = off
        mb_scr[slot] = jnp.max(sc, axis=0, keepdims=True) + off

    def softmax(slot):
        m_old = m_scr[...]
        m_new = jnp.maximum(m_old, mb_scr[slot])
        al_scr[slot] = jnp.exp2(m_old - m_new)
        p_scr[slot] = jnp.exp2(s_scr[slot] - (m_new - ro_scr[slot])).astype(BF16)
        m_scr[...] = m_new

    def pv(t, slot):
        vt = vt_ref[0, 0, :, pl.ds(pl.multiple_of(tile_of(t) * tk, tk), tk)]
        acc_scr[...] = al_scr[slot] * acc_scr[...] + jnp.dot(vt, p_scr[slot], preferred_element_type=F32)

    scores(0, 0)
    scores(1, 1)
    softmax(0)

    def trip(t, slot):
        pv(t, slot)
        scores(t + 2, slot)
        softmax(1 - slot)

    def body(i, c):
        for u in range(DIFF_UNROLL):
            trip(DIFF_UNROLL * i + u, u % 2)
        return c

    lax.fori_loop(0, (n_act + DIFF_UNROLL - 1) // DIFF_UNROLL, body, 0)

    acc = acc_scr[...]
    o2 = acc[0:DIFF_DV] * (1.0 / acc[DIFF_DV:DIFF_DV + 1])
    o = o2[:, :tq] - lam * o2[:, tq:]
    ms = jnp.mean(o * o, axis=0, keepdims=True)
    o = o * lax.rsqrt(ms + RMS_EPS) * g_ref[...] * out_scale
    o_ref[0, 0] = o.astype(o_ref.dtype)


def diff_attention_pallas(dq, dk, dv, lq1, lk1, lq2, lk2, norm_g, lambda_init):
    B, S, _ = dq.shape
    H = DIFF_HEADS
    tq = min(DIFF_TQ, S)
    tk = min(DIFF_TK, S)
    slopes = jnp.exp2(-8.0 * jnp.arange(1, H + 1, dtype=F32) / H) * LOG2E
    lam = (jnp.exp(jnp.sum(lq1.astype(F32) * lk1.astype(F32)))
           - jnp.exp(jnp.sum(lq2.astype(F32) * lk2.astype(F32))) + lambda_init).reshape(1)
    npos = DIFF_POS_PIECES
    pad = LANES - 2 * DIFF_DH - npos
    q4 = dq.astype(BF16).reshape(B, S, H, 2 * DIFF_DH).transpose(0, 2, 1, 3)
    q_aug = jnp.concatenate([q4, jnp.ones((B, H, S, npos), BF16), jnp.zeros((B, H, S, pad), BF16)], axis=-1)
    k4 = dk.astype(BF16).reshape(B, S, H, 2 * DIFF_DH).transpose(0, 2, 1, 3)
    kbias = slopes[:, None] * jnp.arange(S, dtype=F32)[None, :]
    kpos = jnp.stack(_split_trunc_bf16(kbias, npos), axis=-1)[None]
    k_aug = jnp.concatenate([k4, jnp.broadcast_to(kpos, (B, H, S, npos)), jnp.zeros((B, H, S, pad), BF16)],
                            axis=-1)
    v4 = dv.astype(BF16).reshape(B, S, H, DIFF_DV).transpose(0, 2, 3, 1)
    vt = jnp.concatenate([v4, jnp.ones((B, H, 1, S), BF16),
                          jnp.zeros((B, H, DIFF_VT_ROWS - DIFF_DV - 1, S), BF16)], axis=2)
    g = norm_g.astype(F32).reshape(DIFF_DV, 1)

    kern = functools.partial(_diff_attn_kernel, tq=tq, tk=tk, seq=S, out_scale=1.0 - lambda_init)
    out = pl.pallas_call(
        kern,
        grid=(B, H, S // tq),
        in_specs=[
            pl.BlockSpec(memory_space=pltpu.SMEM),
            pl.BlockSpec(memory_space=pltpu.SMEM),
            pl.BlockSpec(memory_space=pltpu.SMEM),
            pl.BlockSpec((1, 1, tq, LANES), lambda b, h, i: (b, h, i, 0)),
            pl.BlockSpec((1, 1, S, LANES), lambda b, h, i: (b, h, 0, 0)),
            pl.BlockSpec((1, 1, DIFF_VT_ROWS, S), lambda b, h, i: (b, h, 0, 0)),
            pl.BlockSpec((DIFF_DV, 1), lambda b, h, i: (0, 0)),
        ],
        out_specs=pl.BlockSpec((1, 1, DIFF_DV, tq), lambda b, h, i: (b, h, 0, i)),
        out_shape=jax.ShapeDtypeStruct((B, H, DIFF_DV, S), F32),
        scratch_shapes=[
            pltpu.VMEM((2, 2 * tq, LANES), BF16),
            pltpu.VMEM((2, 1, 2 * tq), F32),
            pltpu.VMEM((2, tk, 2 * tq), F32),
            pltpu.VMEM((2, 1, 2 * tq), F32),
            pltpu.VMEM((2, 1, 2 * tq), F32),
            pltpu.VMEM((2, tk, 2 * tq), BF16),
            pltpu.VMEM((2, 1, 2 * tq), F32),
            pltpu.VMEM((1, 2 * tq), F32),
            pltpu.VMEM((DIFF_VT_ROWS, 2 * tq), F32),
            pltpu.VMEM((8, LANES), F32),
        ],
        compiler_params=pltpu.CompilerParams(
            dimension_semantics=("parallel", "parallel", "arbitrary"),
            vmem_limit_bytes=VMEM_LIMIT),
        name="diff_attn",
    )(slopes, 1.0 / slopes, lam, q_aug, k_aug, vt, g)
    return out.transpose(0, 3, 1, 2).reshape(B, S, DIFF_V_W)


MERGE_TM = 256


def _group_rms(o, ones_bd, g, width):
    ss = _dot_split(o * o, ones_bd, n=2)
    return o * lax.rsqrt(ss * (1.0 / width) + RMS_EPS) * g


def _gelu_tanh(x):
    return 0.5 * x * (1.0 + jnp.tanh(0.7978845608028654 * (x + 0.044715 * (x * x * x))))


def _merge_kernel(x_ref, gla_ref, gog_ref, lru_ref, lg_ref, dif_ref, gdn_ref, nz_ref,
                  mg_ref, glag_ref, gdng_ref, ones_ref, mw_ref, mb_ref, up_ref, wo_ref, o_ref):
    x = x_ref[...]
    h = _rms_rows(x, mg_ref[...]).astype(BF16)
    ones_bd = ones_ref[...]
    y_gla = _group_rms(gla_ref[0] + gla_ref[1], ones_bd, glag_ref[...], GLA_DV) * _silu(gog_ref[...].astype(F32))
    y_lru = (lru_ref[0] + lru_ref[1]) * _gelu_tanh(lg_ref[...].astype(F32))
    y_dif = dif_ref[...]
    y_gdn = _group_rms(gdn_ref[0] + gdn_ref[1], ones_bd, gdng_ref[...], GDN_DV) * _silu(nz_ref[...].astype(F32))
    merged = None
    for i, y in enumerate((y_gla, y_lru, y_dif, y_gdn)):
        gate = _sigmoid(jnp.dot(h, mw_ref[i], preferred_element_type=F32) + mb_ref[i])
        term = gate * jnp.dot(y.astype(BF16), up_ref[i], preferred_element_type=F32)
        merged = term if merged is None else merged + term
    o_ref[...] = x + jnp.dot(merged.astype(BF16), wo_ref[...], preferred_element_type=F32)


def merge_pallas(xt, gla_o, gog, lru_h, lg, y_diff, gdn_o, nz, mix_g, gla_g, gdn_g, merge_w, merge_b, branch_up,
                 mix_out):
    T, D = xt.shape
    tm = min(MERGE_TM, T)
    W = BRANCH_WIDTH
    r = np.arange(W) // GLA_DV
    ones_bd = jnp.asarray((r[:, None] == r[None, :]).astype(np.float32)).astype(BF16)
    tok = lambda w: pl.BlockSpec((tm, w), lambda i: (i, 0))
    tok2 = pl.BlockSpec((2, tm, W), lambda i: (0, i, 0))
    const = lambda shape: pl.BlockSpec(shape, lambda i: (0,) * len(shape), pipeline_mode=pl.Buffered(1))
    return pl.pallas_call(
        _merge_kernel,
        grid=(T // tm,),
        in_specs=[tok(D), tok2, tok(W), tok2, tok(W), tok(W), tok2, tok(W),
                  const((1, D)), const((1, W)), const((1, W)), const((W, W)),
                  const((N_BRANCHES, D, D)), const((N_BRANCHES, 1, D)), const((N_BRANCHES, W, D)), const((D, D))],
        out_specs=tok(D),
        out_shape=jax.ShapeDtypeStruct((T, D), F32),
        compiler_params=_cparams("parallel"),
        name="merge",
    )(xt, gla_o, gog, lru_h, lg, y_diff, gdn_o, nz,
      mix_g.reshape(1, D).astype(F32), jnp.tile(gla_g.astype(F32), GLA_HEADS).reshape(1, W),
      jnp.tile(gdn_g.astype(F32), GDN_HEADS).reshape(1, W), ones_bd,
      merge_w.astype(BF16), merge_b.astype(F32).reshape(N_BRANCHES, 1, D), branch_up.astype(BF16),
      mix_out.astype(BF16))


XATTN_TM = 512


def _kv_kernel(m_ref, g_ref, w_ref, o_ref):
    mn = _rms_rows(m_ref[0], g_ref[...]).astype(BF16)
    o_ref[0] = jnp.dot(mn, w_ref[...], preferred_element_type=F32).astype(BF16)


def _xattn_kernel(x_ref, g_ref, wq_ref, kv_ref, wo_ref, o_ref):
    x = x_ref[0]
    h = _rms_rows(x, g_ref[...]).astype(BF16)
    q = jnp.dot(h, wq_ref[...], preferred_element_type=F32).astype(BF16)
    outs = []
    for hh in range(XATTN_HEADS):
        lo = hh * XATTN_DH
        k = kv_ref[0, :, lo:lo + XATTN_DH]
        v = kv_ref[0, :, D_MODEL + lo:D_MODEL + lo + XATTN_DH]
        s = _dot_nt(q[:, lo:lo + XATTN_DH], k) * XATTN_DH ** -0.5
        p = jnp.exp(s - jnp.max(s, axis=-1, keepdims=True))
        p = p / jnp.sum(p, axis=-1, keepdims=True)
        outs.append(jnp.dot(p.astype(BF16), v, preferred_element_type=F32).astype(BF16))
    o = jnp.concatenate(outs, axis=-1)
    o_ref[0] = x + jnp.dot(o, wo_ref[...], preferred_element_type=F32)


def xattn_pallas(x, mem, xg, mg, wq, wkv, wo):
    B, S, D = x.shape
    M = mem.shape[1]
    tm = min(XATTN_TM, S)
    kv = pl.pallas_call(
        _kv_kernel,
        grid=(B,),
        in_specs=[pl.BlockSpec((1, M, D), lambda b: (b, 0, 0)),
                  pl.BlockSpec((1, D), lambda b: (0, 0)),
                  pl.BlockSpec((D, 2 * D), lambda b: (0, 0))],
        out_specs=pl.BlockSpec((1, M, 2 * D), lambda b: (b, 0, 0)),
        out_shape=jax.ShapeDtypeStruct((B, M, 2 * D), BF16),
        compiler_params=_cparams("parallel"),
        name="xattn_kv",
    )(mem, mg.reshape(1, D).astype(F32), wkv.astype(BF16))
    const = lambda shape: pl.BlockSpec(shape, lambda b, i: (0,) * len(shape), pipeline_mode=pl.Buffered(1))
    return pl.pallas_call(
        _xattn_kernel,
        grid=(B, S // tm),
        in_specs=[pl.BlockSpec((1, tm, D), lambda b, i: (b, i, 0)),
                  const((1, D)), const((D, D)),
                  pl.BlockSpec((1, M, 2 * D), lambda b, i: (b, 0, 0)),
                  const((D, D))],
        out_specs=pl.BlockSpec((1, tm, D), lambda b, i: (b, i, 0)),
        out_shape=jax.ShapeDtypeStruct((B, S, D), F32),
        compiler_params=_cparams("parallel", "parallel"),
        name="xattn",
    )(x, xg.reshape(1, D).astype(F32), wq.astype(BF16), kv, wo.astype(BF16))


ROUTE_TM = 512
ROUTE_EXPERT_LANE = 32
MOE_BM = 256
COMBINE_TM = 512


def _router_kernel(x_ref, g_ref, w_ref, b_ref, h_ref, r_ref):
    h = _rms_rows(x_ref[...], g_ref[...])
    h_ref[...] = h
    logits = _dot_f32(h, w_ref[...]) + b_ref[...]
    lane = lax.broadcasted_iota(jnp.int32, logits.shape, 1)
    big = jnp.int32(1 << 20)
    neg = jnp.float32(-jnp.inf)

    def top(vals):
        m = jnp.max(vals, axis=-1, keepdims=True)
        idx = jnp.min(jnp.where(vals == m, lane, big), axis=-1, keepdims=True)
        return m, idx

    is_g = lane < N_GROUPS
    gmax, gidx = top(jnp.where(is_g, logits, neg))
    gsum = jnp.sum(jnp.where(is_g, jnp.exp(logits - gmax), 0.0), axis=-1, keepdims=True)
    g_w = 1.0 / gsum
    lo = ROUTE_EXPERT_LANE + EXPERTS_PER_GROUP * gidx
    in_grp = jnp.abs(2 * (lane - lo) - (EXPERTS_PER_GROUP - 1)) < EXPERTS_PER_GROUP
    el = jnp.where(in_grp, logits, neg)
    e1, i1 = top(el)
    e2, i2 = top(jnp.where(lane == i1, neg, el))
    t = jnp.exp(e2 - e1)
    w1 = g_w / (1.0 + t)
    w2 = g_w * t / (1.0 + t)
    f = lambda v: v.astype(F32)
    r_ref[...] = jnp.where(lane == 0, f(i1 - ROUTE_EXPERT_LANE),
                           jnp.where(lane == 1, f(i2 - ROUTE_EXPERT_LANE),
                                     jnp.where(lane == 2, w1, jnp.where(lane == 3, w2, 0.0))))


def router_pallas(xt, g, w_group, b_group, w_expert, b_expert):
    T, D = xt.shape
    tm = min(ROUTE_TM, T)
    w = jnp.zeros((D, LANES), F32).at[:, :N_GROUPS].set(w_group.astype(F32))
    w = w.at[:, ROUTE_EXPERT_LANE:ROUTE_EXPERT_LANE + N_EXPERTS].set(w_expert.astype(F32))
    b = jnp.zeros((1, LANES), F32).at[0, :N_GROUPS].set(b_group.astype(F32))
    b = b.at[0, ROUTE_EXPERT_LANE:ROUTE_EXPERT_LANE + N_EXPERTS].set(b_expert.astype(F32))
    return pl.pallas_call(
        _router_kernel,
        grid=(T // tm,),
        in_specs=[pl.BlockSpec((tm, D), lambda i: (i, 0)),
                  pl.BlockSpec((1, D), lambda i: (0, 0)),
                  pl.BlockSpec((D, LANES), lambda i: (0, 0)),
                  pl.BlockSpec((1, LANES), lambda i: (0, 0))],
        out_specs=[pl.BlockSpec((tm, D), lambda i: (i, 0)), pl.BlockSpec((tm, LANES), lambda i: (i, 0))],
        out_shape=[jax.ShapeDtypeStruct((T, D), F32), jax.ShapeDtypeStruct((T, LANES), F32)],
        compiler_params=_cparams("parallel"),
        name="moe_router",
    )(xt, g.reshape(1, D).astype(F32), w, b)


def _gather_rows(idx_hbm_row, src_hbm, idx_smem, dst, sem_idx, sem_rows, n):
    cp = pltpu.make_async_copy(idx_hbm_row, idx_smem, sem_idx)
    cp.start()
    cp.wait()

    def row_copy(r):
        return pltpu.make_async_copy(src_hbm.at[pl.ds(idx_smem[r], 1)], dst.at[pl.ds(r, 1)], sem_rows)

    def issue(r, c):
        row_copy(r).start()
        return c

    def drain(r, c):
        row_copy(r).wait()
        return c

    lax.fori_loop(0, n, issue, 0)
    lax.fori_loop(0, n, drain, 0)


def _expert_kernel(be_ref, nu_ref, idx_hbm, h_hbm, w1_ref, w3_ref, w2_ref, y_ref, idx_smem, xbuf, sem_idx, sem_rows,
                   *, bm):
    i = pl.program_id(0)

    @pl.when(i < nu_ref[0])
    def _():
        _gather_rows(idx_hbm.at[i], h_hbm, idx_smem, xbuf, sem_idx, sem_rows, bm)
        xb = xbuf[...].astype(BF16)
        a = jnp.dot(xb, w1_ref[0], preferred_element_type=F32)
        g = jnp.dot(xb, w3_ref[0], preferred_element_type=F32)
        y_ref[...] = jnp.dot((_silu(a) * g).astype(BF16), w2_ref[0], preferred_element_type=F32)

    @pl.when(i >= nu_ref[0])
    def _():
        y_ref[...] = jnp.zeros(y_ref.shape, F32)


def _combine_kernel(x_ref, r_ref, pos_hbm, y_hbm, g_ref, o_ref, idx_smem, ybuf, sem_idx, sem_rows, *, tm, final):
    i = pl.program_id(0)
    _gather_rows(pos_hbm.at[i], y_hbm, idx_smem, ybuf, sem_idx, sem_rows, 2 * tm)
    r = r_ref[...]
    out = x_ref[...] + r[:, 2:3] * ybuf[0:tm] + r[:, 3:4] * ybuf[tm:2 * tm]
    if final:
        out = _rms_rows(out, g_ref[...])
    o_ref[...] = out


def moe_pallas(xt, g, w_group, b_group, w_expert, b_expert, w1, w3, w2, final_g=None):
    T, D = xt.shape
    bm = MOE_BM
    h, route = router_pallas(xt, g, w_group, b_group, w_expert, b_expert)
    eid = route[:, 0:2].astype(jnp.int32).reshape(-1)
    A = T * TOP_K
    n_blk = A // bm + N_EXPERTS
    P = n_blk * bm
    order = jnp.argsort(eid)
    eid_s = eid[order]
    tok_s = order // TOP_K
    counts = jnp.zeros((N_EXPERTS,), jnp.int32).at[eid].add(1)
    padded = (counts + bm - 1) // bm * bm
    pad_end = jnp.cumsum(padded)
    pad_start = pad_end - padded
    seg_start = jnp.cumsum(counts) - counts
    dest = pad_start[eid_s] + jnp.arange(A, dtype=jnp.int32) - seg_start[eid_s]
    row_tok = jnp.zeros((P,), jnp.int32).at[dest].set(tok_s.astype(jnp.int32))
    pos = jnp.zeros((A,), jnp.int32).at[order].set(dest)
    blk_expert = jnp.minimum(
        jnp.searchsorted(pad_end, jnp.arange(n_blk, dtype=jnp.int32) * bm, side='right'), N_EXPERTS - 1
    ).astype(jnp.int32)
    n_used = (pad_end[-1] // bm).astype(jnp.int32).reshape(1)

    y = pl.pallas_call(
        functools.partial(_expert_kernel, bm=bm),
        grid_spec=pltpu.PrefetchScalarGridSpec(
            num_scalar_prefetch=2,
            grid=(n_blk,),
            in_specs=[pl.BlockSpec(memory_space=pl.ANY),
                      pl.BlockSpec(memory_space=pl.ANY),
                      pl.BlockSpec((1, D, D_EXPERT), lambda i, be, nu: (be[i], 0, 0)),
                      pl.BlockSpec((1, D, D_EXPERT), lambda i, be, nu: (be[i], 0, 0)),
                      pl.BlockSpec((1, D_EXPERT, D), lambda i, be, nu: (be[i], 0, 0))],
            out_specs=pl.BlockSpec((bm, D), lambda i, be, nu: (i, 0)),
            scratch_shapes=[pltpu.SMEM((bm,), jnp.int32), pltpu.VMEM((bm, D), F32),
                            pltpu.SemaphoreType.DMA(()), pltpu.SemaphoreType.DMA(())]),
        out_shape=jax.ShapeDtypeStruct((P, D), F32),
        compiler_params=_cparams("arbitrary"),
        name="moe_experts",
    )(blk_expert, n_used, row_tok.reshape(n_blk, bm), h, w1.astype(BF16), w3.astype(BF16), w2.astype(BF16))

    tm = min(COMBINE_TM, T)
    nt = T // tm
    pos_t = pos.reshape(nt, tm, TOP_K).transpose(0, 2, 1).reshape(nt, TOP_K * tm)
    fg = (final_g if final_g is not None else jnp.ones((D,), F32)).reshape(1, D).astype(F32)
    return pl.pallas_call(
        functools.partial(_combine_kernel, tm=tm, final=final_g is not None),
        grid=(nt,),
        in_specs=[pl.BlockSpec((tm, D), lambda i: (i, 0)),
                  pl.BlockSpec((tm, LANES), lambda i: (i, 0)),
                  pl.BlockSpec(memory_space=pl.ANY),
                  pl.BlockSpec(memory_space=pl.ANY),
                  pl.BlockSpec((1, D), lambda i: (0, 0))],
        out_specs=pl.BlockSpec((tm, D), lambda i: (i, 0)),
        out_shape=jax.ShapeDtypeStruct((T, D), F32),
        scratch_shapes=[pltpu.SMEM((TOP_K * tm,), jnp.int32), pltpu.VMEM((TOP_K * tm, D), F32),
                        pltpu.SemaphoreType.DMA(()), pltpu.SemaphoreType.DMA(())],
        compiler_params=_cparams("arbitrary"),
        name="moe_combine",
    )(xt, route, pos_t, y, fg)


PAIR_BM = 256
PAIRS_PER_GROUP = EXPERTS_PER_GROUP * (EXPERTS_PER_GROUP - 1) // 2
N_CLASSES = N_GROUPS * PAIRS_PER_GROUP


def _route_class_kernel(x_ref, g_ref, w_ref, b_ref, r_ref):
    h = _rms_rows(x_ref[...], g_ref[...])
    logits = _dot_f32(h, w_ref[...]) + b_ref[...]
    lane = lax.broadcasted_iota(jnp.int32, logits.shape, 1)
    big = jnp.int32(1 << 20)
    neg = jnp.float32(-jnp.inf)

    def top(vals):
        m = jnp.max(vals, axis=-1, keepdims=True)
        idx = jnp.min(jnp.where(vals == m, lane, big), axis=-1, keepdims=True)
        return m, idx

    _, gidx = top(jnp.where(lane < N_GROUPS, logits, neg))
    lo0 = ROUTE_EXPERT_LANE + EXPERTS_PER_GROUP * gidx
    in_grp = jnp.abs(2 * (lane - lo0) - (EXPERTS_PER_GROUP - 1)) < EXPERTS_PER_GROUP
    el = jnp.where(in_grp, logits, neg)
    _, i1 = top(el)
    _, i2 = top(jnp.where(lane == i1, neg, el))
    a = jnp.minimum(i1, i2) - lo0
    b = jnp.maximum(i1, i2) - lo0
    pair = a * (2 * EXPERTS_PER_GROUP - 1 - a) // 2 + (b - a - 1)
    r_ref[...] = jnp.broadcast_to(gidx * PAIRS_PER_GROUP + pair, r_ref.shape)


def _pair_expert_kernel(lo_ref, hi_ref, nv_ref, idx_hbm, x_hbm, g_ref, wr_ref, br_ref, fg_ref,
                        w1a_ref, w3a_ref, w2a_ref, w1b_ref, w3b_ref, w2b_ref, o_hbm,
                        idx_smem, xbuf, obuf, sem_idx, sem_in, sem_out, *, bm, final):
    i = pl.program_id(0)
    n = nv_ref[i]

    @pl.when(i == 0)
    def _():
        xbuf[...] = jnp.zeros(xbuf.shape, F32)

    @pl.when(n > 0)
    def _():
        cp = pltpu.make_async_copy(idx_hbm.at[i], idx_smem, sem_idx)
        cp.start()
        cp.wait()

        def in_copy(r):
            return pltpu.make_async_copy(x_hbm.at[pl.ds(idx_smem[r], 1)], xbuf.at[pl.ds(r, 1)], sem_in)

        def out_copy(r):
            return pltpu.make_async_copy(obuf.at[pl.ds(r, 1)], o_hbm.at[pl.ds(idx_smem[r], 1)], sem_out)

        def loop(copy_of, wait):
            def body(r, c):
                cpy = copy_of(r)
                cpy.wait() if wait else cpy.start()
                return c
            lax.fori_loop(0, n, body, 0)

        loop(in_copy, False)
        loop(in_copy, True)
        x = xbuf[...]
        h = _rms_rows(x, g_ref[...])
        logits = _dot_f32(h, wr_ref[...]) + br_ref[...]
        lane = lax.broadcasted_iota(jnp.int32, logits.shape, 1)
        lo = lo_ref[i]
        hi = hi_ref[i]
        grp = lo // EXPERTS_PER_GROUP
        is_g = lane < N_GROUPS
        gmax = jnp.max(jnp.where(is_g, logits, -jnp.inf), axis=-1, keepdims=True)
        eg = jnp.exp(logits - gmax)
        g_w = (jnp.sum(jnp.where(lane == grp, eg, 0.0), axis=-1, keepdims=True)
               / jnp.sum(jnp.where(is_g, eg, 0.0), axis=-1, keepdims=True))
        e_lo = jnp.sum(jnp.where(lane == ROUTE_EXPERT_LANE + lo, logits, 0.0), axis=-1, keepdims=True)
        e_hi = jnp.sum(jnp.where(lane == ROUTE_EXPERT_LANE + hi, logits, 0.0), axis=-1, keepdims=True)
        m = jnp.maximum(e_lo, e_hi)
        t_lo = jnp.exp(e_lo - m)
        t_hi = jnp.exp(e_hi - m)
        inv = g_w / (t_lo + t_hi)
        hb = h.astype(BF16)

        def expert(w1, w3, w2):
            a = jnp.dot(hb, w1[0], preferred_element_type=F32)
            b = jnp.dot(hb, w3[0], preferred_element_type=F32)
            return jnp.dot((_silu(a) * b).astype(BF16), w2[0], preferred_element_type=F32)

        y = expert(w1a_ref, w3a_ref, w2a_ref) * (t_lo * inv) + expert(w1b_ref, w3b_ref, w2b_ref) * (t_hi * inv)
        out = x + y
        if final:
            out = _rms_rows(out, fg_ref[...])
        obuf[...] = out
        loop(out_copy, False)
        loop(out_copy, True)


def moe_pair_pallas(xt, g, w_group, b_group, w_expert, b_expert, w1, w3, w2, final_g=None):
    T, D = xt.shape
    bm = PAIR_BM
    tm = min(ROUTE_TM, T)
    wr = jnp.zeros((D, LANES), F32).at[:, :N_GROUPS].set(w_group.astype(F32))
    wr = wr.at[:, ROUTE_EXPERT_LANE:ROUTE_EXPERT_LANE + N_EXPERTS].set(w_expert.astype(F32))
    br = jnp.zeros((1, LANES), F32).at[0, :N_GROUPS].set(b_group.astype(F32))
    br = br.at[0, ROUTE_EXPERT_LANE:ROUTE_EXPERT_LANE + N_EXPERTS].set(b_expert.astype(F32))
    g2 = g.reshape(1, D).astype(F32)
    cls = pl.pallas_call(
        _route_class_kernel,
        grid=(T // tm,),
        in_specs=[pl.BlockSpec((tm, D), lambda i: (i, 0)),
                  pl.BlockSpec((1, D), lambda i: (0, 0)),
                  pl.BlockSpec((D, LANES), lambda i: (0, 0)),
                  pl.BlockSpec((1, LANES), lambda i: (0, 0))],
        out_specs=pl.BlockSpec((tm, LANES), lambda i: (i, 0)),
        out_shape=jax.ShapeDtypeStruct((T, LANES), jnp.int32),
        compiler_params=_cparams("parallel"),
        name="moe_route",
    )(xt, g2, wr, br)[:, 0]

    n_blk = T // bm + N_CLASSES
    order = jnp.argsort(cls).astype(jnp.int32)
    cls_s = cls[order]
    bounds = jnp.searchsorted(cls_s, jnp.arange(N_CLASSES + 1, dtype=jnp.int32), side='left').astype(jnp.int32)
    seg_start = bounds[:-1]
    counts = bounds[1:] - seg_start
    nblk_c = (counts + bm - 1) // bm
    blk_end = jnp.cumsum(nblk_c)
    blk_start = blk_end - nblk_c
    bidx = jnp.arange(n_blk, dtype=jnp.int32)
    blk_cls = jnp.minimum(jnp.searchsorted(blk_end, bidx, side='right'), N_CLASSES - 1).astype(jnp.int32)
    first = seg_start[blk_cls] + (bidx - blk_start[blk_cls]) * bm
    n_valid = jnp.clip(seg_start[blk_cls] + counts[blk_cls] - first, 0, bm)
    n_valid = jnp.where(bidx < blk_end[-1], n_valid, 0).astype(jnp.int32)
    src = jnp.clip(first[:, None] + jnp.arange(bm, dtype=jnp.int32)[None, :], 0, T - 1)
    row_tok = order[src]
    pa, pb = np.triu_indices(EXPERTS_PER_GROUP, k=1)
    grp_of = np.repeat(np.arange(N_GROUPS), PAIRS_PER_GROUP) * EXPERTS_PER_GROUP
    lo_tab = jnp.asarray((grp_of + np.tile(pa, N_GROUPS)).astype(np.int32))
    hi_tab = jnp.asarray((grp_of + np.tile(pb, N_GROUPS)).astype(np.int32))
    blk_lo = lo_tab[blk_cls]
    blk_hi = hi_tab[blk_cls]

    fg = (final_g if final_g is not None else jnp.ones((D,), F32)).reshape(1, D).astype(F32)
    w1b, w3b, w2b = w1.astype(BF16), w3.astype(BF16), w2.astype(BF16)
    const = lambda shape: pl.BlockSpec(shape, lambda i, lo, hi, nv: (0,) * len(shape))
    wspec = lambda shape, which: pl.BlockSpec(
        shape, (lambda i, lo, hi, nv: (lo[i], 0, 0)) if which == 0 else (lambda i, lo, hi, nv: (hi[i], 0, 0)))
    return pl.pallas_call(
        functools.partial(_pair_expert_kernel, bm=bm, final=final_g is not None),
        grid_spec=pltpu.PrefetchScalarGridSpec(
            num_scalar_prefetch=3,
            grid=(n_blk,),
            in_specs=[pl.BlockSpec(memory_space=pl.ANY),
                      pl.BlockSpec(memory_space=pl.ANY),
                      const((1, D)), const((D, LANES)), const((1, LANES)), const((1, D)),
                      wspec((1, D, D_EXPERT), 0), wspec((1, D, D_EXPERT), 0), wspec((1, D_EXPERT, D), 0),
                      wspec((1, D, D_EXPERT), 1), wspec((1, D, D_EXPERT), 1), wspec((1, D_EXPERT, D), 1)],
            out_specs=pl.BlockSpec(memory_space=pl.ANY),
            scratch_shapes=[pltpu.SMEM((bm,), jnp.int32), pltpu.VMEM((bm, D), F32), pltpu.VMEM((bm, D), F32),
                            pltpu.SemaphoreType.DMA(()), pltpu.SemaphoreType.DMA(()), pltpu.SemaphoreType.DMA(())]),
        out_shape=jax.ShapeDtypeStruct((T, D), F32),
        compiler_params=_cparams("arbitrary"),
        name="moe_pair_experts",
    )(blk_lo, blk_hi, n_valid, row_tok, xt, g2, wr, br, fg, w1b, w3b, w2b, w1b, w3b, w2b)


GDN_TB = 512
GDN_INV_PASSES = 1
GDN_UNROLL_B = 8


def _gdn_prep_kernel(prev_ref, cur_ref, next_ref, ba_ref, cw_ref, ones_ref, alog_ref, dt_ref, qkv_ref, bg_ref, *, nt):
    qkv = _silu(_conv4(prev_ref, cur_ref, next_ref, cw_ref[...], nt))
    ones_bd = ones_ref[...]
    w = GDN_QK_W
    q = qkv[:, 0:w]
    k = qkv[:, w:2 * w]
    qn = q * lax.rsqrt(_dot_split(q * q, ones_bd, n=2) + 1e-6) * GDN_DK ** -0.5
    kn = k * lax.rsqrt(_dot_split(k * k, ones_bd, n=2) + 1e-6)
    qkv_ref[0] = jnp.concatenate([qn, kn, qkv[:, 2 * w:]], axis=-1)
    ba = ba_ref[0]
    beta = _sigmoid(ba)
    log_a = -jnp.exp(alog_ref[...]) * _softplus(ba + dt_ref[...])
    lane = lax.broadcasted_iota(jnp.int32, ba.shape, 1)
    h = GDN_HEADS
    for d in range(2):
        b_d = pltpu.roll(beta, (LANES - d * h) % LANES, 1)
        a_d = pltpu.roll(log_a, (LANES - (2 * h + d * h) + h) % LANES, 1)
        bg_ref[d, 0] = jnp.where(lane < h, b_d, jnp.where(lane < 2 * h, a_d, 0.0))


def _gdn_chunk_kernel(qkv_ref, bg_ref, tri_ref, tribd_ref, o_ref, st_ref, *, nb):
    c = GDN_CHUNK
    H = GDN_HEADS
    n = H * c

    @pl.when(pl.program_id(1) == 0)
    def _():
        st_ref[...] = jnp.zeros(st_ref.shape, F32)

    tri_bf = tri_ref[0].astype(BF16)
    incl = tribd_ref[0]
    ri = lax.broadcasted_iota(jnp.int32, (n, n), 0)
    ci = lax.broadcasted_iota(jnp.int32, (n, n), 1)
    eye = jnp.where(ri == ci, 1.0, 0.0)
    strict = incl - eye
    bdmask = jnp.where(ri // c == ci // c, 1.0, 0.0)

    def stack(x):
        w = x.shape[1] // H
        return jnp.concatenate([x[:, hh * w:(hh + 1) * w] for hh in range(H)], axis=0)

    def col(x, lane0):
        return jnp.concatenate([x[:, lane0 + hh:lane0 + hh + 1] for hh in range(H)], axis=0)

    def bd(x_st):
        return jnp.concatenate([x_st] * H, axis=1) * bdmask

    def mm(a, b, passes, dot=_dot_nn):
        if passes == 1:
            return dot(a.astype(BF16), b.astype(BF16))
        return _dot_f32(a, b, dot=dot)

    def one_batch(b):
        qkv = qkv_ref[b]
        bg = bg_ref[0, b]
        q_st = stack(qkv[:, 0:GDN_QK_W])
        k_st = stack(qkv[:, GDN_QK_W:2 * GDN_QK_W])
        v_st = stack(qkv[:, 2 * GDN_QK_W:])
        acc = None
        for p in _split_bf16(bg, 3):
            t = _dot_nn(tri_bf, p)
            acc = t if acc is None else acc + t
        gam = acc
        tot = jnp.sum(bg, axis=0, keepdims=True)
        g_col = col(gam, H)
        b_col = col(bg, 0)
        gam_t = jnp.concatenate([gam, jnp.zeros_like(gam)], axis=0).T
        g_row = jnp.concatenate([gam_t[H + hh:H + hh + 1, 0:c] for hh in range(H)], axis=1)
        end_col = jnp.concatenate([jnp.broadcast_to(tot[:, H + hh:H + hh + 1], (c, 1)) for hh in range(H)], axis=0)
        exp_g = jnp.exp(g_col)
        decay = jnp.exp(jnp.minimum(g_col - g_row, 0.0)) * incl
        k_bf = k_st.astype(BF16)
        kk = _dot_nt(k_bf, k_bf)
        a_mat = strict * b_col * kk * decay
        p_inv = eye - a_mat
        x_pow = a_mat
        for _ in range(int(math.log2(c)) - 1):
            x_pow = mm(x_pow, x_pow, GDN_INV_PASSES)
            p_inv = p_inv + mm(p_inv, x_pow, GDN_INV_PASSES)
        rhs = jnp.concatenate([v_st * b_col, k_st * (b_col * exp_g)], axis=1)
        p_bf = p_inv.astype(BF16)
        sol = _dot_nn(p_bf, rhs.astype(BF16))
        resid = rhs - sol - _dot_f32(a_mat, sol)
        sol = sol + _dot_nn(p_bf, resid.astype(BF16))
        u_st = sol[:, 0:c]
        kc_bd = bd(sol[:, c:2 * c])
        qk = _dot_nt(q_st.astype(BF16), k_bf) * decay
        qd_bd = bd(q_st * exp_g)
        kd_bd = bd(k_st * jnp.exp(end_col - g_col))
        st = st_ref[b]
        st_bf = st.astype(BF16)
        v_new = u_st - _dot_nn(kc_bd.astype(BF16), st_bf)
        v_new_bf = v_new.astype(BF16)
        o_st = _dot_nn(qd_bd.astype(BF16), st_bf) + _dot_nn(qk.astype(BF16), v_new_bf)
        st_ref[b] = jnp.exp(end_col) * st + _dot_tn(kd_bd.astype(BF16), v_new_bf)
        o_ref[0, b] = jnp.concatenate([o_st[hh * c:(hh + 1) * c] for hh in range(H)], axis=1)

    def pair(i, carry):
        for j in range(GDN_UNROLL_B):
            one_batch(i * GDN_UNROLL_B + j)
        return carry

    lax.fori_loop(0, nb // GDN_UNROLL_B, pair, 0)


def _gdn_chunk_compact_kernel(qkv_ref, bg_ref, tri_ref, tril_ref, sel_ref, o_ref, st_ref, *, nb):
    c = GDN_CHUNK
    H = GDN_HEADS
    n = H * c

    @pl.when(pl.program_id(1) == 0)
    def _():
        st_ref[...] = jnp.zeros(st_ref.shape, F32)

    tri_bf = tri_ref[0].astype(BF16)
    incl = tril_ref[0]
    ri = lax.broadcasted_iota(jnp.int32, (c, n), 0)
    ci = lax.broadcasted_iota(jnp.int32, (c, n), 1)
    eye = jnp.where(ri == ci % c, 1.0, 0.0)
    strict = incl - eye
    rb = lax.broadcasted_iota(jnp.int32, (n, n), 0) // c
    cb = lax.broadcasted_iota(jnp.int32, (n, n), 1) // c
    bdmask = rb == cb
    sel_g = sel_ref[0]
    sel_b = sel_ref[1]

    def bd(x):
        return jnp.where(bdmask, jnp.concatenate([x] * H, axis=0), 0.0)

    def one_batch(b):
        qkv = qkv_ref[b]
        bg = bg_ref[0, b]
        q = qkv[:, 0:n]
        k = qkv[:, n:2 * n]
        v = qkv[:, 2 * n:]
        gam = _dot_split(bg, None, dot=lambda p, _: _dot_nn(tri_bf, p))
        k_bf = k.astype(BF16)
        kq = _dot_nt(jnp.concatenate([k_bf, q.astype(BF16)], axis=0), bd(k).astype(BF16))
        bexp = _dot_split(bg, sel_b)
        yield
        tot = jnp.sum(bg, axis=0, keepdims=True)
        gt = jnp.concatenate([gam, jnp.broadcast_to(tot, (8, LANES))], axis=0)
        ge = _dot_split(gt, sel_g)
        yield
        gexp = ge[0:c]
        end_row = ge[c:c + 1]
        gam_t = jnp.concatenate([gam, jnp.zeros_like(gam)], axis=0).T
        g_row = jnp.concatenate([gam_t[H + hh:H + hh + 1, 0:c] for hh in range(H)], axis=1)
        decay = jnp.exp(jnp.minimum(gexp - g_row, 0.0)) * incl
        exp_g = jnp.exp(gexp)
        a_c = strict * bexp * kq[0:c] * decay
        qk = kq[c:2 * c] * decay
        p_c = eye - a_c
        x_c = _dot_nn(a_c.astype(BF16), bd(a_c).astype(BF16))
        yield
        n_sq = int(math.log2(c)) - 1
        for it in range(n_sq):
            r_bd = bd(x_c).astype(BF16)
            if it + 1 < n_sq:
                both = _dot_nn(jnp.concatenate([x_c, p_c], axis=0).astype(BF16), r_bd)
                yield
                x_c = both[0:c]
                p_c = p_c + both[c:2 * c]
            else:
                dp = _dot_nn(p_c.astype(BF16), r_bd)
                yield
                p_c = p_c + dp
        ap = _dot_f32(a_c, bd(p_c))
        yield
        resid = eye - p_c - ap
        p_bf = p_c.astype(BF16)
        dp = _dot_nn(p_bf, bd(resid).astype(BF16))
        yield
        p_c = p_c + dp
        p_bf = p_c.astype(BF16)
        u = _dot_nn(p_bf, bd(v * bexp).astype(BF16))
        kc = _dot_nn(p_bf, bd(k * (bexp * exp_g)).astype(BF16))
        yield
        st = st_ref[b]
        st_bf = st.astype(BF16)
        sq = _dot_nn(jnp.concatenate([kc, q * exp_g], axis=0).astype(BF16), st_bf)
        yield
        v_new = u - sq[0:c]
        o_ref[0, b] = sq[c:2 * c] + _dot_nn(qk.astype(BF16), bd(v_new).astype(BF16))
        kd = (k * jnp.exp(end_row - gexp)).astype(BF16)
        upd = _dot_tn(kd, v_new.astype(BF16))
        st_ref[b] = st * jnp.exp(end_row) + jnp.where(bdmask, upd, 0.0)

    n_side = math.gcd(nb, GDN_UNROLL_B)

    def group(i, carry):
        chains = [one_batch(i * n_side + j) for j in range(n_side)]
        live = True
        while live:
            live = False
            for ch in chains:
                try:
                    next(ch)
                    live = True
                except StopIteration:
                    pass
        return carry

    if nb == n_side:
        group(0, 0)
    else:
        lax.fori_loop(0, nb // n_side, group, 0)


def gdn_pallas(nqkv, nba, conv_w, a_log, dt_bias):
    B, S, W = nqkv.shape
    tb = min(GDN_TB, S)
    nt = S // tb
    c = GDN_CHUNK
    nc = S // c
    H = GDN_HEADS
    r = np.arange(GDN_QK_W) // GDN_DK
    ones_bd = jnp.asarray((r[:, None] == r[None, :]).astype(np.float32)).astype(BF16)
    alog = jnp.zeros((1, LANES), F32).at[0, 2 * H:4 * H].set(a_log.astype(F32).reshape(-1))
    dt = jnp.zeros((1, LANES), F32).at[0, 2 * H:4 * H].set(dt_bias.astype(F32).reshape(-1))
    full = lambda shape: pl.BlockSpec(shape, lambda b, i: (0,) * len(shape))
    qkvn, bg = pl.pallas_call(
        functools.partial(_gdn_prep_kernel, nt=nt),
        grid=(B, nt),
        in_specs=_halo_specs(tb, W, nt) + [pl.BlockSpec((1, tb, LANES), lambda b, i: (b, i, 0)),
                                           full((CONV_WIDTH, W)), full((GDN_QK_W, GDN_QK_W)),
                                           full((1, LANES)), full((1, LANES))],
        out_specs=[pl.BlockSpec((1, tb, W), lambda b, i: (b, i, 0)),
                   pl.BlockSpec((2, 1, tb, LANES), lambda b, i: (0, b, i, 0))],
        out_shape=[jax.ShapeDtypeStruct((B, S, W), F32), jax.ShapeDtypeStruct((2, B, S, LANES), F32)],
        compiler_params=_cparams("parallel", "parallel"),
        name="gdn_prep",
    )(nqkv, nqkv, nqkv, nba, conv_w.astype(F32), ones_bd, alog, dt)

    t = np.arange(c)
    tri_np = np.stack([t[:, None] >= t[None, :], t[:, None] <= t[None, :]]).astype(np.float32)
    tri = jnp.asarray(tri_np)
    tri_lanes = jnp.asarray(np.tile(tri_np, (1, 1, H)))
    lane = np.arange(LANES)[:, None]
    head = (np.arange(H * c) // c)[None, :]
    sel = jnp.asarray(np.stack([lane == H + head, lane == head]).astype(np.float32)).astype(BF16)

    def tmap(d, i):
        return i + d * (nc - 1 - 2 * i)

    return pl.pallas_call(
        functools.partial(_gdn_chunk_compact_kernel, nb=B),
        grid=(2, nc),
        in_specs=[pl.BlockSpec((B, c, W), lambda d, i: (0, tmap(d, i), 0)),
                  pl.BlockSpec((1, B, c, LANES), lambda d, i: (d, 0, tmap(d, i), 0)),
                  pl.BlockSpec((1, c, c), lambda d, i: (d, 0, 0)),
                  pl.BlockSpec((1, c, H * c), lambda d, i: (d, 0, 0)),
                  pl.BlockSpec((2, LANES, H * c), lambda d, i: (0, 0, 0))],
        out_specs=pl.BlockSpec((1, B, c, GDN_V_W), lambda d, i: (d, 0, tmap(d, i), 0)),
        out_shape=jax.ShapeDtypeStruct((2, B, S, GDN_V_W), F32),
        scratch_shapes=[pltpu.VMEM((B, H * GDN_DK, H * GDN_DV), F32)],
        compiler_params=_cparams("arbitrary", "arbitrary"),
        name="gdn_chunk",
    )(qkvn, bg, tri, tri_lanes, sel)


def gdn_scan(q, k, v, beta, log_a):
    b_, s_, h_, dk = q.shape
    dv = v.shape[-1]
    c = GDN_CHUNK
    qc, kc, vc, bc, gc = (_to_chunks(t, c) for t in (q, k, v, beta, log_a))
    gam = jnp.cumsum(gc, axis=-1)
    diff = gam[..., :, None] - gam[..., None, :]
    incl = jnp.tril(jnp.ones((c, c), dtype=bool))
    strict = jnp.tril(jnp.ones((c, c), dtype=bool), k=-1)
    decay = jnp.exp(jnp.where(incl, diff, -jnp.inf))
    kk = jnp.einsum('bhntd,bhnsd->bhnts', kc, kc)
    a_mat = jnp.where(strict, bc[..., None] * kk * decay, 0.0) + jnp.eye(c, dtype=jnp.float32)
    rhs = jnp.concatenate([vc * bc[..., None], kc * (bc * jnp.exp(gam))[..., None]], axis=-1)
    sol = lax.linalg.triangular_solve(a_mat, rhs, left_side=True, lower=True, unit_diagonal=True)
    u_val, k_cum = sol[..., :dv], sol[..., dv:]
    qk = jnp.einsum('bhntd,bhnsd->bhnts', qc, kc) * decay
    q_dec = qc * jnp.exp(gam)[..., None]
    k_dec = kc * jnp.exp(gam[..., -1:] - gam)[..., None]
    c_dec = jnp.exp(gam[..., -1])

    def step(state, inp):
        u_i, kc_i, qk_i, qd_i, kd_i, cd_i = inp
        v_new = u_i - jnp.einsum('bhtd,bhde->bhte', kc_i, state)
        o = jnp.einsum('bhtd,bhde->bhte', qd_i, state) + jnp.einsum('bhts,bhse->bhte', qk_i, v_new)
        state = cd_i[..., None, None] * state + jnp.einsum('bhsd,bhse->bhde', kd_i, v_new)
        return state, o

    xs = tuple(jnp.moveaxis(t, 2, 0) for t in (u_val, k_cum, qk, q_dec, k_dec, c_dec))
    _, o = lax.scan(step, jnp.zeros((b_, h_, dk, dv), jnp.float32), xs)
    return _from_chunks(o)


def gdn_mixer(qkv, z, ba, conv_w, a_log, dt_bias, norm_g):
    B, S, _ = qkv.shape
    f32 = jnp.float32
    qkv = jax.nn.silu(dwconv_centred(qkv, conv_w)).astype(f32)
    q, k, v = jnp.split(qkv, [GDN_QK_W, 2 * GDN_QK_W], axis=-1)
    q = l2norm(q.reshape(B, S, GDN_HEADS, GDN_DK)) * GDN_DK ** -0.5
    k = l2norm(k.reshape(B, S, GDN_HEADS, GDN_DK))
    v = v.reshape(B, S, GDN_HEADS, GDN_DV)
    ba = ba.astype(f32).reshape(B, S, 4, GDN_HEADS)
    beta = jax.nn.sigmoid(ba[:, :, :2])
    log_a = -jnp.exp(a_log.astype(f32)) * jax.nn.softplus(ba[:, :, 2:] + dt_bias.astype(f32))
    o_f = gdn_scan(q, k, v, beta[:, :, 0], log_a[:, :, 0])
    o_b = _flip(gdn_scan(_flip(q), _flip(k), _flip(v), _flip(beta[:, :, 1]), _flip(log_a[:, :, 1])))
    o = rmsnorm(o_f + o_b, norm_g) * jax.nn.silu(z.astype(f32)).reshape(B, S, GDN_HEADS, GDN_DV)
    return o.reshape(B, S, GDN_V_W).astype(z.dtype)


def gated_merge(h, branches, merge_w, merge_b, branch_up):
    out = None
    for i, y in enumerate(branches):
        term = jax.nn.sigmoid(h @ merge_w[i] + merge_b[i]) * (y @ branch_up[i])
        out = term if out is None else out + term
    return out


def memory_cross_attention(h, mem_n, w_q, w_kv, w_o):
    B, S, D = h.shape
    M = mem_n.shape[1]
    q = (h @ w_q).reshape(B, S, XATTN_HEADS, XATTN_DH)
    kv = (mem_n @ w_kv).reshape(B, M, 2, XATTN_HEADS, XATTN_DH)
    k, v = kv[:, :, 0], kv[:, :, 1]
    s = jnp.einsum('bshd,bmhd->bhsm', q, k).astype(jnp.float32) * XATTN_DH ** -0.5
    p = jax.nn.softmax(s, axis=-1)
    o = jnp.einsum('bhsm,bmhd->bshd', p.astype(v.dtype), v).reshape(B, S, D)
    return o @ w_o


def hier_moe(h, w_group, b_group, w_expert, b_expert, w1, w3, w2):
    B, S, D = h.shape
    T = B * S
    f32 = jnp.float32
    ht = h.reshape(T, D)
    g_logits = (ht @ w_group).astype(f32) + b_group.astype(f32)
    g_prob = jax.nn.softmax(g_logits, axis=-1)
    _, g_idx = lax.top_k(g_logits, 1)
    g_w = jnp.take_along_axis(g_prob, g_idx, axis=1)
    e_logits = ((ht @ w_expert).astype(f32) + b_expert.astype(f32)).reshape(T, N_GROUPS, EXPERTS_PER_GROUP)
    e_logits = jnp.take_along_axis(
        e_logits, jnp.broadcast_to(g_idx[:, :, None], (T, 1, EXPERTS_PER_GROUP)), axis=1)[:, 0]
    e_top, e_idx = lax.top_k(e_logits, TOP_K)
    gate = jax.nn.softmax(e_top, axis=-1) * g_w
    eid = (g_idx * EXPERTS_PER_GROUP + e_idx).reshape(-1)
    tok = jnp.repeat(jnp.arange(T, dtype=jnp.int32), TOP_K)
    wts = gate.reshape(-1)
    A = T * TOP_K
    n_blk = -(-A // MOE_BLOCK) + N_EXPERTS
    P = n_blk * MOE_BLOCK
    order = jnp.argsort(eid)
    eid_s, tok_s, w_s = eid[order], tok[order], wts[order]
    counts = jnp.zeros((N_EXPERTS,), jnp.int32).at[eid].add(1)
    padded = (counts + MOE_BLOCK - 1) // MOE_BLOCK * MOE_BLOCK
    pad_end = jnp.cumsum(padded)
    pad_start = pad_end - padded
    seg_start = jnp.cumsum(counts) - counts
    dest = pad_start[eid_s] + jnp.arange(A, dtype=jnp.int32) - seg_start[eid_s]
    row_tok = jnp.full((P,), T, jnp.int32).at[dest].set(tok_s)
    row_w = jnp.zeros((P,), f32).at[dest].set(w_s)
    blk_expert = jnp.minimum(
        jnp.searchsorted(pad_end, jnp.arange(n_blk, dtype=jnp.int32) * MOE_BLOCK, side='right'),
        N_EXPERTS - 1)
    h_pad = jnp.concatenate([ht, jnp.zeros((1, D), ht.dtype)], axis=0)
    xb = h_pad[row_tok].reshape(n_blk, MOE_BLOCK, D)

    def expert_block(args):
        xi, e = args
        return (jax.nn.silu(xi @ w1[e]) * (xi @ w3[e])) @ w2[e]

    yb = lax.map(expert_block, (xb, blk_expert)).reshape(P, D)
    out = jnp.zeros((T + 1, D), h.dtype).at[row_tok].add(yb * row_w[:, None].astype(yb.dtype))
    return out[:T].reshape(B, S, D)


def kernel(x, mem, mix_norm, w_in, gla_lr_up, gla_lr_bias, gla_norm, lru_conv_w, lru_conv_b, lru_w_a, lru_b_a, lru_w_x, lru_b_x, lru_lambda, diff_lq1, diff_lk1, diff_lq2, diff_lk2, diff_norm, gdn_conv_w, gdn_a_log, gdn_dt_bias, gdn_norm, merge_w, merge_b, branch_up, mix_out, xattn_norm, mem_norm, xattn_wq, xattn_wkv, xattn_wo, moe_norm, moe_w_group, moe_b_group, moe_w_expert, moe_b_expert, moe_w1, moe_w3, moe_w2, final_norm):
    B, S, D = x.shape
    T = B * S
    xt = x.reshape(T, D)
    for l in range(DEPTH):
        gla_in, gog, lx, lg, dq, dk, dv, nqkv, nz, nba = inproj_pallas(xt, mix_norm[l], w_in[l])
        gla_o = gla_pallas(gla_in.reshape(B, S, -1), gla_lr_up[l], gla_lr_bias[l])
        lru_h = lru_pallas(lx.reshape(B, S, -1), lru_conv_w[l], lru_conv_b[l], lru_w_a[l], lru_b_a[l],
                           lru_w_x[l], lru_b_x[l], lru_lambda[l])
        y_diff = diff_attention_pallas(dq.reshape(B, S, -1), dk.reshape(B, S, -1), dv.reshape(B, S, -1),
                                       diff_lq1[l], diff_lk1[l], diff_lq2[l], diff_lk2[l], diff_norm[l],
                                       0.8 - 0.6 * math.exp(-0.3 * l))
        gdn_o = gdn_pallas(nqkv.reshape(B, S, -1), nba.reshape(B, S, -1), gdn_conv_w[l], gdn_a_log[l],
                           gdn_dt_bias[l])
        xt = merge_pallas(xt, gla_o.reshape(2, T, -1), gog, lru_h.reshape(2, T, -1), lg, y_diff.reshape(T, -1),
                          gdn_o.reshape(2, T, -1), nz, mix_norm[l], gla_norm[l], gdn_norm[l],
                          merge_w[l], merge_b[l], branch_up[l], mix_out[l])
        xt = xattn_pallas(xt.reshape(B, S, D), mem, xattn_norm[l], mem_norm[l], xattn_wq[l], xattn_wkv[l],
                          xattn_wo[l]).reshape(T, D)
        xt = moe_pair_pallas(xt, moe_norm[l], moe_w_group[l], moe_b_group[l], moe_w_expert[l], moe_b_expert[l],
                             moe_w1[l], moe_w3[l], moe_w2[l], final_g=final_norm if l == DEPTH - 1 else None)
    return xt.reshape(B, S, D)
```

```python
import functools
import math

import jax
import jax.numpy as jnp
import numpy as np
from jax import lax
from jax.experimental import pallas as pl
from jax.experimental.pallas import tpu as pltpu

D_MODEL = 1024
DEPTH = 2
N_BRANCHES = 4
BRANCH_WIDTH = D_MODEL // 4
RMS_EPS = 1e-6
CONV_WIDTH = 4

GLA_HEADS = 4
GLA_DV = BRANCH_WIDTH // GLA_HEADS
GLA_DK = GLA_DV // 2
GLA_RANK = 16
GLA_GATE_NORM = 16.0
GLA_CHUNK = 64
GLA_QK_W = GLA_HEADS * GLA_DK
GLA_V_W = GLA_HEADS * GLA_DV

LRU_WIDTH = BRANCH_WIDTH
LRU_BLOCKS = 4
LRU_C = 8.0

DIFF_HEADS = 4
DIFF_DV = BRANCH_WIDTH // DIFF_HEADS
DIFF_DH = DIFF_DV // 2
DIFF_QK_W = DIFF_HEADS * 2 * DIFF_DH
DIFF_V_W = DIFF_HEADS * DIFF_DV

GDN_HEADS = 4
GDN_DK = BRANCH_WIDTH // GDN_HEADS
GDN_DV = BRANCH_WIDTH // GDN_HEADS
GDN_CHUNK = 64
GDN_QK_W = GDN_HEADS * GDN_DK
GDN_V_W = GDN_HEADS * GDN_DV

IN_SPLITS = (GLA_QK_W, GLA_QK_W, GLA_V_W, GLA_V_W, 2 * GLA_RANK,
             LRU_WIDTH, LRU_WIDTH,
             DIFF_QK_W, DIFF_QK_W, DIFF_V_W,
             2 * GDN_QK_W + GDN_V_W, GDN_V_W, 4 * GDN_HEADS)

XATTN_HEADS = 4
XATTN_DH = D_MODEL // XATTN_HEADS

N_GROUPS = 4
EXPERTS_PER_GROUP = 8
N_EXPERTS = N_GROUPS * EXPERTS_PER_GROUP
TOP_K = 2
D_EXPERT = D_MODEL // 2
MOE_BLOCK = 128

LANES = 128
VMEM_LIMIT = 56 * 1024 * 1024

F32 = jnp.float32
BF16 = jnp.bfloat16
NEG_BIG = -1e30


def rmsnorm(x, g):
    xf = x.astype(jnp.float32)
    y = xf * lax.rsqrt(jnp.mean(xf * xf, axis=-1, keepdims=True) + RMS_EPS)
    return (y * g.astype(jnp.float32)).astype(x.dtype)


def l2norm(t):
    return t * lax.rsqrt(jnp.sum(t * t, axis=-1, keepdims=True) + 1e-6)


def _flip(t):
    return jnp.flip(t, axis=1)


def dwconv_centred(x, w):
    k = w.shape[0]
    return lax.conv_general_dilated(
        x, w[:, None, :].astype(x.dtype), window_strides=(1,),
        padding=[(k // 2, k - 1 - k // 2)],
        dimension_numbers=('NWC', 'WIO', 'NWC'),
        feature_group_count=x.shape[-1])


def _to_chunks(t, chunk):
    b, s, h = t.shape[:3]
    t = t.reshape((b, s // chunk, chunk, h) + t.shape[3:])
    return jnp.moveaxis(t, 3, 1)


def _from_chunks(o):
    nc, b, h, c, d = o.shape
    return o.transpose(1, 0, 3, 2, 4).reshape(b, nc * c, h, d)


def _linrec_combine(c1, c2):
    a1, b1 = c1
    a2, b2 = c2
    return a1 * a2, a2 * b1 + b2


def gla_scan(q, k, v, log_f):
    b_, s_, h_, dk = q.shape
    dv = v.shape[-1]
    c = GLA_CHUNK
    incl = jnp.tril(jnp.ones((c, c), dtype=bool))[:, :, None]

    def step(state, inp):
        q_i, k_i, v_i, g_i = inp
        bcum = jnp.cumsum(g_i, axis=2)
        decay = jnp.exp(jnp.where(incl, bcum[:, :, :, None, :] - bcum[:, :, None, :, :], -jnp.inf))
        scores = jnp.einsum('bhtd,bhsd,bhtsd->bhts', q_i, k_i, decay)
        o = (jnp.einsum('bhts,bhse->bhte', scores, v_i)
             + jnp.einsum('bhtd,bhde->bhte', q_i * jnp.exp(bcum), state))
        b_end = bcum[:, :, -1:, :]
        state = (jnp.exp(b_end)[:, :, 0, :, None] * state
                 + jnp.einsum('bhsd,bhse->bhde', k_i * jnp.exp(b_end - bcum), v_i))
        return state, o

    xs = tuple(jnp.moveaxis(_to_chunks(t, c), 2, 0) for t in (q, k, v, log_f))
    _, o = lax.scan(step, jnp.zeros((b_, h_, dk, dv), jnp.float32), xs)
    return _from_chunks(o)


def gla_mixer(q, k, v, og, lr, lr_up, lr_bias, norm_g):
    B, S, _ = q.shape
    f32 = jnp.float32
    qh = q.astype(f32).reshape(B, S, GLA_HEADS, GLA_DK) * GLA_DK ** -0.5
    kh = k.astype(f32).reshape(B, S, GLA_HEADS, GLA_DK)
    vh = v.astype(f32).reshape(B, S, GLA_HEADS, GLA_DV)
    lr = lr.astype(f32).reshape(B, S, 2, GLA_RANK)
    z = jnp.einsum('bsdr,drk->bsdk', lr, lr_up.astype(f32)) + lr_bias.astype(f32)
    log_f = (jax.nn.log_sigmoid(z) / GLA_GATE_NORM).reshape(B, S, 2, GLA_HEADS, GLA_DK)
    o_f = gla_scan(qh, kh, vh, log_f[:, :, 0])
    o_b = _flip(gla_scan(_flip(qh), _flip(kh), _flip(vh), _flip(log_f[:, :, 1])))
    o = rmsnorm(o_f + o_b, norm_g) * jax.nn.silu(og.astype(f32)).reshape(B, S, GLA_HEADS, GLA_DV)
    return o.reshape(B, S, GLA_V_W).astype(q.dtype)


def rglru_mixer(xb, gb, conv_w, conv_b, w_a, b_a, w_x, b_x, lam):
    B, S, W = xb.shape
    f32 = jnp.float32
    u = (dwconv_centred(xb, conv_w) + conv_b).astype(f32)
    ub = u.reshape(B, S, LRU_BLOCKS, W // LRU_BLOCKS)

    def gate(w, b):
        return jax.nn.sigmoid(jnp.einsum('bsni,nio->bsno', ub, w.astype(f32)).reshape(B, S, W) + b.astype(f32))

    def direction(d, reverse):
        r = gate(w_a[d], b_a[d])
        i = gate(w_x[d], b_x[d])
        log_a = -LRU_C * r * jax.nn.softplus(-lam[d].astype(f32))
        a = jnp.exp(log_a)
        xin = jnp.sqrt(-jnp.expm1(2.0 * log_a)) * (i * u)
        _, hs = lax.associative_scan(_linrec_combine, (a, xin), axis=1, reverse=reverse)
        return hs

    hsum = direction(0, False) + direction(1, True)
    return (hsum * jax.nn.gelu(gb.astype(f32))).astype(xb.dtype)


def _cparams(*sem):
    return pltpu.CompilerParams(dimension_semantics=sem, vmem_limit_bytes=VMEM_LIMIT)


def _split_bf16(a, n):
    parts, r = [], a
    for i in range(n):
        p = r.astype(BF16)
        parts.append(p)
        if i + 1 < n:
            r = r - p.astype(F32)
    return parts


def _split_trunc_bf16(a, n):
    parts, r = [], a
    for i in range(n):
        bits = lax.bitcast_convert_type(r, jnp.uint32) & jnp.uint32(0xFFFF0000)
        p = lax.bitcast_convert_type(bits, F32)
        parts.append(p.astype(BF16))
        if i + 1 < n:
            r = r - p
    return parts


def _dot_nn(a, b):
    return jnp.dot(a, b, preferred_element_type=F32)


def _dot_nt(a, b):
    return lax.dot_general(a, b, (((1,), (1,)), ((), ())), preferred_element_type=F32)


def _dot_tn(a, b):
    return lax.dot_general(a, b, (((0,), (0,)), ((), ())), preferred_element_type=F32)


def _dot_split(a, b_exact, n=3, dot=_dot_nn):
    acc = None
    for p in _split_bf16(a, n):
        t = dot(p, b_exact)
        acc = t if acc is None else acc + t
    return acc


def _dot_f32(a, b, dot=_dot_nn):
    a_hi, a_lo = _split_bf16(a, 2)
    b_hi, b_lo = _split_bf16(b, 2)
    return dot(a_hi, b_hi) + (dot(a_hi, b_lo) + dot(a_lo, b_hi))


def _rms_rows(x, g):
    return x * lax.rsqrt(jnp.mean(x * x, axis=-1, keepdims=True) + RMS_EPS) * g


def _sigmoid(x):
    return 1.0 / (1.0 + jnp.exp(-x))


def _softplus(x):
    return jnp.maximum(x, 0.0) + jnp.log(1.0 + jnp.exp(-jnp.abs(x)))


def _silu(x):
    return x * _sigmoid(x)


_G_GLA = (0, 640)
_G_GOG = (640, 896)
_G_LX = (896, 1152)
_G_LG = (1152, 1408)
_G_DQ = (1408, 1664)
_G_DK = (1664, 1920)
_G_DV = (1920, 2176)
_G_NQKV = (2176, 2944)
_G_NZ = (2944, 3200)
_G_NBA = (3200, 3328)
IN_PAD_W = 3328
IN_TM = 512


def _permute_w_in(w):
    z = lambda n: jnp.zeros((w.shape[0], n), w.dtype)
    return jnp.concatenate([w[:, 0:512], w[:, 768:800], z(96), w[:, 512:768], w[:, 800:3120], z(112)],
                           axis=1).astype(BF16)


def _inproj_kernel(x_ref, g_ref, w_ref, gla_ref, gog_ref, lx_ref, lg_ref, dq_ref, dk_ref, dv_ref,
                   nqkv_ref, nz_ref, nba_ref):
    h = _rms_rows(x_ref[...], g_ref[...]).astype(BF16)

    def proj(grp):
        return jnp.dot(h, w_ref[:, grp[0]:grp[1]], preferred_element_type=F32)

    gla_ref[...] = proj(_G_GLA)
    gog_ref[...] = proj(_G_GOG).astype(BF16)
    lx_ref[...] = proj(_G_LX)
    lg_ref[...] = proj(_G_LG).astype(BF16)
    dq_ref[...] = (proj(_G_DQ) * (DIFF_DH ** -0.5 * LOG2E)).astype(BF16)
    dk_ref[...] = proj(_G_DK).astype(BF16)
    dv_ref[...] = proj(_G_DV).astype(BF16)
    nqkv_ref[...] = proj(_G_NQKV)
    nz_ref[...] = proj(_G_NZ).astype(BF16)
    nba_ref[...] = proj(_G_NBA)


def inproj_pallas(xt, g, w_in):
    T, D = xt.shape
    tm = min(IN_TM, T)
    groups = [(_G_GLA, F32), (_G_GOG, BF16), (_G_LX, F32), (_G_LG, BF16), (_G_DQ, BF16), (_G_DK, BF16),
              (_G_DV, BF16), (_G_NQKV, F32), (_G_NZ, BF16), (_G_NBA, F32)]
    return pl.pallas_call(
        _inproj_kernel,
        grid=(T // tm,),
        in_specs=[pl.BlockSpec((tm, D), lambda i: (i, 0)),
                  pl.BlockSpec((1, D), lambda i: (0, 0)),
                  pl.BlockSpec((D, IN_PAD_W), lambda i: (0, 0))],
        out_specs=[pl.BlockSpec((tm, b - a), lambda i: (i, 0)) for (a, b), _ in groups],
        out_shape=[jax.ShapeDtypeStruct((T, b - a), dt) for (a, b), dt in groups],
        compiler_params=_cparams("parallel"),
        name="inproj",
    )(xt, g.reshape(1, D).astype(F32), _permute_w_in(w_in))


def _gla_kernel(x_ref, tri_ref, w_ref, b_ref, o_ref, st_ref, *, nb):
    c = GLA_CHUNK

    @pl.when(pl.program_id(1) == 0)
    def _():
        st_ref[...] = jnp.zeros(st_ref.shape, F32)

    tri = tri_ref[0]
    tri_bf = tri.astype(BF16)
    tri4 = jnp.concatenate([tri] * GLA_HEADS, axis=0)
    w = w_ref[0]
    bias = b_ref[0]
    lane_qk = lax.broadcasted_iota(jnp.int32, (c, GLA_QK_W), 1) // GLA_DK
    lane_v = lax.broadcasted_iota(jnp.int32, (c, GLA_V_W), 1) // GLA_DV
    row_s = lax.broadcasted_iota(jnp.int32, (GLA_V_W, GLA_QK_W), 0) // GLA_DV
    col_s = lax.broadcasted_iota(jnp.int32, (GLA_V_W, GLA_QK_W), 1) // GLA_DK
    st_mask = row_s == col_s

    def one_batch(b):
        blk = x_ref[b]
        q = blk[:, 0:128] * GLA_DK ** -0.5
        k = blk[:, 128:256]
        v = blk[:, 256:512].astype(BF16)
        lr = blk[:, 512:640]
        z = _dot_f32(lr, w) + bias
        yield
        g = (jnp.minimum(z, 0.0) - jnp.log(1.0 + jnp.exp(-jnp.abs(z)))) * (1.0 / GLA_GATE_NORM)
        acc = None
        for p in _split_bf16(g, 3):
            t = _dot_nn(tri_bf, p)
            acc = t if acc is None else acc + t
        yield
        bc = acc
        tot = jnp.sum(g, axis=0, keepdims=True)
        ref = 0.5 * tot
        qt = q * jnp.exp(bc - ref)
        kt = (k * jnp.exp(ref - bc)).astype(BF16)
        qd = (q * jnp.exp(bc)).astype(BF16)
        kd = (k * jnp.exp(tot - bc)).astype(BF16)
        qstack = jnp.concatenate(
            [jnp.where(lane_qk == hh, qt, 0.0) for hh in range(GLA_HEADS)], axis=0).astype(BF16)
        s = _dot_nt(qstack, kt)
        st = st_ref[b]
        o = _dot_nt(qd, st.astype(BF16))
        upd = _dot_tn(v, kd)
        yield
        s = (s * tri4).astype(BF16)
        ohs = [_dot_nn(s[hh * c:(hh + 1) * c], v) for hh in range(GLA_HEADS)]
        yield
        for hh in range(GLA_HEADS):
            o = o + jnp.where(lane_v == hh, ohs[hh], 0.0)
        o_ref[0, b] = o
        st_ref[b] = st * jnp.exp(tot) + jnp.where(st_mask, upd, 0.0)

    chains = [one_batch(b) for b in range(nb)]
    live = True
    while live:
        live = False
        for ch in chains:
            try:
                next(ch)
                live = True
            except StopIteration:
                pass


def gla_pallas(gla_in, lr_up, lr_bias):
    B, S, W = gla_in.shape
    c = GLA_CHUNK
    nc = S // c
    r = np.arange(c)
    tri = jnp.asarray(np.stack([r[:, None] >= r[None, :], r[:, None] <= r[None, :]]).astype(np.float32))
    w = jnp.zeros((2, LANES, GLA_QK_W), F32)
    w = w.at[0, 0:GLA_RANK].set(lr_up[0].astype(F32)).at[1, GLA_RANK:2 * GLA_RANK].set(lr_up[1].astype(F32))
    bias = lr_bias.astype(F32).reshape(2, 1, GLA_QK_W)

    def tmap(d, i):
        return (0, i + d * (nc - 1 - 2 * i), 0)

    return pl.pallas_call(
        functools.partial(_gla_kernel, nb=B),
        grid=(2, nc),
        in_specs=[pl.BlockSpec((B, c, W), tmap),
                  pl.BlockSpec((1, c, c), lambda d, i: (d, 0, 0)),
                  pl.BlockSpec((1, LANES, GLA_QK_W), lambda d, i: (d, 0, 0)),
                  pl.BlockSpec((1, 1, GLA_QK_W), lambda d, i: (d, 0, 0))],
        out_specs=pl.BlockSpec((1, B, c, GLA_V_W), lambda d, i: (d,) + tmap(d, i)),
        out_shape=jax.ShapeDtypeStruct((2, B, S, GLA_V_W), F32),
        scratch_shapes=[pltpu.VMEM((B, GLA_V_W, GLA_QK_W), F32)],
        compiler_params=_cparams("arbitrary", "arbitrary"),
        name="gla",
    )(gla_in, tri, w, bias)


LRU_TB = 512
HALO = 8


def _halo_specs(tb, width, nt):
    r = tb // HALO
    return [pl.BlockSpec((1, HALO, width), lambda b, i: (b, jnp.maximum(i * r - 1, 0), 0)),
            pl.BlockSpec((1, tb, width), lambda b, i: (b, i, 0)),
            pl.BlockSpec((1, HALO, width), lambda b, i: (b, jnp.minimum((i + 1) * r, nt * r - 1), 0))]


def _conv4(prev_ref, cur_ref, next_ref, w, nt):
    i = pl.program_id(1)
    cur = cur_ref[0]
    tb = cur.shape[0]
    prev = prev_ref[0] * jnp.where(i > 0, 1.0, 0.0)
    nxt = next_ref[0] * jnp.where(i < nt - 1, 1.0, 0.0)
    ext = jnp.concatenate([prev, cur, nxt], axis=0)
    out = None
    for j in range(CONV_WIDTH):
        off = HALO + j - CONV_WIDTH // 2
        t = ext[off:off + tb] * w[j:j + 1]
        out = t if out is None else out + t
    return out


def _lru_prep_kernel(prev_ref, cur_ref, next_ref, cw_ref, cb_ref, wg_ref, bg_ref, lam_ref, a_ref, x_ref, *, nt):
    u = _conv4(prev_ref, cur_ref, next_ref, cw_ref[...], nt) + cb_ref[...]
    gates = _sigmoid(jnp.dot(u.astype(BF16), wg_ref[...], preferred_element_type=F32) + bg_ref[...])
    w = LRU_WIDTH
    for d in range(2):
        r = gates[:, (2 * d) * w:(2 * d + 1) * w]
        ig = gates[:, (2 * d + 1) * w:(2 * d + 2) * w]
        log_a = -LRU_C * r * _softplus(-lam_ref[d:d + 1])
        a_ref[d, 0] = jnp.exp(log_a)
        x_ref[d, 0] = jnp.sqrt(1.0 - jnp.exp(2.0 * log_a)) * (ig * u)


def _lru_scan_kernel(a_ref, x_ref, h_ref, st_ref, *, tb):
    d = pl.program_id(0)

    @pl.when(pl.program_id(1) == 0)
    def _():
        st_ref[...] = jnp.zeros(st_ref.shape, F32)

    def body(t, h):
        tt = t + d * (tb - 1 - 2 * t)
        h = a_ref[0, :, pl.ds(tt, 1), :] * h + x_ref[0, :, pl.ds(tt, 1), :]
        h_ref[0, :, pl.ds(tt, 1), :] = h
        return h

    st_ref[...] = lax.fori_loop(0, tb, body, st_ref[...], unroll=8)


def lru_pallas(lx, conv_w, conv_b, w_a, b_a, w_x, b_x, lam):
    B, S, W = lx.shape
    tb = min(LRU_TB, S)
    nt = S // tb
    blk = W // LRU_BLOCKS

    def dense(wb):
        m = jnp.zeros((W, W), F32)
        for n in range(LRU_BLOCKS):
            m = m.at[n * blk:(n + 1) * blk, n * blk:(n + 1) * blk].set(wb[n].astype(F32))
        return m

    wg = jnp.concatenate([dense(w_a[0]), dense(w_x[0]), dense(w_a[1]), dense(w_x[1])], axis=1).astype(BF16)
    bg = jnp.concatenate([b_a[0], b_x[0], b_a[1], b_x[1]]).astype(F32).reshape(1, 4 * W)
    full = lambda shape: pl.BlockSpec(shape, lambda b, i: (0,) * len(shape))
    a, xin = pl.pallas_call(
        functools.partial(_lru_prep_kernel, nt=nt),
        grid=(B, nt),
        in_specs=_halo_specs(tb, W, nt) + [full((CONV_WIDTH, W)), full((1, W)), full((W, 4 * W)), full((1, 4 * W)),
                                           full((2, W))],
        out_specs=[pl.BlockSpec((2, 1, tb, W), lambda b, i: (0, b, i, 0))] * 2,
        out_shape=[jax.ShapeDtypeStruct((2, B, S, W), F32)] * 2,
        compiler_params=_cparams("parallel", "parallel"),
        name="lru_prep",
    )(lx, lx, lx, conv_w.astype(F32), conv_b.astype(F32).reshape(1, W), wg, bg, lam.astype(F32))

    def tmap(d, i):
        return (d, 0, i + d * (nt - 1 - 2 * i), 0)

    return pl.pallas_call(
        functools.partial(_lru_scan_kernel, tb=tb),
        grid=(2, nt),
        in_specs=[pl.BlockSpec((1, B, tb, W), tmap)] * 2,
        out_specs=pl.BlockSpec((1, B, tb, W), tmap),
        out_shape=jax.ShapeDtypeStruct((2, B, S, W), F32),
        scratch_shapes=[pltpu.VMEM((B, 1, W), F32)],
        compiler_params=_cparams("arbitrary", "arbitrary"),
        name="lru_scan",
    )(a, xin)


DIFF_TQ = 256
DIFF_TK = 512
DIFF_POS_LANE = 2 * DIFF_DH
DIFF_POS_PIECES = 3
DIFF_UNROLL = 2
DIFF_UNDERFLOW = 160.0
DIFF_VT_ROWS = DIFF_DV + 8
LOG2E = 1.4426950408889634


def _diff_attn_kernel(slopes_ref, islopes_ref, lam_ref, q_ref, k_ref, vt_ref, g_ref, o_ref,
                      qv_scr, rt_scr, s_scr, mb_scr, ro_scr, p_scr, al_scr, m_scr, acc_scr, kn_scr,
                      *, tq, tk, seq, out_scale):
    h = pl.program_id(1)
    qi = pl.program_id(2)
    slope = slopes_ref[h]
    lam = lam_ref[0]
    nk = seq // tk

    q = q_ref[0, 0]
    lane = lax.broadcasted_iota(jnp.int32, q.shape, 1)
    zero = jnp.zeros_like(q)
    q1 = jnp.where(jnp.abs(2 * lane - (DIFF_DH + DIFF_POS_LANE - 1)) < DIFF_DH, zero, q)
    q2 = jnp.where(lane < DIFF_DH, zero, q)
    qs = jnp.concatenate([q1, q2], axis=0)
    lane2 = lax.broadcasted_iota(jnp.int32, qs.shape, 1)
    col = lax.broadcasted_iota(jnp.int32, (1, 2 * tq), 1)
    qpos = qi * tq + jnp.where(col >= tq, col - tq, col)
    shift_q = slope * qpos.astype(F32)
    qv_scr[0] = qs
    qv_scr[1] = jnp.where(lane2 >= DIFF_POS_LANE, -qs, qs)
    rt_scr[0] = -shift_q
    rt_scr[1] = shift_q

    @pl.when(qi == 0)
    def _():
        def chunk(i, best):
            kc = k_ref[0, 0, pl.ds(pl.multiple_of(i * tk, tk), tk), :].astype(F32)
            lk = lax.broadcasted_iota(jnp.int32, kc.shape, 1)
            k2 = kc * kc
            n1 = jnp.sum(jnp.where(lk < DIFF_DH, k2, 0.0), axis=1, keepdims=True)
            n2 = jnp.sum(jnp.where(jnp.abs(2 * lk - (DIFF_DH + DIFF_POS_LANE - 1)) < DIFF_DH, k2, 0.0),
                         axis=1, keepdims=True)
            return jnp.maximum(best, jnp.max(jnp.maximum(n1, n2), axis=0, keepdims=True))

        best = lax.fori_loop(0, nk, chunk, jnp.zeros((1, 1), F32))
        kn_scr[...] = jnp.broadcast_to(jnp.sqrt(best), kn_scr.shape)

    kd = (qi * tq) // tk
    k0 = pl.multiple_of(kd * tk, tk)
    q_mid = jnp.where(lane2 >= DIFF_POS_LANE, jnp.zeros_like(qs), qs)
    s = _dot_nt(k_ref[0, 0, pl.ds(k0, tk), :], q_mid)
    kpos = k0 + lax.broadcasted_iota(jnp.int32, (tk, 2 * tq), 0)
    s = s - slope * jnp.abs(qpos - kpos).astype(F32)
    m0 = jnp.max(s, axis=0, keepdims=True)
    p = jnp.exp2(s - m0).astype(BF16)
    acc_scr[...] = jnp.dot(vt_ref[0, 0, :, pl.ds(k0, tk)], p, preferred_element_type=F32)
    m_scr[...] = m0

    m_min = jnp.min(m0, axis=1, keepdims=True)
    qf = q.astype(F32)
    sq = qf * qf
    qn1 = jnp.sum(jnp.where(lane < DIFF_DH, sq, 0.0), axis=1, keepdims=True)
    qn2 = jnp.sum(jnp.where(jnp.abs(2 * lane - (DIFF_DH + DIFF_POS_LANE - 1)) < DIFF_DH, sq, 0.0),
                  axis=1, keepdims=True)
    qn = jnp.sqrt(jnp.max(jnp.maximum(qn1, qn2), axis=0, keepdims=True))
    reach = (qn * kn_scr[0:1, 0:1] * 1.001 - m_min + DIFF_UNDERFLOW) * islopes_ref[h]
    q_first = jnp.full((1, 1), qi * tq, jnp.int32).astype(F32)
    kd_f = jnp.full((1, 1), kd, jnp.int32).astype(F32)
    lo_f = jnp.ceil((q_first - (tk - 1) - reach) * (1.0 / tk))
    hi_f = jnp.floor((q_first + (tq - 1) + reach) * (1.0 / tk))
    k_lo = jnp.max(jnp.clip(lo_f, 0.0, kd_f).astype(jnp.int32))
    k_hi = jnp.max(jnp.clip(hi_f, kd_f, float(nk - 1)).astype(jnp.int32))
    n_act = k_hi - k_lo

    def tile_of(t):
        kj = k_lo + t
        return jnp.minimum(kj + jnp.where(kj >= kd, 1, 0), nk - 1)

    def scores(t, slot):
        kj = tile_of(t)
        var = jnp.where(kj > kd, 1, 0)
        kt = k_ref[0, 0, pl.ds(pl.multiple_of(kj * tk, tk), tk), :]
        sc = _dot_nt(kt, qv_scr[var])
        s_scr[slot] = sc
        off = rt_scr[var] + jnp.where(t < n_act, 0.0, NEG_BIG)
        ro_scr[slot] = off
        mb_scr[slot] = jnp.max(sc, axis=0, keepdims=True) + off

    def softmax(slot):
        m_old = m_scr[...]
        m_new = jnp.maximum(m_old, mb_scr[slot])
        al_scr[slot] = jnp.exp2(m_old - m_new)
        p_scr[slot] = jnp.exp2(s_scr[slot] - (m_new - ro_scr[slot])).astype(BF16)
        m_scr[...] = m_new

    def pv(t, slot):
        vt = vt_ref[0, 0, :, pl.ds(pl.multiple_of(tile_of(t) * tk, tk), tk)]
        acc_scr[...] = al_scr[slot] * acc_scr[...] + jnp.dot(vt, p_scr[slot], preferred_element_type=F32)

    scores(0, 0)
    scores(1, 1)
    softmax(0)

    def trip(t, slot):
        pv(t, slot)
        scores(t + 2, slot)
        softmax(1 - slot)

    def body(i, c):
        for u in range(DIFF_UNROLL):
            trip(DIFF_UNROLL * i + u, u % 2)
        return c

    lax.fori_loop(0, (n_act + DIFF_UNROLL - 1) // DIFF_UNROLL, body, 0)

    acc = acc_scr[...]
    o2 = acc[0:DIFF_DV] * (1.0 / acc[DIFF_DV:DIFF_DV + 1])
    o = o2[:, :tq] - lam * o2[:, tq:]
    ms = jnp.mean(o * o, axis=0, keepdims=True)
    o = o * lax.rsqrt(ms + RMS_EPS) * g_ref[...] * out_scale
    o_ref[0, 0] = o.astype(o_ref.dtype)


def diff_attention_pallas(dq, dk, dv, lq1, lk1, lq2, lk2, norm_g, lambda_init):
    B, S, _ = dq.shape
    H = DIFF_HEADS
    tq = min(DIFF_TQ, S)
    tk = min(DIFF_TK, S)
    slopes = jnp.exp2(-8.0 * jnp.arange(1, H + 1, dtype=F32) / H) * LOG2E
    lam = (jnp.exp(jnp.sum(lq1.astype(F32) * lk1.astype(F32)))
           - jnp.exp(jnp.sum(lq2.astype(F32) * lk2.astype(F32))) + lambda_init).reshape(1)
    npos = DIFF_POS_PIECES
    pad = LANES - 2 * DIFF_DH - npos
    q4 = dq.astype(BF16).reshape(B, S, H, 2 * DIFF_DH).transpose(0, 2, 1, 3)
    q_aug = jnp.concatenate([q4, jnp.ones((B, H, S, npos), BF16), jnp.zeros((B, H, S, pad), BF16)], axis=-1)
    k4 = dk.astype(BF16).reshape(B, S, H, 2 * DIFF_DH).transpose(0, 2, 1, 3)
    kbias = slopes[:, None] * jnp.arange(S, dtype=F32)[None, :]
    kpos = jnp.stack(_split_trunc_bf16(kbias, npos), axis=-1)[None]
    k_aug = jnp.concatenate([k4, jnp.broadcast_to(kpos, (B, H, S, npos)), jnp.zeros((B, H, S, pad), BF16)],
                            axis=-1)
    v4 = dv.astype(BF16).reshape(B, S, H, DIFF_DV).transpose(0, 2, 3, 1)
    vt = jnp.concatenate([v4, jnp.ones((B, H, 1, S), BF16),
                          jnp.zeros((B, H, DIFF_VT_ROWS - DIFF_DV - 1, S), BF16)], axis=2)
    g = norm_g.astype(F32).reshape(DIFF_DV, 1)

    kern = functools.partial(_diff_attn_kernel, tq=tq, tk=tk, seq=S, out_scale=1.0 - lambda_init)
    out = pl.pallas_call(
        kern,
        grid=(B, H, S // tq),
        in_specs=[
            pl.BlockSpec(memory_space=pltpu.SMEM),
            pl.BlockSpec(memory_space=pltpu.SMEM),
            pl.BlockSpec(memory_space=pltpu.SMEM),
            pl.BlockSpec((1, 1, tq, LANES), lambda b, h, i: (b, h, i, 0)),
            pl.BlockSpec((1, 1, S, LANES), lambda b, h, i: (b, h, 0, 0)),
            pl.BlockSpec((1, 1, DIFF_VT_ROWS, S), lambda b, h, i: (b, h, 0, 0)),
            pl.BlockSpec((DIFF_DV, 1), lambda b, h, i: (0, 0)),
        ],
        out_specs=pl.BlockSpec((1, 1, DIFF_DV, tq), lambda b, h, i: (b, h, 0, i)),
        out_shape=jax.ShapeDtypeStruct((B, H, DIFF_DV, S), F32),
        scratch_shapes=[
            pltpu.VMEM((2, 2 * tq, LANES), BF16),
            pltpu.VMEM((2, 1, 2 * tq), F32),
            pltpu.VMEM((2, tk, 2 * tq), F32),
            pltpu.VMEM((2, 1, 2 * tq), F32),
            pltpu.VMEM((2, 1, 2 * tq), F32),
            pltpu.VMEM((2, tk, 2 * tq), BF16),
            pltpu.VMEM((2, 1, 2 * tq), F32),
            pltpu.VMEM((1, 2 * tq), F32),
            pltpu.VMEM((DIFF_VT_ROWS, 2 * tq), F32),
            pltpu.VMEM((8, LANES), F32),
        ],
        compiler_params=pltpu.CompilerParams(
            dimension_semantics=("parallel", "parallel", "arbitrary"),
            vmem_limit_bytes=VMEM_LIMIT),
        name="diff_attn",
    )(slopes, 1.0 / slopes, lam, q_aug, k_aug, vt, g)
    return out.transpose(0, 3, 1, 2).reshape(B, S, DIFF_V_W)


MERGE_TM = 256


def _group_rms(o, ones_bd, g, width):
    ss = _dot_split(o * o, ones_bd, n=2)
    return o * lax.rsqrt(ss * (1.0 / width) + RMS_EPS) * g


def _gelu_tanh(x):
    return 0.5 * x * (1.0 + jnp.tanh(0.7978845608028654 * (x + 0.044715 * (x * x * x))))


def _merge_kernel(x_ref, gla_ref, gog_ref, lru_ref, lg_ref, dif_ref, gdn_ref, nz_ref,
                  mg_ref, glag_ref, gdng_ref, ones_ref, mw_ref, mb_ref, up_ref, wo_ref, o_ref):
    x = x_ref[...]
    h = _rms_rows(x, mg_ref[...]).astype(BF16)
    ones_bd = ones_ref[...]
    y_gla = _group_rms(gla_ref[0] + gla_ref[1], ones_bd, glag_ref[...], GLA_DV) * _silu(gog_ref[...].astype(F32))
    y_lru = (lru_ref[0] + lru_ref[1]) * _gelu_tanh(lg_ref[...].astype(F32))
    y_dif = dif_ref[...]
    y_gdn = _group_rms(gdn_ref[0] + gdn_ref[1], ones_bd, gdng_ref[...], GDN_DV) * _silu(nz_ref[...].astype(F32))
    merged = None
    for i, y in enumerate((y_gla, y_lru, y_dif, y_gdn)):
        gate = _sigmoid(jnp.dot(h, mw_ref[i], preferred_element_type=F32) + mb_ref[i])
        term = gate * jnp.dot(y.astype(BF16), up_ref[i], preferred_element_type=F32)
        merged = term if merged is None else merged + term
    o_ref[...] = x + jnp.dot(merged.astype(BF16), wo_ref[...], preferred_element_type=F32)


def merge_pallas(xt, gla_o, gog, lru_h, lg, y_diff, gdn_o, nz, mix_g, gla_g, gdn_g, merge_w, merge_b, branch_up,
                 mix_out):
    T, D = xt.shape
    tm = min(MERGE_TM, T)
    W = BRANCH_WIDTH
    r = np.arange(W) // GLA_DV
    ones_bd = jnp.asarray((r[:, None] == r[None, :]).astype(np.float32)).astype(BF16)
    tok = lambda w: pl.BlockSpec((tm, w), lambda i: (i, 0))
    tok2 = pl.BlockSpec((2, tm, W), lambda i: (0, i, 0))
    const = lambda shape: pl.BlockSpec(shape, lambda i: (0,) * len(shape), pipeline_mode=pl.Buffered(1))
    return pl.pallas_call(
        _merge_kernel,
        grid=(T // tm,),
        in_specs=[tok(D), tok2, tok(W), tok2, tok(W), tok(W), tok2, tok(W),
                  const((1, D)), const((1, W)), const((1, W)), const((W, W)),
                  const((N_BRANCHES, D, D)), const((N_BRANCHES, 1, D)), const((N_BRANCHES, W, D)), const((D, D))],
        out_specs=tok(D),
        out_shape=jax.ShapeDtypeStruct((T, D), F32),
        compiler_params=_cparams("parallel"),
        name="merge",
    )(xt, gla_o, gog, lru_h, lg, y_diff, gdn_o, nz,
      mix_g.reshape(1, D).astype(F32), jnp.tile(gla_g.astype(F32), GLA_HEADS).reshape(1, W),
      jnp.tile(gdn_g.astype(F32), GDN_HEADS).reshape(1, W), ones_bd,
      merge_w.astype(BF16), merge_b.astype(F32).reshape(N_BRANCHES, 1, D), branch_up.astype(BF16),
      mix_out.astype(BF16))


XATTN_TM = 512


def _kv_kernel(m_ref, g_ref, w_ref, o_ref):
    mn = _rms_rows(m_ref[0], g_ref[...]).astype(BF16)
    o_ref[0] = jnp.dot(mn, w_ref[...], preferred_element_type=F32).astype(BF16)


def _xattn_kernel(x_ref, g_ref, wq_ref, kv_ref, wo_ref, o_ref):
    x = x_ref[0]
    h = _rms_rows(x, g_ref[...]).astype(BF16)
    q = jnp.dot(h, wq_ref[...], preferred_element_type=F32).astype(BF16)
    outs = []
    for hh in range(XATTN_HEADS):
        lo = hh * XATTN_DH
        k = kv_ref[0, :, lo:lo + XATTN_DH]
        v = kv_ref[0, :, D_MODEL + lo:D_MODEL + lo + XATTN_DH]
        s = _dot_nt(q[:, lo:lo + XATTN_DH], k) * XATTN_DH ** -0.5
        p = jnp.exp(s - jnp.max(s, axis=-1, keepdims=True))
        p = p / jnp.sum(p, axis=-1, keepdims=True)
        outs.append(jnp.dot(p.astype(BF16), v, preferred_element_type=F32).astype(BF16))
    o = jnp.concatenate(outs, axis=-1)
    o_ref[0] = x + jnp.dot(o, wo_ref[...], preferred_element_type=F32)


def xattn_pallas(x, mem, xg, mg, wq, wkv, wo):
    B, S, D = x.shape
    M = mem.shape[1]
    tm = min(XATTN_TM, S)
    kv = pl.pallas_call(
        _kv_kernel,
        grid=(B,),
        in_specs=[pl.BlockSpec((1, M, D), lambda b: (b, 0, 0)),
                  pl.BlockSpec((1, D), lambda b: (0, 0)),
                  pl.BlockSpec((D, 2 * D), lambda b: (0, 0))],
        out_specs=pl.BlockSpec((1, M, 2 * D), lambda b: (b, 0, 0)),
        out_shape=jax.ShapeDtypeStruct((B, M, 2 * D), BF16),
        compiler_params=_cparams("parallel"),
        name="xattn_kv",
    )(mem, mg.reshape(1, D).astype(F32), wkv.astype(BF16))
    const = lambda shape: pl.BlockSpec(shape, lambda b, i: (0,) * len(shape), pipeline_mode=pl.Buffered(1))
    return pl.pallas_call(
        _xattn_kernel,
        grid=(B, S // tm),
        in_specs=[pl.BlockSpec((1, tm, D), lambda b, i: (b, i, 0)),
                  const((1, D)), const((D, D)),
                  pl.BlockSpec((1, M, 2 * D), lambda b, i: (b, 0, 0)),
                  const((D, D))],
        out_specs=pl.BlockSpec((1, tm, D), lambda b, i: (b, i, 0)),
        out_shape=jax.ShapeDtypeStruct((B, S, D), F32),
        compiler_params=_cparams("parallel", "parallel"),
        name="xattn",
    )(x, xg.reshape(1, D).astype(F32), wq.astype(BF16), kv, wo.astype(BF16))


ROUTE_TM = 512
ROUTE_EXPERT_LANE = 32
MOE_BM = 256
COMBINE_TM = 512


def _router_kernel(x_ref, g_ref, w_ref, b_ref, h_ref, r_ref):
    h = _rms_rows(x_ref[...], g_ref[...])
    h_ref[...] = h
    logits = _dot_f32(h, w_ref[...]) + b_ref[...]
    lane = lax.broadcasted_iota(jnp.int32, logits.shape, 1)
    big = jnp.int32(1 << 20)
    neg = jnp.float32(-jnp.inf)

    def top(vals):
        m = jnp.max(vals, axis=-1, keepdims=True)
        idx = jnp.min(jnp.where(vals == m, lane, big), axis=-1, keepdims=True)
        return m, idx

    is_g = lane < N_GROUPS
    gmax, gidx = top(jnp.where(is_g, logits, neg))
    gsum = jnp.sum(jnp.where(is_g, jnp.exp(logits - gmax), 0.0), axis=-1, keepdims=True)
    g_w = 1.0 / gsum
    lo = ROUTE_EXPERT_LANE + EXPERTS_PER_GROUP * gidx
    in_grp = jnp.abs(2 * (lane - lo) - (EXPERTS_PER_GROUP - 1)) < EXPERTS_PER_GROUP
    el = jnp.where(in_grp, logits, neg)
    e1, i1 = top(el)
    e2, i2 = top(jnp.where(lane == i1, neg, el))
    t = jnp.exp(e2 - e1)
    w1 = g_w / (1.0 + t)
    w2 = g_w * t / (1.0 + t)
    f = lambda v: v.astype(F32)
    r_ref[...] = jnp.where(lane == 0, f(i1 - ROUTE_EXPERT_LANE),
                           jnp.where(lane == 1, f(i2 - ROUTE_EXPERT_LANE),
                                     jnp.where(lane == 2, w1, jnp.where(lane == 3, w2, 0.0))))


def router_pallas(xt, g, w_group, b_group, w_expert, b_expert):
    T, D = xt.shape
    tm = min(ROUTE_TM, T)
    w = jnp.zeros((D, LANES), F32).at[:, :N_GROUPS].set(w_group.astype(F32))
    w = w.at[:, ROUTE_EXPERT_LANE:ROUTE_EXPERT_LANE + N_EXPERTS].set(w_expert.astype(F32))
    b = jnp.zeros((1, LANES), F32).at[0, :N_GROUPS].set(b_group.astype(F32))
    b = b.at[0, ROUTE_EXPERT_LANE:ROUTE_EXPERT_LANE + N_EXPERTS].set(b_expert.astype(F32))
    return pl.pallas_call(
        _router_kernel,
        grid=(T // tm,),
        in_specs=[pl.BlockSpec((tm, D), lambda i: (i, 0)),
                  pl.BlockSpec((1, D), lambda i: (0, 0)),
                  pl.BlockSpec((D, LANES), lambda i: (0, 0)),
                  pl.BlockSpec((1, LANES), lambda i: (0, 0))],
        out_specs=[pl.BlockSpec((tm, D), lambda i: (i, 0)), pl.BlockSpec((tm, LANES), lambda i: (i, 0))],
        out_shape=[jax.ShapeDtypeStruct((T, D), F32), jax.ShapeDtypeStruct((T, LANES), F32)],
        compiler_params=_cparams("parallel"),
        name="moe_router",
    )(xt, g.reshape(1, D).astype(F32), w, b)


def _gather_rows(idx_hbm_row, src_hbm, idx_smem, dst, sem_idx, sem_rows, n):
    cp = pltpu.make_async_copy(idx_hbm_row, idx_smem, sem_idx)
    cp.start()
    cp.wait()

    def row_copy(r):
        return pltpu.make_async_copy(src_hbm.at[pl.ds(idx_smem[r], 1)], dst.at[pl.ds(r, 1)], sem_rows)

    def issue(r, c):
        row_copy(r).start()
        return c

    def drain(r, c):
        row_copy(r).wait()
        return c

    lax.fori_loop(0, n, issue, 0)
    lax.fori_loop(0, n, drain, 0)


def _expert_kernel(be_ref, nu_ref, idx_hbm, h_hbm, w1_ref, w3_ref, w2_ref, y_ref, idx_smem, xbuf, sem_idx, sem_rows,
                   *, bm):
    i = pl.program_id(0)

    @pl.when(i < nu_ref[0])
    def _():
        _gather_rows(idx_hbm.at[i], h_hbm, idx_smem, xbuf, sem_idx, sem_rows, bm)
        xb = xbuf[...].astype(BF16)
        a = jnp.dot(xb, w1_ref[0], preferred_element_type=F32)
        g = jnp.dot(xb, w3_ref[0], preferred_element_type=F32)
        y_ref[...] = jnp.dot((_silu(a) * g).astype(BF16), w2_ref[0], preferred_element_type=F32)

    @pl.when(i >= nu_ref[0])
    def _():
        y_ref[...] = jnp.zeros(y_ref.shape, F32)


def _combine_kernel(x_ref, r_ref, pos_hbm, y_hbm, g_ref, o_ref, idx_smem, ybuf, sem_idx, sem_rows, *, tm, final):
    i = pl.program_id(0)
    _gather_rows(pos_hbm.at[i], y_hbm, idx_smem, ybuf, sem_idx, sem_rows, 2 * tm)
    r = r_ref[...]
    out = x_ref[...] + r[:, 2:3] * ybuf[0:tm] + r[:, 3:4] * ybuf[tm:2 * tm]
    if final:
        out = _rms_rows(out, g_ref[...])
    o_ref[...] = out


def moe_pallas(xt, g, w_group, b_group, w_expert, b_expert, w1, w3, w2, final_g=None):
    T, D = xt.shape
    bm = MOE_BM
    h, route = router_pallas(xt, g, w_group, b_group, w_expert, b_expert)
    eid = route[:, 0:2].astype(jnp.int32).reshape(-1)
    A = T * TOP_K
    n_blk = A // bm + N_EXPERTS
    P = n_blk * bm
    order = jnp.argsort(eid)
    eid_s = eid[order]
    tok_s = order // TOP_K
    counts = jnp.zeros((N_EXPERTS,), jnp.int32).at[eid].add(1)
    padded = (counts + bm - 1) // bm * bm
    pad_end = jnp.cumsum(padded)
    pad_start = pad_end - padded
    seg_start = jnp.cumsum(counts) - counts
    dest = pad_start[eid_s] + jnp.arange(A, dtype=jnp.int32) - seg_start[eid_s]
    row_tok = jnp.zeros((P,), jnp.int32).at[dest].set(tok_s.astype(jnp.int32))
    pos = jnp.zeros((A,), jnp.int32).at[order].set(dest)
    blk_expert = jnp.minimum(
        jnp.searchsorted(pad_end, jnp.arange(n_blk, dtype=jnp.int32) * bm, side='right'), N_EXPERTS - 1
    ).astype(jnp.int32)
    n_used = (pad_end[-1] // bm).astype(jnp.int32).reshape(1)

    y = pl.pallas_call(
        functools.partial(_expert_kernel, bm=bm),
        grid_spec=pltpu.PrefetchScalarGridSpec(
            num_scalar_prefetch=2,
            grid=(n_blk,),
            in_specs=[pl.BlockSpec(memory_space=pl.ANY),
                      pl.BlockSpec(memory_space=pl.ANY),
                      pl.BlockSpec((1, D, D_EXPERT), lambda i, be, nu: (be[i], 0, 0)),
                      pl.BlockSpec((1, D, D_EXPERT), lambda i, be, nu: (be[i], 0, 0)),
                      pl.BlockSpec((1, D_EXPERT, D), lambda i, be, nu: (be[i], 0, 0))],
            out_specs=pl.BlockSpec((bm, D), lambda i, be, nu: (i, 0)),
            scratch_shapes=[pltpu.SMEM((bm,), jnp.int32), pltpu.VMEM((bm, D), F32),
                            pltpu.SemaphoreType.DMA(()), pltpu.SemaphoreType.DMA(())]),
        out_shape=jax.ShapeDtypeStruct((P, D), F32),
        compiler_params=_cparams("arbitrary"),
        name="moe_experts",
    )(blk_expert, n_used, row_tok.reshape(n_blk, bm), h, w1.astype(BF16), w3.astype(BF16), w2.astype(BF16))

    tm = min(COMBINE_TM, T)
    nt = T // tm
    pos_t = pos.reshape(nt, tm, TOP_K).transpose(0, 2, 1).reshape(nt, TOP_K * tm)
    fg = (final_g if final_g is not None else jnp.ones((D,), F32)).reshape(1, D).astype(F32)
    return pl.pallas_call(
        functools.partial(_combine_kernel, tm=tm, final=final_g is not None),
        grid=(nt,),
        in_specs=[pl.BlockSpec((tm, D), lambda i: (i, 0)),
                  pl.BlockSpec((tm, LANES), lambda i: (i, 0)),
                  pl.BlockSpec(memory_space=pl.ANY),
                  pl.BlockSpec(memory_space=pl.ANY),
                  pl.BlockSpec((1, D), lambda i: (0, 0))],
        out_specs=pl.BlockSpec((tm, D), lambda i: (i, 0)),
        out_shape=jax.ShapeDtypeStruct((T, D), F32),
        scratch_shapes=[pltpu.SMEM((TOP_K * tm,), jnp.int32), pltpu.VMEM((TOP_K * tm, D), F32),
                        pltpu.SemaphoreType.DMA(()), pltpu.SemaphoreType.DMA(())],
        compiler_params=_cparams("arbitrary"),
        name="moe_combine",
    )(xt, route, pos_t, y, fg)


PAIR_BM = 256
PAIR_DMA_GROUP = 8
PAIRS_PER_GROUP = EXPERTS_PER_GROUP * (EXPERTS_PER_GROUP - 1) // 2
N_CLASSES = N_GROUPS * PAIRS_PER_GROUP


def _route_class_kernel(x_ref, g_ref, w_ref, b_ref, r_ref):
    h = _rms_rows(x_ref[...], g_ref[...])
    logits = _dot_f32(h, w_ref[...]) + b_ref[...]
    lane = lax.broadcasted_iota(jnp.int32, logits.shape, 1)
    big = jnp.int32(1 << 20)
    neg = jnp.float32(-jnp.inf)

    def top(vals):
        m = jnp.max(vals, axis=-1, keepdims=True)
        idx = jnp.min(jnp.where(vals == m, lane, big), axis=-1, keepdims=True)
        return m, idx

    _, gidx = top(jnp.where(lane < N_GROUPS, logits, neg))
    lo0 = ROUTE_EXPERT_LANE + EXPERTS_PER_GROUP * gidx
    in_grp = jnp.abs(2 * (lane - lo0) - (EXPERTS_PER_GROUP - 1)) < EXPERTS_PER_GROUP
    el = jnp.where(in_grp, logits, neg)
    _, i1 = top(el)
    _, i2 = top(jnp.where(lane == i1, neg, el))
    a = jnp.minimum(i1, i2) - lo0
    b = jnp.maximum(i1, i2) - lo0
    pair = a * (2 * EXPERTS_PER_GROUP - 1 - a) // 2 + (b - a - 1)
    r_ref[...] = jnp.broadcast_to(gidx * PAIRS_PER_GROUP + pair, r_ref.shape)


def _pair_expert_kernel(lo_ref, hi_ref, nv_ref, idx_hbm, x_hbm, g_ref, wr_ref, br_ref, fg_ref,
                        w1a_ref, w3a_ref, w2a_ref, w1b_ref, w3b_ref, w2b_ref, o_hbm,
                        idx_smem, xbuf, obuf, sem_idx, sem_in, sem_out, *, bm, final):
    i = pl.program_id(0)
    n = nv_ref[i]

    @pl.when(i == 0)
    def _():
        xbuf[...] = jnp.zeros(xbuf.shape, F32)

    @pl.when(n > 0)
    def _():
        cp = pltpu.make_async_copy(idx_hbm.at[i], idx_smem, sem_idx)
        cp.start()
        cp.wait()

        def in_copy(r):
            return pltpu.make_async_copy(x_hbm.at[pl.ds(idx_smem[r], 1)], xbuf.at[pl.ds(r, 1)], sem_in)

        def out_copy(r):
            return pltpu.make_async_copy(obuf.at[pl.ds(r, 1)], o_hbm.at[pl.ds(idx_smem[r], 1)], sem_out)

        def loop(copy_of, wait):
            def act(r, u):
                cpy = copy_of(r)
                cpy.wait() if wait else cpy.start(priority=u % 2)

            def group(gi, c):
                for u in range(PAIR_DMA_GROUP):
                    act(gi * PAIR_DMA_GROUP + u, u)
                return c

            def single(r, c):
                act(r, 0)
                return c

            n_groups = n // PAIR_DMA_GROUP
            lax.fori_loop(0, n_groups, group, 0)
            lax.fori_loop(n_groups * PAIR_DMA_GROUP, n, single, 0)

        loop(in_copy, False)
        loop(in_copy, True)
        x = xbuf[...]
        h = _rms_rows(x, g_ref[...])
        logits = _dot_f32(h, wr_ref[...]) + br_ref[...]
        lane = lax.broadcasted_iota(jnp.int32, logits.shape, 1)
        lo = lo_ref[i]
        hi = hi_ref[i]
        grp = lo // EXPERTS_PER_GROUP
        is_g = lane < N_GROUPS
        gmax = jnp.max(jnp.where(is_g, logits, -jnp.inf), axis=-1, keepdims=True)
        eg = jnp.exp(logits - gmax)
        g_w = (jnp.sum(jnp.where(lane == grp, eg, 0.0), axis=-1, keepdims=True)
               / jnp.sum(jnp.where(is_g, eg, 0.0), axis=-1, keepdims=True))
        e_lo = jnp.sum(jnp.where(lane == ROUTE_EXPERT_LANE + lo, logits, 0.0), axis=-1, keepdims=True)
        e_hi = jnp.sum(jnp.where(lane == ROUTE_EXPERT_LANE + hi, logits, 0.0), axis=-1, keepdims=True)
        m = jnp.maximum(e_lo, e_hi)
        t_lo = jnp.exp(e_lo - m)
        t_hi = jnp.exp(e_hi - m)
        inv = g_w / (t_lo + t_hi)
        hb = h.astype(BF16)

        def expert(w1, w3, w2):
            a = jnp.dot(hb, w1[0], preferred_element_type=F32)
            b = jnp.dot(hb, w3[0], preferred_element_type=F32)
            return jnp.dot((_silu(a) * b).astype(BF16), w2[0], preferred_element_type=F32)

        y = expert(w1a_ref, w3a_ref, w2a_ref) * (t_lo * inv) + expert(w1b_ref, w3b_ref, w2b_ref) * (t_hi * inv)
        out = x + y
        if final:
            out = _rms_rows(out, fg_ref[...])
        obuf[...] = out
        loop(out_copy, False)
        loop(out_copy, True)


def moe_pair_pallas(xt, g, w_group, b_group, w_expert, b_expert, w1, w3, w2, final_g=None):
    T, D = xt.shape
    bm = PAIR_BM
    tm = min(ROUTE_TM, T)
    wr = jnp.zeros((D, LANES), F32).at[:, :N_GROUPS].set(w_group.astype(F32))
    wr = wr.at[:, ROUTE_EXPERT_LANE:ROUTE_EXPERT_LANE + N_EXPERTS].set(w_expert.astype(F32))
    br = jnp.zeros((1, LANES), F32).at[0, :N_GROUPS].set(b_group.astype(F32))
    br = br.at[0, ROUTE_EXPERT_LANE:ROUTE_EXPERT_LANE + N_EXPERTS].set(b_expert.astype(F32))
    g2 = g.reshape(1, D).astype(F32)
    cls = pl.pallas_call(
        _route_class_kernel,
        grid=(T // tm,),
        in_specs=[pl.BlockSpec((tm, D), lambda i: (i, 0)),
                  pl.BlockSpec((1, D), lambda i: (0, 0)),
                  pl.BlockSpec((D, LANES), lambda i: (0, 0)),
                  pl.BlockSpec((1, LANES), lambda i: (0, 0))],
        out_specs=pl.BlockSpec((tm, LANES), lambda i: (i, 0)),
        out_shape=jax.ShapeDtypeStruct((T, LANES), jnp.int32),
        compiler_params=_cparams("parallel"),
        name="moe_route",
    )(xt, g2, wr, br)[:, 0]

    n_blk = T // bm + N_CLASSES
    order = jnp.argsort(cls).astype(jnp.int32)
    cls_s = cls[order]
    bounds = jnp.searchsorted(cls_s, jnp.arange(N_CLASSES + 1, dtype=jnp.int32), side='left').astype(jnp.int32)
    seg_start = bounds[:-1]
    counts = bounds[1:] - seg_start
    nblk_c = (counts + bm - 1) // bm
    blk_end = jnp.cumsum(nblk_c)
    blk_start = blk_end - nblk_c
    bidx = jnp.arange(n_blk, dtype=jnp.int32)
    blk_cls = jnp.minimum(jnp.searchsorted(blk_end, bidx, side='right'), N_CLASSES - 1).astype(jnp.int32)
    first = seg_start[blk_cls] + (bidx - blk_start[blk_cls]) * bm
    n_valid = jnp.clip(seg_start[blk_cls] + counts[blk_cls] - first, 0, bm)
    n_valid = jnp.where(bidx < blk_end[-1], n_valid, 0).astype(jnp.int32)
    src = jnp.clip(first[:, None] + jnp.arange(bm, dtype=jnp.int32)[None, :], 0, T - 1)
    row_tok = order[src]
    pa, pb = np.triu_indices(EXPERTS_PER_GROUP, k=1)
    grp_of = np.repeat(np.arange(N_GROUPS), PAIRS_PER_GROUP) * EXPERTS_PER_GROUP
    lo_tab = jnp.asarray((grp_of + np.tile(pa, N_GROUPS)).astype(np.int32))
    hi_tab = jnp.asarray((grp_of + np.tile(pb, N_GROUPS)).astype(np.int32))
    blk_lo = lo_tab[blk_cls]
    blk_hi = hi_tab[blk_cls]

    fg = (final_g if final_g is not None else jnp.ones((D,), F32)).reshape(1, D).astype(F32)
    w1b, w3b, w2b = w1.astype(BF16), w3.astype(BF16), w2.astype(BF16)
    const = lambda shape: pl.BlockSpec(shape, lambda i, lo, hi, nv: (0,) * len(shape))
    wspec = lambda shape, which: pl.BlockSpec(
        shape, (lambda i, lo, hi, nv: (lo[i], 0, 0)) if which == 0 else (lambda i, lo, hi, nv: (hi[i], 0, 0)))
    return pl.pallas_call(
        functools.partial(_pair_expert_kernel, bm=bm, final=final_g is not None),
        grid_spec=pltpu.PrefetchScalarGridSpec(
            num_scalar_prefetch=3,
            grid=(n_blk,),
            in_specs=[pl.BlockSpec(memory_space=pl.ANY),
                      pl.BlockSpec(memory_space=pl.ANY),
                      const((1, D)), const((D, LANES)), const((1, LANES)), const((1, D)),
                      wspec((1, D, D_EXPERT), 0), wspec((1, D, D_EXPERT), 0), wspec((1, D_EXPERT, D), 0),
                      wspec((1, D, D_EXPERT), 1), wspec((1, D, D_EXPERT), 1), wspec((1, D_EXPERT, D), 1)],
            out_specs=pl.BlockSpec(memory_space=pl.ANY),
            scratch_shapes=[pltpu.SMEM((bm,), jnp.int32), pltpu.VMEM((bm, D), F32), pltpu.VMEM((bm, D), F32),
                            pltpu.SemaphoreType.DMA(()), pltpu.SemaphoreType.DMA(()), pltpu.SemaphoreType.DMA(())]),
        out_shape=jax.ShapeDtypeStruct((T, D), F32),
        compiler_params=_cparams("arbitrary"),
        name="moe_pair_experts",
    )(blk_lo, blk_hi, n_valid, row_tok, xt, g2, wr, br, fg, w1b, w3b, w2b, w1b, w3b, w2b)


GDN_TB = 512
GDN_INV_PASSES = 1
GDN_UNROLL_B = 8


def _gdn_prep_kernel(prev_ref, cur_ref, next_ref, ba_ref, cw_ref, ones_ref, alog_ref, dt_ref, qkv_ref, bg_ref, *, nt):
    qkv = _silu(_conv4(prev_ref, cur_ref, next_ref, cw_ref[...], nt))
    ones_bd = ones_ref[...]
    w = GDN_QK_W
    q = qkv[:, 0:w]
    k = qkv[:, w:2 * w]
    qn = q * lax.rsqrt(_dot_split(q * q, ones_bd, n=2) + 1e-6) * GDN_DK ** -0.5
    kn = k * lax.rsqrt(_dot_split(k * k, ones_bd, n=2) + 1e-6)
    qkv_ref[0] = jnp.concatenate([qn, kn, qkv[:, 2 * w:]], axis=-1)
    ba = ba_ref[0]
    beta = _sigmoid(ba)
    log_a = -jnp.exp(alog_ref[...]) * _softplus(ba + dt_ref[...])
    lane = lax.broadcasted_iota(jnp.int32, ba.shape, 1)
    h = GDN_HEADS
    for d in range(2):
        b_d = pltpu.roll(beta, (LANES - d * h) % LANES, 1)
        a_d = pltpu.roll(log_a, (LANES - (2 * h + d * h) + h) % LANES, 1)
        bg_ref[d, 0] = jnp.where(lane < h, b_d, jnp.where(lane < 2 * h, a_d, 0.0))


def _gdn_chunk_kernel(qkv_ref, bg_ref, tri_ref, tribd_ref, o_ref, st_ref, *, nb):
    c = GDN_CHUNK
    H = GDN_HEADS
    n = H * c

    @pl.when(pl.program_id(1) == 0)
    def _():
        st_ref[...] = jnp.zeros(st_ref.shape, F32)

    tri_bf = tri_ref[0].astype(BF16)
    incl = tribd_ref[0]
    ri = lax.broadcasted_iota(jnp.int32, (n, n), 0)
    ci = lax.broadcasted_iota(jnp.int32, (n, n), 1)
    eye = jnp.where(ri == ci, 1.0, 0.0)
    strict = incl - eye
    bdmask = jnp.where(ri // c == ci // c, 1.0, 0.0)

    def stack(x):
        w = x.shape[1] // H
        return jnp.concatenate([x[:, hh * w:(hh + 1) * w] for hh in range(H)], axis=0)

    def col(x, lane0):
        return jnp.concatenate([x[:, lane0 + hh:lane0 + hh + 1] for hh in range(H)], axis=0)

    def bd(x_st):
        return jnp.concatenate([x_st] * H, axis=1) * bdmask

    def mm(a, b, passes, dot=_dot_nn):
        if passes == 1:
            return dot(a.astype(BF16), b.astype(BF16))
        return _dot_f32(a, b, dot=dot)

    def one_batch(b):
        qkv = qkv_ref[b]
        bg = bg_ref[0, b]
        q_st = stack(qkv[:, 0:GDN_QK_W])
        k_st = stack(qkv[:, GDN_QK_W:2 * GDN_QK_W])
        v_st = stack(qkv[:, 2 * GDN_QK_W:])
        acc = None
        for p in _split_bf16(bg, 3):
            t = _dot_nn(tri_bf, p)
            acc = t if acc is None else acc + t
        gam = acc
        tot = jnp.sum(bg, axis=0, keepdims=True)
        g_col = col(gam, H)
        b_col = col(bg, 0)
        gam_t = jnp.concatenate([gam, jnp.zeros_like(gam)], axis=0).T
        g_row = jnp.concatenate([gam_t[H + hh:H + hh + 1, 0:c] for hh in range(H)], axis=1)
        end_col = jnp.concatenate([jnp.broadcast_to(tot[:, H + hh:H + hh + 1], (c, 1)) for hh in range(H)], axis=0)
        exp_g = jnp.exp(g_col)
        decay = jnp.exp(jnp.minimum(g_col - g_row, 0.0)) * incl
        k_bf = k_st.astype(BF16)
        kk = _dot_nt(k_bf, k_bf)
        a_mat = strict * b_col * kk * decay
        p_inv = eye - a_mat
        x_pow = a_mat
        for _ in range(int(math.log2(c)) - 1):
            x_pow = mm(x_pow, x_pow, GDN_INV_PASSES)
            p_inv = p_inv + mm(p_inv, x_pow, GDN_INV_PASSES)
        rhs = jnp.concatenate([v_st * b_col, k_st * (b_col * exp_g)], axis=1)
        p_bf = p_inv.astype(BF16)
        sol = _dot_nn(p_bf, rhs.astype(BF16))
        resid = rhs - sol - _dot_f32(a_mat, sol)
        sol = sol + _dot_nn(p_bf, resid.astype(BF16))
        u_st = sol[:, 0:c]
        kc_bd = bd(sol[:, c:2 * c])
        qk = _dot_nt(q_st.astype(BF16), k_bf) * decay
        qd_bd = bd(q_st * exp_g)
        kd_bd = bd(k_st * jnp.exp(end_col - g_col))
        st = st_ref[b]
        st_bf = st.astype(BF16)
        v_new = u_st - _dot_nn(kc_bd.astype(BF16), st_bf)
        v_new_bf = v_new.astype(BF16)
        o_st = _dot_nn(qd_bd.astype(BF16), st_bf) + _dot_nn(qk.astype(BF16), v_new_bf)
        st_ref[b] = jnp.exp(end_col) * st + _dot_tn(kd_bd.astype(BF16), v_new_bf)
        o_ref[0, b] = jnp.concatenate([o_st[hh * c:(hh + 1) * c] for hh in range(H)], axis=1)

    def pair(i, carry):
        for j in range(GDN_UNROLL_B):
            one_batch(i * GDN_UNROLL_B + j)
        return carry

    lax.fori_loop(0, nb // GDN_UNROLL_B, pair, 0)


def _gdn_chunk_compact_kernel(qkv_ref, bg_ref, tri_ref, tril_ref, sel_ref, o_ref, st_ref, *, nb):
    c = GDN_CHUNK
    H = GDN_HEADS
    n = H * c

    @pl.when(pl.program_id(1) == 0)
    def _():
        st_ref[...] = jnp.zeros(st_ref.shape, F32)

    tri_bf = tri_ref[0].astype(BF16)
    incl = tril_ref[0]
    ri = lax.broadcasted_iota(jnp.int32, (c, n), 0)
    ci = lax.broadcasted_iota(jnp.int32, (c, n), 1)
    eye = jnp.where(ri == ci % c, 1.0, 0.0)
    strict = incl - eye
    rb = lax.broadcasted_iota(jnp.int32, (n, n), 0) // c
    cb = lax.broadcasted_iota(jnp.int32, (n, n), 1) // c
    bdmask = rb == cb
    sel_g = sel_ref[0]
    sel_b = sel_ref[1]

    def bd(x):
        return jnp.where(bdmask, jnp.concatenate([x] * H, axis=0), 0.0)

    def one_batch(b):
        qkv = qkv_ref[b]
        bg = bg_ref[0, b]
        q = qkv[:, 0:n]
        k = qkv[:, n:2 * n]
        v = qkv[:, 2 * n:]
        gam = _dot_split(bg, None, dot=lambda p, _: _dot_nn(tri_bf, p))
        k_bf = k.astype(BF16)
        kq = _dot_nt(jnp.concatenate([k_bf, q.astype(BF16)], axis=0), bd(k).astype(BF16))
        bexp = _dot_split(bg, sel_b)
        yield
        tot = jnp.sum(bg, axis=0, keepdims=True)
        gt = jnp.concatenate([gam, jnp.broadcast_to(tot, (8, LANES))], axis=0)
        ge = _dot_split(gt, sel_g)
        yield
        gexp = ge[0:c]
        end_row = ge[c:c + 1]
        gam_t = jnp.concatenate([gam, jnp.zeros_like(gam)], axis=0).T
        g_row = jnp.concatenate([gam_t[H + hh:H + hh + 1, 0:c] for hh in range(H)], axis=1)
        decay = jnp.exp(jnp.minimum(gexp - g_row, 0.0)) * incl
        exp_g = jnp.exp(gexp)
        a_c = strict * bexp * kq[0:c] * decay
        qk = kq[c:2 * c] * decay
        p_c = eye - a_c
        x_c = _dot_nn(a_c.astype(BF16), bd(a_c).astype(BF16))
        yield
        n_sq = int(math.log2(c)) - 1
        for it in range(n_sq):
            r_bd = bd(x_c).astype(BF16)
            if it + 1 < n_sq:
                both = _dot_nn(jnp.concatenate([x_c, p_c], axis=0).astype(BF16), r_bd)
                yield
                x_c = both[0:c]
                p_c = p_c + both[c:2 * c]
            else:
                dp = _dot_nn(p_c.astype(BF16), r_bd)
                yield
                p_c = p_c + dp
        ap = _dot_f32(a_c, bd(p_c))
        yield
        resid = eye - p_c - ap
        p_bf = p_c.astype(BF16)
        dp = _dot_nn(p_bf, bd(resid).astype(BF16))
        yield
        p_c = p_c + dp
        p_bf = p_c.astype(BF16)
        u = _dot_nn(p_bf, bd(v * bexp).astype(BF16))
        kc = _dot_nn(p_bf, bd(k * (bexp * exp_g)).astype(BF16))
        yield
        st = st_ref[b]
        st_bf = st.astype(BF16)
        sq = _dot_nn(jnp.concatenate([kc, q * exp_g], axis=0).astype(BF16), st_bf)
        yield
        v_new = u - sq[0:c]
        o_ref[0, b] = sq[c:2 * c] + _dot_nn(qk.astype(BF16), bd(v_new).astype(BF16))
        kd = (k * jnp.exp(end_row - gexp)).astype(BF16)
        upd = _dot_tn(kd, v_new.astype(BF16))
        st_ref[b] = st * jnp.exp(end_row) + jnp.where(bdmask, upd, 0.0)

    n_side = math.gcd(nb, GDN_UNROLL_B)

    def group(i, carry):
        chains = [one_batch(i * n_side + j) for j in range(n_side)]
        live = True
        while live:
            live = False
            for ch in chains:
                try:
                    next(ch)
                    live = True
                except StopIteration:
                    pass
        return carry

    if nb == n_side:
        group(0, 0)
    else:
        lax.fori_loop(0, nb // n_side, group, 0)


def gdn_pallas(nqkv, nba, conv_w, a_log, dt_bias):
    B, S, W = nqkv.shape
    tb = min(GDN_TB, S)
    nt = S // tb
    c = GDN_CHUNK
    nc = S // c
    H = GDN_HEADS
    r = np.arange(GDN_QK_W) // GDN_DK
    ones_bd = jnp.asarray((r[:, None] == r[None, :]).astype(np.float32)).astype(BF16)
    alog = jnp.zeros((1, LANES), F32).at[0, 2 * H:4 * H].set(a_log.astype(F32).reshape(-1))
    dt = jnp.zeros((1, LANES), F32).at[0, 2 * H:4 * H].set(dt_bias.astype(F32).reshape(-1))
    full = lambda shape: pl.BlockSpec(shape, lambda b, i: (0,) * len(shape))
    qkvn, bg = pl.pallas_call(
        functools.partial(_gdn_prep_kernel, nt=nt),
        grid=(B, nt),
        in_specs=_halo_specs(tb, W, nt) + [pl.BlockSpec((1, tb, LANES), lambda b, i: (b, i, 0)),
                                           full((CONV_WIDTH, W)), full((GDN_QK_W, GDN_QK_W)),
                                           full((1, LANES)), full((1, LANES))],
        out_specs=[pl.BlockSpec((1, tb, W), lambda b, i: (b, i, 0)),
                   pl.BlockSpec((2, 1, tb, LANES), lambda b, i: (0, b, i, 0))],
        out_shape=[jax.ShapeDtypeStruct((B, S, W), F32), jax.ShapeDtypeStruct((2, B, S, LANES), F32)],
        compiler_params=_cparams("parallel", "parallel"),
        name="gdn_prep",
    )(nqkv, nqkv, nqkv, nba, conv_w.astype(F32), ones_bd, alog, dt)

    t = np.arange(c)
    tri_np = np.stack([t[:, None] >= t[None, :], t[:, None] <= t[None, :]]).astype(np.float32)
    tri = jnp.asarray(tri_np)
    tri_lanes = jnp.asarray(np.tile(tri_np, (1, 1, H)))
    lane = np.arange(LANES)[:, None]
    head = (np.arange(H * c) // c)[None, :]
    sel = jnp.asarray(np.stack([lane == H + head, lane == head]).astype(np.float32)).astype(BF16)

    def tmap(d, i):
        return i + d * (nc - 1 - 2 * i)

    return pl.pallas_call(
        functools.partial(_gdn_chunk_compact_kernel, nb=B),
        grid=(2, nc),
        in_specs=[pl.BlockSpec((B, c, W), lambda d, i: (0, tmap(d, i), 0)),
                  pl.BlockSpec((1, B, c, LANES), lambda d, i: (d, 0, tmap(d, i), 0)),
                  pl.BlockSpec((1, c, c), lambda d, i: (d, 0, 0)),
                  pl.BlockSpec((1, c, H * c), lambda d, i: (d, 0, 0)),
                  pl.BlockSpec((2, LANES, H * c), lambda d, i: (0, 0, 0))],
        out_specs=pl.BlockSpec((1, B, c, GDN_V_W), lambda d, i: (d, 0, tmap(d, i), 0)),
        out_shape=jax.ShapeDtypeStruct((2, B, S, GDN_V_W), F32),
        scratch_shapes=[pltpu.VMEM((B, H * GDN_DK, H * GDN_DV), F32)],
        compiler_params=_cparams("arbitrary", "arbitrary"),
        name="gdn_chunk",
    )(qkvn, bg, tri, tri_lanes, sel)


def gdn_scan(q, k, v, beta, log_a):
    b_, s_, h_, dk = q.shape
    dv = v.shape[-1]
    c = GDN_CHUNK
    qc, kc, vc, bc, gc = (_to_chunks(t, c) for t in (q, k, v, beta, log_a))
    gam = jnp.cumsum(gc, axis=-1)
    diff = gam[..., :, None] - gam[..., None, :]
    incl = jnp.tril(jnp.ones((c, c), dtype=bool))
    strict = jnp.tril(jnp.ones((c, c), dtype=bool), k=-1)
    decay = jnp.exp(jnp.where(incl, diff, -jnp.inf))
    kk = jnp.einsum('bhntd,bhnsd->bhnts', kc, kc)
    a_mat = jnp.where(strict, bc[..., None] * kk * decay, 0.0) + jnp.eye(c, dtype=jnp.float32)
    rhs = jnp.concatenate([vc * bc[..., None], kc * (bc * jnp.exp(gam))[..., None]], axis=-1)
    sol = lax.linalg.triangular_solve(a_mat, rhs, left_side=True, lower=True, unit_diagonal=True)
    u_val, k_cum = sol[..., :dv], sol[..., dv:]
    qk = jnp.einsum('bhntd,bhnsd->bhnts', qc, kc) * decay
    q_dec = qc * jnp.exp(gam)[..., None]
    k_dec = kc * jnp.exp(gam[..., -1:] - gam)[..., None]
    c_dec = jnp.exp(gam[..., -1])

    def step(state, inp):
        u_i, kc_i, qk_i, qd_i, kd_i, cd_i = inp
        v_new = u_i - jnp.einsum('bhtd,bhde->bhte', kc_i, state)
        o = jnp.einsum('bhtd,bhde->bhte', qd_i, state) + jnp.einsum('bhts,bhse->bhte', qk_i, v_new)
        state = cd_i[..., None, None] * state + jnp.einsum('bhsd,bhse->bhde', kd_i, v_new)
        return state, o

    xs = tuple(jnp.moveaxis(t, 2, 0) for t in (u_val, k_cum, qk, q_dec, k_dec, c_dec))
    _, o = lax.scan(step, jnp.zeros((b_, h_, dk, dv), jnp.float32), xs)
    return _from_chunks(o)


def gdn_mixer(qkv, z, ba, conv_w, a_log, dt_bias, norm_g):
    B, S, _ = qkv.shape
    f32 = jnp.float32
    qkv = jax.nn.silu(dwconv_centred(qkv, conv_w)).astype(f32)
    q, k, v = jnp.split(qkv, [GDN_QK_W, 2 * GDN_QK_W], axis=-1)
    q = l2norm(q.reshape(B, S, GDN_HEADS, GDN_DK)) * GDN_DK ** -0.5
    k = l2norm(k.reshape(B, S, GDN_HEADS, GDN_DK))
    v = v.reshape(B, S, GDN_HEADS, GDN_DV)
    ba = ba.astype(f32).reshape(B, S, 4, GDN_HEADS)
    beta = jax.nn.sigmoid(ba[:, :, :2])
    log_a = -jnp.exp(a_log.astype(f32)) * jax.nn.softplus(ba[:, :, 2:] + dt_bias.astype(f32))
    o_f = gdn_scan(q, k, v, beta[:, :, 0], log_a[:, :, 0])
    o_b = _flip(gdn_scan(_flip(q), _flip(k), _flip(v), _flip(beta[:, :, 1]), _flip(log_a[:, :, 1])))
    o = rmsnorm(o_f + o_b, norm_g) * jax.nn.silu(z.astype(f32)).reshape(B, S, GDN_HEADS, GDN_DV)
    return o.reshape(B, S, GDN_V_W).astype(z.dtype)


def gated_merge(h, branches, merge_w, merge_b, branch_up):
    out = None
    for i, y in enumerate(branches):
        term = jax.nn.sigmoid(h @ merge_w[i] + merge_b[i]) * (y @ branch_up[i])
        out = term if out is None else out + term
    return out


def memory_cross_attention(h, mem_n, w_q, w_kv, w_o):
    B, S, D = h.shape
    M = mem_n.shape[1]
    q = (h @ w_q).reshape(B, S, XATTN_HEADS, XATTN_DH)
    kv = (mem_n @ w_kv).reshape(B, M, 2, XATTN_HEADS, XATTN_DH)
    k, v = kv[:, :, 0], kv[:, :, 1]
    s = jnp.einsum('bshd,bmhd->bhsm', q, k).astype(jnp.float32) * XATTN_DH ** -0.5
    p = jax.nn.softmax(s, axis=-1)
    o = jnp.einsum('bhsm,bmhd->bshd', p.astype(v.dtype), v).reshape(B, S, D)
    return o @ w_o


def hier_moe(h, w_group, b_group, w_expert, b_expert, w1, w3, w2):
    B, S, D = h.shape
    T = B * S
    f32 = jnp.float32
    ht = h.reshape(T, D)
    g_logits = (ht @ w_group).astype(f32) + b_group.astype(f32)
    g_prob = jax.nn.softmax(g_logits, axis=-1)
    _, g_idx = lax.top_k(g_logits, 1)
    g_w = jnp.take_along_axis(g_prob, g_idx, axis=1)
    e_logits = ((ht @ w_expert).astype(f32) + b_expert.astype(f32)).reshape(T, N_GROUPS, EXPERTS_PER_GROUP)
    e_logits = jnp.take_along_axis(
        e_logits, jnp.broadcast_to(g_idx[:, :, None], (T, 1, EXPERTS_PER_GROUP)), axis=1)[:, 0]
    e_top, e_idx = lax.top_k(e_logits, TOP_K)
    gate = jax.nn.softmax(e_top, axis=-1) * g_w
    eid = (g_idx * EXPERTS_PER_GROUP + e_idx).reshape(-1)
    tok = jnp.repeat(jnp.arange(T, dtype=jnp.int32), TOP_K)
    wts = gate.reshape(-1)
    A = T * TOP_K
    n_blk = -(-A // MOE_BLOCK) + N_EXPERTS
    P = n_blk * MOE_BLOCK
    order = jnp.argsort(eid)
    eid_s, tok_s, w_s = eid[order], tok[order], wts[order]
    counts = jnp.zeros((N_EXPERTS,), jnp.int32).at[eid].add(1)
    padded = (counts + MOE_BLOCK - 1) // MOE_BLOCK * MOE_BLOCK
    pad_end = jnp.cumsum(padded)
    pad_start = pad_end - padded
    seg_start = jnp.cumsum(counts) - counts
    dest = pad_start[eid_s] + jnp.arange(A, dtype=jnp.int32) - seg_start[eid_s]
    row_tok = jnp.full((P,), T, jnp.int32).at[dest].set(tok_s)
    row_w = jnp.zeros((P,), f32).at[dest].set(w_s)
    blk_expert = jnp.minimum(
        jnp.searchsorted(pad_end, jnp.arange(n_blk, dtype=jnp.int32) * MOE_BLOCK, side='right'),
        N_EXPERTS - 1)
    h_pad = jnp.concatenate([ht, jnp.zeros((1, D), ht.dtype)], axis=0)
    xb = h_pad[row_tok].reshape(n_blk, MOE_BLOCK, D)

    def expert_block(args):
        xi, e = args
        return (jax.nn.silu(xi @ w1[e]) * (xi @ w3[e])) @ w2[e]

    yb = lax.map(expert_block, (xb, blk_expert)).reshape(P, D)
    out = jnp.zeros((T + 1, D), h.dtype).at[row_tok].add(yb * row_w[:, None].astype(yb.dtype))
    return out[:T].reshape(B, S, D)


def kernel(x, mem, mix_norm, w_in, gla_lr_up, gla_lr_bias, gla_norm, lru_conv_w, lru_conv_b, lru_w_a, lru_b_a, lru_w_x, lru_b_x, lru_lambda, diff_lq1, diff_lk1, diff_lq2, diff_lk2, diff_norm, gdn_conv_w, gdn_a_log, gdn_dt_bias, gdn_norm, merge_w, merge_b, branch_up, mix_out, xattn_norm, mem_norm, xattn_wq, xattn_wkv, xattn_wo, moe_norm, moe_w_group, moe_b_group, moe_w_expert, moe_b_expert, moe_w1, moe_w3, moe_w2, final_norm):
    B, S, D = x.shape
    T = B * S
    xt = x.reshape(T, D)
    for l in range(DEPTH):
        gla_in, gog, lx, lg, dq, dk, dv, nqkv, nz, nba = inproj_pallas(xt, mix_norm[l], w_in[l])
        gla_o = gla_pallas(gla_in.reshape(B, S, -1), gla_lr_up[l], gla_lr_bias[l])
        lru_h = lru_pallas(lx.reshape(B, S, -1), lru_conv_w[l], lru_conv_b[l], lru_w_a[l], lru_b_a[l],
                           lru_w_x[l], lru_b_x[l], lru_lambda[l])
        y_diff = diff_attention_pallas(dq.reshape(B, S, -1), dk.reshape(B, S, -1), dv.reshape(B, S, -1),
                                       diff_lq1[l], diff_lk1[l], diff_lq2[l], diff_lk2[l], diff_norm[l],
                                       0.8 - 0.6 * math.exp(-0.3 * l))
        gdn_o = gdn_pallas(nqkv.reshape(B, S, -1), nba.reshape(B, S, -1), gdn_conv_w[l], gdn_a_log[l],
                           gdn_dt_bias[l])
        xt = merge_pallas(xt, gla_o.reshape(2, T, -1), gog, lru_h.reshape(2, T, -1), lg, y_diff.reshape(T, -1),
                          gdn_o.reshape(2, T, -1), nz, mix_norm[l], gla_norm[l], gdn_norm[l],
                          merge_w[l], merge_b[l], branch_up[l], mix_out[l])
        xt = xattn_pallas(xt.reshape(B, S, D), mem, xattn_norm[l], mem_norm[l], xattn_wq[l], xattn_wkv[l],
                          xattn_wo[l]).reshape(T, D)
        xt = moe_pair_pallas(xt, moe_norm[l], moe_w_group[l], moe_b_group[l], moe_w_expert[l], moe_b_expert[l],
                             moe_w1[l], moe_w3[l], moe_w2[l], final_g=final_norm if l == DEPTH - 1 else None)
    return xt.reshape(B, S, D)
```

```python
import functools
import math

import jax
import jax.numpy as jnp
import numpy as np
from jax import lax
from jax.experimental import pallas as pl
from jax.experimental.pallas import tpu as pltpu

D_MODEL = 1024
DEPTH = 2
N_BRANCHES = 4
BRANCH_WIDTH = D_MODEL // 4
RMS_EPS = 1e-6
CONV_WIDTH = 4

GLA_HEADS = 4
GLA_DV = BRANCH_WIDTH // GLA_HEADS
GLA_DK = GLA_DV // 2
GLA_RANK = 16
GLA_GATE_NORM = 16.0
GLA_CHUNK = 64
GLA_QK_W = GLA_HEADS * GLA_DK
GLA_V_W = GLA_HEADS * GLA_DV

LRU_WIDTH = BRANCH_WIDTH
LRU_BLOCKS = 4
LRU_C = 8.0

DIFF_HEADS = 4
DIFF_DV = BRANCH_WIDTH // DIFF_HEADS
DIFF_DH = DIFF_DV // 2
DIFF_QK_W = DIFF_HEADS * 2 * DIFF_DH
DIFF_V_W = DIFF_HEADS * DIFF_DV

GDN_HEADS = 4
GDN_DK = BRANCH_WIDTH // GDN_HEADS
GDN_DV = BRANCH_WIDTH // GDN_HEADS
GDN_CHUNK = 64
GDN_QK_W = GDN_HEADS * GDN_DK
GDN_V_W = GDN_HEADS * GDN_DV

XATTN_HEADS = 4
XATTN_DH = D_MODEL // XATTN_HEADS

N_GROUPS = 4
EXPERTS_PER_GROUP = 8
N_EXPERTS = N_GROUPS * EXPERTS_PER_GROUP
TOP_K = 2
D_EXPERT = D_MODEL // 2

LANES = 128
VMEM_LIMIT = 56 * 1024 * 1024

F32 = jnp.float32
BF16 = jnp.bfloat16
NEG_BIG = -1e30


def _cparams(*sem):
    return pltpu.CompilerParams(dimension_semantics=sem, vmem_limit_bytes=VMEM_LIMIT)


def _split_bf16(a, n):
    parts, r = [], a
    for i in range(n):
        p = r.astype(BF16)
        parts.append(p)
        if i + 1 < n:
            r = r - p.astype(F32)
    return parts


def _split_trunc_bf16(a, n):
    parts, r = [], a
    for i in range(n):
        bits = lax.bitcast_convert_type(r, jnp.uint32) & jnp.uint32(0xFFFF0000)
        p = lax.bitcast_convert_type(bits, F32)
        parts.append(p.astype(BF16))
        if i + 1 < n:
            r = r - p
    return parts


def _dot_nn(a, b):
    return jnp.dot(a, b, preferred_element_type=F32)


def _dot_nt(a, b):
    return lax.dot_general(a, b, (((1,), (1,)), ((), ())), preferred_element_type=F32)


def _dot_tn(a, b):
    return lax.dot_general(a, b, (((0,), (0,)), ((), ())), preferred_element_type=F32)


def _dot_split(a, b_exact, n=3, dot=_dot_nn):
    acc = None
    for p in _split_bf16(a, n):
        t = dot(p, b_exact)
        acc = t if acc is None else acc + t
    return acc


def _dot_f32(a, b, dot=_dot_nn):
    a_hi, a_lo = _split_bf16(a, 2)
    b_hi, b_lo = _split_bf16(b, 2)
    return dot(a_hi, b_hi) + (dot(a_hi, b_lo) + dot(a_lo, b_hi))


def _cumsum_rows(tri_bf, x):
    acc = None
    for p in _split_bf16(x, 3):
        t = _dot_nn(tri_bf, p)
        acc = t if acc is None else acc + t
    return acc


def _rms_rows(x, g):
    return x * lax.rsqrt(jnp.mean(x * x, axis=-1, keepdims=True) + RMS_EPS) * g


def _sigmoid(x):
    return 1.0 / (1.0 + jnp.exp(-x))


def _softplus(x):
    return jnp.maximum(x, 0.0) + jnp.log(1.0 + jnp.exp(-jnp.abs(x)))


def _silu(x):
    return x * _sigmoid(x)


_G_GLA = (0, 640)
_G_GOG = (640, 896)
_G_LX = (896, 1152)
_G_LG = (1152, 1408)
_G_DQ = (1408, 1664)
_G_DK = (1664, 1920)
_G_DV = (1920, 2176)
_G_NQKV = (2176, 2944)
_G_NZ = (2944, 3200)
_G_NBA = (3200, 3328)
IN_PAD_W = 3328
IN_TM = 512


def _permute_w_in(w):
    z = lambda n: jnp.zeros((w.shape[0], n), w.dtype)
    return jnp.concatenate([w[:, 0:512], w[:, 768:800], z(96), w[:, 512:768], w[:, 800:3120], z(112)],
                           axis=1).astype(BF16)


def _inproj_kernel(x_ref, g_ref, w_ref, gla_ref, gog_ref, lx_ref, lg_ref, dq_ref, dk_ref, dv_ref,
                   nqkv_ref, nz_ref, nba_ref):
    h = _rms_rows(x_ref[...], g_ref[...]).astype(BF16)

    def proj(grp):
        return jnp.dot(h, w_ref[:, grp[0]:grp[1]], preferred_element_type=F32)

    gla_ref[...] = proj(_G_GLA)
    gog_ref[...] = proj(_G_GOG).astype(BF16)
    lx_ref[...] = proj(_G_LX)
    lg_ref[...] = proj(_G_LG).astype(BF16)
    dq_ref[...] = (proj(_G_DQ) * (DIFF_DH ** -0.5 * LOG2E)).astype(BF16)
    dk_ref[...] = proj(_G_DK).astype(BF16)
    dv_ref[...] = proj(_G_DV).astype(BF16)
    nqkv_ref[...] = proj(_G_NQKV)
    nz_ref[...] = proj(_G_NZ).astype(BF16)
    nba_ref[...] = proj(_G_NBA)


def inproj_pallas(xt, g, w_in):
    T, D = xt.shape
    tm = min(IN_TM, T)
    groups = [(_G_GLA, F32), (_G_GOG, BF16), (_G_LX, F32), (_G_LG, BF16), (_G_DQ, BF16), (_G_DK, BF16),
              (_G_DV, BF16), (_G_NQKV, F32), (_G_NZ, BF16), (_G_NBA, F32)]
    return pl.pallas_call(
        _inproj_kernel,
        grid=(T // tm,),
        in_specs=[pl.BlockSpec((tm, D), lambda i: (i, 0)),
                  pl.BlockSpec((1, D), lambda i: (0, 0)),
                  pl.BlockSpec((D, IN_PAD_W), lambda i: (0, 0))],
        out_specs=[pl.BlockSpec((tm, b - a), lambda i: (i, 0)) for (a, b), _ in groups],
        out_shape=[jax.ShapeDtypeStruct((T, b - a), dt) for (a, b), dt in groups],
        compiler_params=_cparams("parallel"),
        name="inproj",
    )(xt, g.reshape(1, D).astype(F32), _permute_w_in(w_in))


def _gla_kernel(x_ref, tri_ref, w_ref, b_ref, o_ref, st_ref, *, nb):
    c = GLA_CHUNK

    @pl.when(pl.program_id(1) == 0)
    def _():
        st_ref[...] = jnp.zeros(st_ref.shape, F32)

    tri = tri_ref[0]
    tri_bf = tri.astype(BF16)
    tri4 = jnp.concatenate([tri] * GLA_HEADS, axis=0)
    w = w_ref[0]
    bias = b_ref[0]
    lane_qk = lax.broadcasted_iota(jnp.int32, (c, GLA_QK_W), 1) // GLA_DK
    lane_v = lax.broadcasted_iota(jnp.int32, (c, GLA_V_W), 1) // GLA_DV
    row_s = lax.broadcasted_iota(jnp.int32, (GLA_V_W, GLA_QK_W), 0) // GLA_DV
    col_s = lax.broadcasted_iota(jnp.int32, (GLA_V_W, GLA_QK_W), 1) // GLA_DK
    st_mask = row_s == col_s

    def one_batch(b):
        blk = x_ref[b]
        q = blk[:, 0:128] * GLA_DK ** -0.5
        k = blk[:, 128:256]
        v = blk[:, 256:512].astype(BF16)
        lr = blk[:, 512:640]
        z = _dot_f32(lr, w) + bias
        yield
        g = (jnp.minimum(z, 0.0) - jnp.log(1.0 + jnp.exp(-jnp.abs(z)))) * (1.0 / GLA_GATE_NORM)
        bc = _cumsum_rows(tri_bf, g)
        yield
        tot = jnp.sum(g, axis=0, keepdims=True)
        ref = 0.5 * tot
        qt = q * jnp.exp(bc - ref)
        kt = (k * jnp.exp(ref - bc)).astype(BF16)
        qd = (q * jnp.exp(bc)).astype(BF16)
        kd = (k * jnp.exp(tot - bc)).astype(BF16)
        qstack = jnp.concatenate(
            [jnp.where(lane_qk == hh, qt, 0.0) for hh in range(GLA_HEADS)], axis=0).astype(BF16)
        s = _dot_nt(qstack, kt)
        st = st_ref[b]
        o = _dot_nt(qd, st.astype(BF16))
        upd = _dot_tn(v, kd)
        yield
        s = (s * tri4).astype(BF16)
        ohs = [_dot_nn(s[hh * c:(hh + 1) * c], v) for hh in range(GLA_HEADS)]
        yield
        for hh in range(GLA_HEADS):
            o = o + jnp.where(lane_v == hh, ohs[hh], 0.0)
        o_ref[0, b] = o
        st_ref[b] = st * jnp.exp(tot) + jnp.where(st_mask, upd, 0.0)

    chains = [one_batch(b) for b in range(nb)]
    live = True
    while live:
        live = False
        for ch in chains:
            try:
                next(ch)
                live = True
            except StopIteration:
                pass


def gla_pallas(gla_in, lr_up, lr_bias):
    B, S, W = gla_in.shape
    c = GLA_CHUNK
    nc = S // c
    r = np.arange(c)
    tri = jnp.asarray(np.stack([r[:, None] >= r[None, :], r[:, None] <= r[None, :]]).astype(np.float32))
    w = jnp.zeros((2, LANES, GLA_QK_W), F32)
    w = w.at[0, 0:GLA_RANK].set(lr_up[0].astype(F32)).at[1, GLA_RANK:2 * GLA_RANK].set(lr_up[1].astype(F32))
    bias = lr_bias.astype(F32).reshape(2, 1, GLA_QK_W)

    def tmap(d, i):
        return (0, i + d * (nc - 1 - 2 * i), 0)

    return pl.pallas_call(
        functools.partial(_gla_kernel, nb=B),
        grid=(2, nc),
        in_specs=[pl.BlockSpec((B, c, W), tmap),
                  pl.BlockSpec((1, c, c), lambda d, i: (d, 0, 0)),
                  pl.BlockSpec((1, LANES, GLA_QK_W), lambda d, i: (d, 0, 0)),
                  pl.BlockSpec((1, 1, GLA_QK_W), lambda d, i: (d, 0, 0))],
        out_specs=pl.BlockSpec((1, B, c, GLA_V_W), lambda d, i: (d,) + tmap(d, i)),
        out_shape=jax.ShapeDtypeStruct((2, B, S, GLA_V_W), F32),
        scratch_shapes=[pltpu.VMEM((B, GLA_V_W, GLA_QK_W), F32)],
        compiler_params=_cparams("arbitrary", "arbitrary"),
        name="gla",
    )(gla_in, tri, w, bias)


LRU_TB = 512
HALO = 8


def _halo_specs(tb, width, nt):
    r = tb // HALO
    return [pl.BlockSpec((1, HALO, width), lambda b, i: (b, jnp.maximum(i * r - 1, 0), 0)),
            pl.BlockSpec((1, tb, width), lambda b, i: (b, i, 0)),
            pl.BlockSpec((1, HALO, width), lambda b, i: (b, jnp.minimum((i + 1) * r, nt * r - 1), 0))]


def _conv4(prev_ref, cur_ref, next_ref, w, nt):
    i = pl.program_id(1)
    cur = cur_ref[0]
    tb = cur.shape[0]
    prev = prev_ref[0] * jnp.where(i > 0, 1.0, 0.0)
    nxt = next_ref[0] * jnp.where(i < nt - 1, 1.0, 0.0)
    ext = jnp.concatenate([prev, cur, nxt], axis=0)
    out = None
    for j in range(CONV_WIDTH):
        off = HALO + j - CONV_WIDTH // 2
        t = ext[off:off + tb] * w[j:j + 1]
        out = t if out is None else out + t
    return out


def _lru_prep_kernel(prev_ref, cur_ref, next_ref, cw_ref, cb_ref, wg_ref, bg_ref, lam_ref, a_ref, x_ref, *, nt):
    u = _conv4(prev_ref, cur_ref, next_ref, cw_ref[...], nt) + cb_ref[...]
    gates = _sigmoid(jnp.dot(u.astype(BF16), wg_ref[...], preferred_element_type=F32) + bg_ref[...])
    w = LRU_WIDTH
    for d in range(2):
        r = gates[:, (2 * d) * w:(2 * d + 1) * w]
        ig = gates[:, (2 * d + 1) * w:(2 * d + 2) * w]
        log_a = -LRU_C * r * _softplus(-lam_ref[d:d + 1])
        a_ref[d, 0] = jnp.exp(log_a)
        x_ref[d, 0] = jnp.sqrt(1.0 - jnp.exp(2.0 * log_a)) * (ig * u)


def _lru_scan_kernel(a_ref, x_ref, h_ref, st_ref, *, tb):
    d = pl.program_id(0)

    @pl.when(pl.program_id(1) == 0)
    def _():
        st_ref[...] = jnp.zeros(st_ref.shape, F32)

    def body(t, h):
        tt = t + d * (tb - 1 - 2 * t)
        h = a_ref[0, :, pl.ds(tt, 1), :] * h + x_ref[0, :, pl.ds(tt, 1), :]
        h_ref[0, :, pl.ds(tt, 1), :] = h
        return h

    st_ref[...] = lax.fori_loop(0, tb, body, st_ref[...], unroll=8)


def lru_pallas(lx, conv_w, conv_b, w_a, b_a, w_x, b_x, lam):
    B, S, W = lx.shape
    tb = min(LRU_TB, S)
    nt = S // tb
    blk = W // LRU_BLOCKS

    def dense(wb):
        m = jnp.zeros((W, W), F32)
        for n in range(LRU_BLOCKS):
            m = m.at[n * blk:(n + 1) * blk, n * blk:(n + 1) * blk].set(wb[n].astype(F32))
        return m

    wg = jnp.concatenate([dense(w_a[0]), dense(w_x[0]), dense(w_a[1]), dense(w_x[1])], axis=1).astype(BF16)
    bg = jnp.concatenate([b_a[0], b_x[0], b_a[1], b_x[1]]).astype(F32).reshape(1, 4 * W)
    full = lambda shape: pl.BlockSpec(shape, lambda b, i: (0,) * len(shape))
    a, xin = pl.pallas_call(
        functools.partial(_lru_prep_kernel, nt=nt),
        grid=(B, nt),
        in_specs=_halo_specs(tb, W, nt) + [full((CONV_WIDTH, W)), full((1, W)), full((W, 4 * W)), full((1, 4 * W)),
                                           full((2, W))],
        out_specs=[pl.BlockSpec((2, 1, tb, W), lambda b, i: (0, b, i, 0))] * 2,
        out_shape=[jax.ShapeDtypeStruct((2, B, S, W), F32)] * 2,
        compiler_params=_cparams("parallel", "parallel"),
        name="lru_prep",
    )(lx, lx, lx, conv_w.astype(F32), conv_b.astype(F32).reshape(1, W), wg, bg, lam.astype(F32))

    def tmap(d, i):
        return (d, 0, i + d * (nt - 1 - 2 * i), 0)

    return pl.pallas_call(
        functools.partial(_lru_scan_kernel, tb=tb),
        grid=(2, nt),
        in_specs=[pl.BlockSpec((1, B, tb, W), tmap)] * 2,
        out_specs=pl.BlockSpec((1, B, tb, W), tmap),
        out_shape=jax.ShapeDtypeStruct((2, B, S, W), F32),
        scratch_shapes=[pltpu.VMEM((B, 1, W), F32)],
        compiler_params=_cparams("arbitrary", "arbitrary"),
        name="lru_scan",
    )(a, xin)


DIFF_TQ = 256
DIFF_TK = 512
DIFF_POS_LANE = 2 * DIFF_DH
DIFF_POS_PIECES = 3
DIFF_UNROLL = 2
DIFF_UNDERFLOW = 160.0
DIFF_VT_ROWS = DIFF_DV + 8
LOG2E = 1.4426950408889634


def _diff_attn_kernel(slopes_ref, islopes_ref, lam_ref, q_ref, k_ref, vt_ref, g_ref, o_ref,
                      qv_scr, rt_scr, s_scr, mb_scr, ro_scr, p_scr, al_scr, m_scr, acc_scr, kn_scr,
                      *, tq, tk, seq, out_scale):
    h = pl.program_id(1)
    qi = pl.program_id(2)
    slope = slopes_ref[h]
    lam = lam_ref[0]
    nk = seq // tk

    q = q_ref[0, 0]
    lane = lax.broadcasted_iota(jnp.int32, q.shape, 1)
    zero = jnp.zeros_like(q)
    q1 = jnp.where(jnp.abs(2 * lane - (DIFF_DH + DIFF_POS_LANE - 1)) < DIFF_DH, zero, q)
    q2 = jnp.where(lane < DIFF_DH, zero, q)
    qs = jnp.concatenate([q1, q2], axis=0)
    lane2 = lax.broadcasted_iota(jnp.int32, qs.shape, 1)
    col = lax.broadcasted_iota(jnp.int32, (1, 2 * tq), 1)
    qpos = qi * tq + jnp.where(col >= tq, col - tq, col)
    shift_q = slope * qpos.astype(F32)
    qv_scr[0] = qs
    qv_scr[1] = jnp.where(lane2 >= DIFF_POS_LANE, -qs, qs)
    rt_scr[0] = -shift_q
    rt_scr[1] = shift_q

    @pl.when(qi == 0)
    def _():
        def chunk(i, best):
            kc = k_ref[0, 0, pl.ds(pl.multiple_of(i * tk, tk), tk), :].astype(F32)
            lk = lax.broadcasted_iota(jnp.int32, kc.shape, 1)
            k2 = kc * kc
            n1 = jnp.sum(jnp.where(lk < DIFF_DH, k2, 0.0), axis=1, keepdims=True)
            n2 = jnp.sum(jnp.where(jnp.abs(2 * lk - (DIFF_DH + DIFF_POS_LANE - 1)) < DIFF_DH, k2, 0.0),
                         axis=1, keepdims=True)
            return jnp.maximum(best, jnp.max(jnp.maximum(n1, n2), axis=0, keepdims=True))

        best = lax.fori_loop(0, nk, chunk, jnp.zeros((1, 1), F32))
        kn_scr[...] = jnp.broadcast_to(jnp.sqrt(best), kn_scr.shape)

    kd = (qi * tq) // tk
    k0 = pl.multiple_of(kd * tk, tk)
    q_mid = jnp.where(lane2 >= DIFF_POS_LANE, jnp.zeros_like(qs), qs)
    s = _dot_nt(k_ref[0, 0, pl.ds(k0, tk), :], q_mid)
    kpos = k0 + lax.broadcasted_iota(jnp.int32, (tk, 2 * tq), 0)
    s = s - slope * jnp.abs(qpos - kpos).astype(F32)
    m0 = jnp.max(s, axis=0, keepdims=True)
    p = jnp.exp2(s - m0).astype(BF16)
    acc_scr[...] = jnp.dot(vt_ref[0, 0, :, pl.ds(k0, tk)], p, preferred_element_type=F32)
    m_scr[...] = m0

    m_min = jnp.min(m0, axis=1, keepdims=True)
    qf = q.astype(F32)
    sq = qf * qf
    qn1 = jnp.sum(jnp.where(lane < DIFF_DH, sq, 0.0), axis=1, keepdims=True)
    qn2 = jnp.sum(jnp.where(jnp.abs(2 * lane - (DIFF_DH + DIFF_POS_LANE - 1)) < DIFF_DH, sq, 0.0),
                  axis=1, keepdims=True)
    qn = jnp.sqrt(jnp.max(jnp.maximum(qn1, qn2), axis=0, keepdims=True))
    reach = (qn * kn_scr[0:1, 0:1] * 1.001 - m_min + DIFF_UNDERFLOW) * islopes_ref[h]
    q_first = jnp.full((1, 1), qi * tq, jnp.int32).astype(F32)
    kd_f = jnp.full((1, 1), kd, jnp.int32).astype(F32)
    lo_f = jnp.ceil((q_first - (tk - 1) - reach) * (1.0 / tk))
    hi_f = jnp.floor((q_first + (tq - 1) + reach) * (1.0 / tk))
    k_lo = jnp.max(jnp.clip(lo_f, 0.0, kd_f).astype(jnp.int32))
    k_hi = jnp.max(jnp.clip(hi_f, kd_f, float(nk - 1)).astype(jnp.int32))
    n_act = k_hi - k_lo

    def tile_of(t):
        kj = k_lo + t
        return jnp.minimum(kj + jnp.where(kj >= kd, 1, 0), nk - 1)

    def scores(t, slot):
        kj = tile_of(t)
        var = jnp.where(kj > kd, 1, 0)
        kt = k_ref[0, 0, pl.ds(pl.multiple_of(kj * tk, tk), tk), :]
        sc = _dot_nt(kt, qv_scr[var])
        s_scr[slot] = sc
        off = rt_scr[var] + jnp.where(t < n_act, 0.0, NEG_BIG)
        ro_scr[slot] = off
        mb_scr[slot] = jnp.max(sc, axis=0, keepdims=True) + off

    def softmax(slot):
        m_old = m_scr[...]
        m_new = jnp.maximum(m_old, mb_scr[slot])
        al_scr[slot] = jnp.exp2(m_old - m_new)
        p_scr[slot] = jnp.exp2(s_scr[slot] - (m_new - ro_scr[slot])).astype(BF16)
        m_scr[...] = m_new

    def pv(t, slot):
        vt = vt_ref[0, 0, :, pl.ds(pl.multiple_of(tile_of(t) * tk, tk), tk)]
        acc_scr[...] = al_scr[slot] * acc_scr[...] + jnp.dot(vt, p_scr[slot], preferred_element_type=F32)

    scores(0, 0)
    scores(1, 1)
    softmax(0)

    def trip(t, slot):
        pv(t, slot)
        scores(t + 2, slot)
        softmax(1 - slot)

    def body(i, c):
        for u in range(DIFF_UNROLL):
            trip(DIFF_UNROLL * i + u, u % 2)
        return c

    lax.fori_loop(0, (n_act + DIFF_UNROLL - 1) // DIFF_UNROLL, body, 0)

    acc = acc_scr[...]
    o2 = acc[0:DIFF_DV] * (1.0 / acc[DIFF_DV:DIFF_DV + 1])
    o = o2[:, :tq] - lam * o2[:, tq:]
    ms = jnp.mean(o * o, axis=0, keepdims=True)
    o = o * lax.rsqrt(ms + RMS_EPS) * g_ref[...] * out_scale
    o_ref[0, 0] = o.astype(o_ref.dtype)


def diff_attention_pallas(dq, dk, dv, lq1, lk1, lq2, lk2, norm_g, lambda_init):
    B, S, _ = dq.shape
    H = DIFF_HEADS
    tq = min(DIFF_TQ, S)
    tk = min(DIFF_TK, S)
    slopes = jnp.exp2(-8.0 * jnp.arange(1, H + 1, dtype=F32) / H) * LOG2E
    lam = (jnp.exp(jnp.sum(lq1.astype(F32) * lk1.astype(F32)))
           - jnp.exp(jnp.sum(lq2.astype(F32) * lk2.astype(F32))) + lambda_init).reshape(1)
    npos = DIFF_POS_PIECES
    pad = LANES - 2 * DIFF_DH - npos
    q4 = dq.astype(BF16).reshape(B, S, H, 2 * DIFF_DH).transpose(0, 2, 1, 3)
    q_aug = jnp.concatenate([q4, jnp.ones((B, H, S, npos), BF16), jnp.zeros((B, H, S, pad), BF16)], axis=-1)
    k4 = dk.astype(BF16).reshape(B, S, H, 2 * DIFF_DH).transpose(0, 2, 1, 3)
    kbias = slopes[:, None] * jnp.arange(S, dtype=F32)[None, :]
    kpos = jnp.stack(_split_trunc_bf16(kbias, npos), axis=-1)[None]
    k_aug = jnp.concatenate([k4, jnp.broadcast_to(kpos, (B, H, S, npos)), jnp.zeros((B, H, S, pad), BF16)],
                            axis=-1)
    v4 = dv.astype(BF16).reshape(B, S, H, DIFF_DV).transpose(0, 2, 3, 1)
    vt = jnp.concatenate([v4, jnp.ones((B, H, 1, S), BF16),
                          jnp.zeros((B, H, DIFF_VT_ROWS - DIFF_DV - 1, S), BF16)], axis=2)
    g = norm_g.astype(F32).reshape(DIFF_DV, 1)

    kern = functools.partial(_diff_attn_kernel, tq=tq, tk=tk, seq=S, out_scale=1.0 - lambda_init)
    out = pl.pallas_call(
        kern,
        grid=(B, H, S // tq),
        in_specs=[
            pl.BlockSpec(memory_space=pltpu.SMEM),
            pl.BlockSpec(memory_space=pltpu.SMEM),
            pl.BlockSpec(memory_space=pltpu.SMEM),
            pl.BlockSpec((1, 1, tq, LANES), lambda b, h, i: (b, h, i, 0)),
            pl.BlockSpec((1, 1, S, LANES), lambda b, h, i: (b, h, 0, 0)),
            pl.BlockSpec((1, 1, DIFF_VT_ROWS, S), lambda b, h, i: (b, h, 0, 0)),
            pl.BlockSpec((DIFF_DV, 1), lambda b, h, i: (0, 0)),
        ],
        out_specs=pl.BlockSpec((1, 1, DIFF_DV, tq), lambda b, h, i: (b, h, 0, i)),
        out_shape=jax.ShapeDtypeStruct((B, H, DIFF_DV, S), F32),
        scratch_shapes=[
            pltpu.VMEM((2, 2 * tq, LANES), BF16),
            pltpu.VMEM((2, 1, 2 * tq), F32),
            pltpu.VMEM((2, tk, 2 * tq), F32),
            pltpu.VMEM((2, 1, 2 * tq), F32),
            pltpu.VMEM((2, 1, 2 * tq), F32),
            pltpu.VMEM((2, tk, 2 * tq), BF16),
            pltpu.VMEM((2, 1, 2 * tq), F32),
            pltpu.VMEM((1, 2 * tq), F32),
            pltpu.VMEM((DIFF_VT_ROWS, 2 * tq), F32),
            pltpu.VMEM((8, LANES), F32),
        ],
        compiler_params=pltpu.CompilerParams(
            dimension_semantics=("parallel", "parallel", "arbitrary"),
            vmem_limit_bytes=VMEM_LIMIT),
        name="diff_attn",
    )(slopes, 1.0 / slopes, lam, q_aug, k_aug, vt, g)
    return out.transpose(0, 3, 1, 2).reshape(B, S, DIFF_V_W)


MERGE_TM = 256


def _group_rms(o, ones_bd, g, width):
    ss = _dot_split(o * o, ones_bd, n=2)
    return o * lax.rsqrt(ss * (1.0 / width) + RMS_EPS) * g


def _gelu_tanh(x):
    return 0.5 * x * (1.0 + jnp.tanh(0.7978845608028654 * (x + 0.044715 * (x * x * x))))


def _merge_kernel(x_ref, gla_ref, gog_ref, lru_ref, lg_ref, dif_ref, gdn_ref, nz_ref,
                  mg_ref, glag_ref, gdng_ref, ones_ref, mw_ref, mb_ref, up_ref, wo_ref, o_ref):
    x = x_ref[...]
    h = _rms_rows(x, mg_ref[...]).astype(BF16)
    ones_bd = ones_ref[...]
    y_gla = _group_rms(gla_ref[0] + gla_ref[1], ones_bd, glag_ref[...], GLA_DV) * _silu(gog_ref[...].astype(F32))
    y_lru = (lru_ref[0] + lru_ref[1]) * _gelu_tanh(lg_ref[...].astype(F32))
    y_dif = dif_ref[...]
    y_gdn = _group_rms(gdn_ref[0] + gdn_ref[1], ones_bd, gdng_ref[...], GDN_DV) * _silu(nz_ref[...].astype(F32))
    merged = None
    for i, y in enumerate((y_gla, y_lru, y_dif, y_gdn)):
        gate = _sigmoid(jnp.dot(h, mw_ref[i], preferred_element_type=F32) + mb_ref[i])
        term = gate * jnp.dot(y.astype(BF16), up_ref[i], preferred_element_type=F32)
        merged = term if merged is None else merged + term
    o_ref[...] = x + jnp.dot(merged.astype(BF16), wo_ref[...], preferred_element_type=F32)


def merge_pallas(xt, gla_o, gog, lru_h, lg, y_diff, gdn_o, nz, mix_g, gla_g, gdn_g, merge_w, merge_b, branch_up,
                 mix_out):
    T, D = xt.shape
    tm = min(MERGE_TM, T)
    W = BRANCH_WIDTH
    r = np.arange(W) // GLA_DV
    ones_bd = jnp.asarray((r[:, None] == r[None, :]).astype(np.float32)).astype(BF16)
    tok = lambda w: pl.BlockSpec((tm, w), lambda i: (i, 0))
    tok2 = pl.BlockSpec((2, tm, W), lambda i: (0, i, 0))
    const = lambda shape: pl.BlockSpec(shape, lambda i: (0,) * len(shape), pipeline_mode=pl.Buffered(1))
    return pl.pallas_call(
        _merge_kernel,
        grid=(T // tm,),
        in_specs=[tok(D), tok2, tok(W), tok2, tok(W), tok(W), tok2, tok(W),
                  const((1, D)), const((1, W)), const((1, W)), const((W, W)),
                  const((N_BRANCHES, D, D)), const((N_BRANCHES, 1, D)), const((N_BRANCHES, W, D)), const((D, D))],
        out_specs=tok(D),
        out_shape=jax.ShapeDtypeStruct((T, D), F32),
        compiler_params=_cparams("parallel"),
        name="merge",
    )(xt, gla_o, gog, lru_h, lg, y_diff, gdn_o, nz,
      mix_g.reshape(1, D).astype(F32), jnp.tile(gla_g.astype(F32), GLA_HEADS).reshape(1, W),
      jnp.tile(gdn_g.astype(F32), GDN_HEADS).reshape(1, W), ones_bd,
      merge_w.astype(BF16), merge_b.astype(F32).reshape(N_BRANCHES, 1, D), branch_up.astype(BF16),
      mix_out.astype(BF16))


XATTN_TM = 512


def _kv_kernel(m_ref, g_ref, w_ref, o_ref):
    mn = _rms_rows(m_ref[0], g_ref[...]).astype(BF16)
    o_ref[0] = jnp.dot(mn, w_ref[...], preferred_element_type=F32).astype(BF16)


def _xattn_kernel(x_ref, g_ref, wq_ref, kv_ref, wo_ref, o_ref):
    x = x_ref[0]
    h = _rms_rows(x, g_ref[...]).astype(BF16)
    q = jnp.dot(h, wq_ref[...], preferred_element_type=F32).astype(BF16)
    outs = []
    for hh in range(XATTN_HEADS):
        lo = hh * XATTN_DH
        k = kv_ref[0, :, lo:lo + XATTN_DH]
        v = kv_ref[0, :, D_MODEL + lo:D_MODEL + lo + XATTN_DH]
        s = _dot_nt(q[:, lo:lo + XATTN_DH], k) * XATTN_DH ** -0.5
        p = jnp.exp(s - jnp.max(s, axis=-1, keepdims=True))
        p = p / jnp.sum(p, axis=-1, keepdims=True)
        outs.append(jnp.dot(p.astype(BF16), v, preferred_element_type=F32).astype(BF16))
    o = jnp.concatenate(outs, axis=-1)
    o_ref[0] = x + jnp.dot(o, wo_ref[...], preferred_element_type=F32)


def xattn_pallas(x, mem, xg, mg, wq, wkv, wo):
    B, S, D = x.shape
    M = mem.shape[1]
    tm = min(XATTN_TM, S)
    kv = pl.pallas_call(
        _kv_kernel,
        grid=(B,),
        in_specs=[pl.BlockSpec((1, M, D), lambda b: (b, 0, 0)),
                  pl.BlockSpec((1, D), lambda b: (0, 0)),
                  pl.BlockSpec((D, 2 * D), lambda b: (0, 0))],
        out_specs=pl.BlockSpec((1, M, 2 * D), lambda b: (b, 0, 0)),
        out_shape=jax.ShapeDtypeStruct((B, M, 2 * D), BF16),
        compiler_params=_cparams("parallel"),
        name="xattn_kv",
    )(mem, mg.reshape(1, D).astype(F32), wkv.astype(BF16))
    const = lambda shape: pl.BlockSpec(shape, lambda b, i: (0,) * len(shape), pipeline_mode=pl.Buffered(1))
    return pl.pallas_call(
        _xattn_kernel,
        grid=(B, S // tm),
        in_specs=[pl.BlockSpec((1, tm, D), lambda b, i: (b, i, 0)),
                  const((1, D)), const((D, D)),
                  pl.BlockSpec((1, M, 2 * D), lambda b, i: (b, 0, 0)),
                  const((D, D))],
        out_specs=pl.BlockSpec((1, tm, D), lambda b, i: (b, i, 0)),
        out_shape=jax.ShapeDtypeStruct((B, S, D), F32),
        compiler_params=_cparams("parallel", "parallel"),
        name="xattn",
    )(x, xg.reshape(1, D).astype(F32), wq.astype(BF16), kv, wo.astype(BF16))


ROUTE_TM = 512
ROUTE_EXPERT_LANE = 32


PAIR_BM = 256
PAIR_DMA_GROUP = 16
PAIRS_PER_GROUP = EXPERTS_PER_GROUP * (EXPERTS_PER_GROUP - 1) // 2
N_CLASSES = N_GROUPS * PAIRS_PER_GROUP


def _route_class_kernel(x_ref, g_ref, w_ref, b_ref, r_ref):
    h = _rms_rows(x_ref[...], g_ref[...])
    logits = _dot_f32(h, w_ref[...]) + b_ref[...]
    lane = lax.broadcasted_iota(jnp.int32, logits.shape, 1)
    big = jnp.int32(1 << 20)
    neg = jnp.float32(-jnp.inf)

    def top(vals):
        m = jnp.max(vals, axis=-1, keepdims=True)
        idx = jnp.min(jnp.where(vals == m, lane, big), axis=-1, keepdims=True)
        return m, idx

    _, gidx = top(jnp.where(lane < N_GROUPS, logits, neg))
    lo0 = ROUTE_EXPERT_LANE + EXPERTS_PER_GROUP * gidx
    in_grp = jnp.abs(2 * (lane - lo0) - (EXPERTS_PER_GROUP - 1)) < EXPERTS_PER_GROUP
    el = jnp.where(in_grp, logits, neg)
    _, i1 = top(el)
    _, i2 = top(jnp.where(lane == i1, neg, el))
    a = jnp.minimum(i1, i2) - lo0
    b = jnp.maximum(i1, i2) - lo0
    pair = a * (2 * EXPERTS_PER_GROUP - 1 - a) // 2 + (b - a - 1)
    r_ref[...] = jnp.broadcast_to(gidx * PAIRS_PER_GROUP + pair, r_ref.shape)


def _pair_expert_kernel(lo_ref, hi_ref, nv_ref, idx_hbm, x_hbm, g_ref, wr_ref, br_ref, fg_ref,
                        w1a_ref, w3a_ref, w2a_ref, w1b_ref, w3b_ref, w2b_ref, o_hbm,
                        idx_smem, xbuf, obuf, sem_idx, sem_in, sem_out, *, bm, final):
    i = pl.program_id(0)
    n = nv_ref[i]

    @pl.when(i == 0)
    def _():
        xbuf[...] = jnp.zeros(xbuf.shape, F32)

    @pl.when(n > 0)
    def _():
        cp = pltpu.make_async_copy(idx_hbm.at[i], idx_smem, sem_idx)
        cp.start()
        cp.wait()

        def in_copy(r):
            return pltpu.make_async_copy(x_hbm.at[pl.ds(idx_smem[r], 1)], xbuf.at[pl.ds(r, 1)], sem_in)

        def out_copy(r):
            return pltpu.make_async_copy(obuf.at[pl.ds(r, 1)], o_hbm.at[pl.ds(idx_smem[r], 1)], sem_out)

        def loop(copy_of, wait):
            def act(r, u):
                cpy = copy_of(r)
                cpy.wait() if wait else cpy.start(priority=u % 2)

            def group(gi, c):
                for u in range(PAIR_DMA_GROUP):
                    act(gi * PAIR_DMA_GROUP + u, u)
                return c

            def single(r, c):
                act(r, 0)
                return c

            n_groups = n // PAIR_DMA_GROUP
            lax.fori_loop(0, n_groups, group, 0)
            lax.fori_loop(n_groups * PAIR_DMA_GROUP, n, single, 0)

        loop(in_copy, False)
        loop(in_copy, True)
        x = xbuf[...]
        h = _rms_rows(x, g_ref[...])
        logits = _dot_f32(h, wr_ref[...]) + br_ref[...]
        lane = lax.broadcasted_iota(jnp.int32, logits.shape, 1)
        lo = lo_ref[i]
        hi = hi_ref[i]
        grp = lo // EXPERTS_PER_GROUP
        is_g = lane < N_GROUPS
        gmax = jnp.max(jnp.where(is_g, logits, -jnp.inf), axis=-1, keepdims=True)
        eg = jnp.exp(logits - gmax)
        g_w = (jnp.sum(jnp.where(lane == grp, eg, 0.0), axis=-1, keepdims=True)
               / jnp.sum(jnp.where(is_g, eg, 0.0), axis=-1, keepdims=True))
        e_lo = jnp.sum(jnp.where(lane == ROUTE_EXPERT_LANE + lo, logits, 0.0), axis=-1, keepdims=True)
        e_hi = jnp.sum(jnp.where(lane == ROUTE_EXPERT_LANE + hi, logits, 0.0), axis=-1, keepdims=True)
        m = jnp.maximum(e_lo, e_hi)
        t_lo = jnp.exp(e_lo - m)
        t_hi = jnp.exp(e_hi - m)
        inv = g_w / (t_lo + t_hi)
        hb = h.astype(BF16)

        def expert(w1, w3, w2):
            a = jnp.dot(hb, w1[0], preferred_element_type=F32)
            b = jnp.dot(hb, w3[0], preferred_element_type=F32)
            return jnp.dot((_silu(a) * b).astype(BF16), w2[0], preferred_element_type=F32)

        y = expert(w1a_ref, w3a_ref, w2a_ref) * (t_lo * inv) + expert(w1b_ref, w3b_ref, w2b_ref) * (t_hi * inv)
        out = x + y
        if final:
            out = _rms_rows(out, fg_ref[...])
        obuf[...] = out
        loop(out_copy, False)
        loop(out_copy, True)


def moe_pair_pallas(xt, g, w_group, b_group, w_expert, b_expert, w1, w3, w2, final_g=None):
    T, D = xt.shape
    bm = PAIR_BM
    tm = min(ROUTE_TM, T)
    wr = jnp.zeros((D, LANES), F32).at[:, :N_GROUPS].set(w_group.astype(F32))
    wr = wr.at[:, ROUTE_EXPERT_LANE:ROUTE_EXPERT_LANE + N_EXPERTS].set(w_expert.astype(F32))
    br = jnp.zeros((1, LANES), F32).at[0, :N_GROUPS].set(b_group.astype(F32))
    br = br.at[0, ROUTE_EXPERT_LANE:ROUTE_EXPERT_LANE + N_EXPERTS].set(b_expert.astype(F32))
    g2 = g.reshape(1, D).astype(F32)
    cls = pl.pallas_call(
        _route_class_kernel,
        grid=(T // tm,),
        in_specs=[pl.BlockSpec((tm, D), lambda i: (i, 0)),
                  pl.BlockSpec((1, D), lambda i: (0, 0)),
                  pl.BlockSpec((D, LANES), lambda i: (0, 0)),
                  pl.BlockSpec((1, LANES), lambda i: (0, 0))],
        out_specs=pl.BlockSpec((tm, LANES), lambda i: (i, 0)),
        out_shape=jax.ShapeDtypeStruct((T, LANES), jnp.int32),
        compiler_params=_cparams("parallel"),
        name="moe_route",
    )(xt, g2, wr, br)[:, 0]

    n_blk = T // bm + N_CLASSES
    order = jnp.argsort(cls).astype(jnp.int32)
    cls_s = cls[order]
    bounds = jnp.searchsorted(cls_s, jnp.arange(N_CLASSES + 1, dtype=jnp.int32), side='left').astype(jnp.int32)
    seg_start = bounds[:-1]
    counts = bounds[1:] - seg_start
    nblk_c = (counts + bm - 1) // bm
    blk_end = jnp.cumsum(nblk_c)
    blk_start = blk_end - nblk_c
    bidx = jnp.arange(n_blk, dtype=jnp.int32)
    blk_cls = jnp.minimum(jnp.searchsorted(blk_end, bidx, side='right'), N_CLASSES - 1).astype(jnp.int32)
    first = seg_start[blk_cls] + (bidx - blk_start[blk_cls]) * bm
    n_valid = jnp.clip(seg_start[blk_cls] + counts[blk_cls] - first, 0, bm)
    n_valid = jnp.where(bidx < blk_end[-1], n_valid, 0).astype(jnp.int32)
    src = jnp.clip(first[:, None] + jnp.arange(bm, dtype=jnp.int32)[None, :], 0, T - 1)
    row_tok = order[src]
    pa, pb = np.triu_indices(EXPERTS_PER_GROUP, k=1)
    grp_of = np.repeat(np.arange(N_GROUPS), PAIRS_PER_GROUP) * EXPERTS_PER_GROUP
    lo_tab = jnp.asarray((grp_of + np.tile(pa, N_GROUPS)).astype(np.int32))
    hi_tab = jnp.asarray((grp_of + np.tile(pb, N_GROUPS)).astype(np.int32))
    blk_lo = lo_tab[blk_cls]
    blk_hi = hi_tab[blk_cls]

    fg = (final_g if final_g is not None else jnp.ones((D,), F32)).reshape(1, D).astype(F32)
    w1b, w3b, w2b = w1.astype(BF16), w3.astype(BF16), w2.astype(BF16)
    const = lambda shape: pl.BlockSpec(shape, lambda i, lo, hi, nv: (0,) * len(shape))
    wspec = lambda shape, which: pl.BlockSpec(
        shape, (lambda i, lo, hi, nv: (lo[i], 0, 0)) if which == 0 else (lambda i, lo, hi, nv: (hi[i], 0, 0)))
    return pl.pallas_call(
        functools.partial(_pair_expert_kernel, bm=bm, final=final_g is not None),
        grid_spec=pltpu.PrefetchScalarGridSpec(
            num_scalar_prefetch=3,
            grid=(n_blk,),
            in_specs=[pl.BlockSpec(memory_space=pl.ANY),
                      pl.BlockSpec(memory_space=pl.ANY),
                      const((1, D)), const((D, LANES)), const((1, LANES)), const((1, D)),
                      wspec((1, D, D_EXPERT), 0), wspec((1, D, D_EXPERT), 0), wspec((1, D_EXPERT, D), 0),
                      wspec((1, D, D_EXPERT), 1), wspec((1, D, D_EXPERT), 1), wspec((1, D_EXPERT, D), 1)],
            out_specs=pl.BlockSpec(memory_space=pl.ANY),
            scratch_shapes=[pltpu.SMEM((bm,), jnp.int32), pltpu.VMEM((bm, D), F32), pltpu.VMEM((bm, D), F32),
                            pltpu.SemaphoreType.DMA(()), pltpu.SemaphoreType.DMA(()), pltpu.SemaphoreType.DMA(())]),
        out_shape=jax.ShapeDtypeStruct((T, D), F32),
        compiler_params=_cparams("arbitrary"),
        name="moe_pair_experts",
    )(blk_lo, blk_hi, n_valid, row_tok, xt, g2, wr, br, fg, w1b, w3b, w2b, w1b, w3b, w2b)


GDN_TB = 512
GDN_UNROLL_B = 8


def _gdn_prep_kernel(prev_ref, cur_ref, next_ref, ba_ref, cw_ref, ones_ref, alog_ref, dt_ref, qkv_ref, bg_ref, *, nt):
    qkv = _silu(_conv4(prev_ref, cur_ref, next_ref, cw_ref[...], nt))
    ones_bd = ones_ref[...]
    w = GDN_QK_W
    q = qkv[:, 0:w]
    k = qkv[:, w:2 * w]
    qn = q * lax.rsqrt(_dot_split(q * q, ones_bd, n=2) + 1e-6) * GDN_DK ** -0.5
    kn = k * lax.rsqrt(_dot_split(k * k, ones_bd, n=2) + 1e-6)
    qkv_ref[0] = jnp.concatenate([qn, kn, qkv[:, 2 * w:]], axis=-1)
    ba = ba_ref[0]
    beta = _sigmoid(ba)
    log_a = -jnp.exp(alog_ref[...]) * _softplus(ba + dt_ref[...])
    lane = lax.broadcasted_iota(jnp.int32, ba.shape, 1)
    h = GDN_HEADS
    for d in range(2):
        b_d = pltpu.roll(beta, (LANES - d * h) % LANES, 1)
        a_d = pltpu.roll(log_a, (LANES - (2 * h + d * h) + h) % LANES, 1)
        bg_ref[d, 0] = jnp.where(lane < h, b_d, jnp.where(lane < 2 * h, a_d, 0.0))


def _gdn_chunk_compact_kernel(qkv_ref, bg_ref, tri_ref, tril_ref, sel_ref, o_ref, st_ref, *, nb):
    c = GDN_CHUNK
    H = GDN_HEADS
    n = H * c

    @pl.when(pl.program_id(1) == 0)
    def _():
        st_ref[...] = jnp.zeros(st_ref.shape, F32)

    tri_bf = tri_ref[0].astype(BF16)
    incl = tril_ref[0]
    ri = lax.broadcasted_iota(jnp.int32, (c, n), 0)
    ci = lax.broadcasted_iota(jnp.int32, (c, n), 1)
    eye = jnp.where(ri == ci % c, 1.0, 0.0)
    strict = incl - eye
    rb = lax.broadcasted_iota(jnp.int32, (n, n), 0) // c
    cb = lax.broadcasted_iota(jnp.int32, (n, n), 1) // c
    bdmask = rb == cb
    sel_g = sel_ref[0]
    sel_b = sel_ref[1]

    def bd(x):
        return jnp.where(bdmask, jnp.concatenate([x] * H, axis=0), 0.0)

    def one_batch(b):
        qkv = qkv_ref[b]
        bg = bg_ref[0, b]
        q = qkv[:, 0:n]
        k = qkv[:, n:2 * n]
        v = qkv[:, 2 * n:]
        gam = _cumsum_rows(tri_bf, bg)
        k_bf = k.astype(BF16)
        kq = _dot_nt(jnp.concatenate([k_bf, q.astype(BF16)], axis=0), bd(k).astype(BF16))
        bexp = _dot_split(bg, sel_b)
        yield
        tot = jnp.sum(bg, axis=0, keepdims=True)
        gt = jnp.concatenate([gam, jnp.broadcast_to(tot, (8, LANES))], axis=0)
        ge = _dot_split(gt, sel_g)
        yield
        gexp = ge[0:c]
        end_row = ge[c:c + 1]
        gam_t = jnp.concatenate([gam, jnp.zeros_like(gam)], axis=0).T
        g_row = jnp.concatenate([gam_t[H + hh:H + hh + 1, 0:c] for hh in range(H)], axis=1)
        decay = jnp.exp(jnp.minimum(gexp - g_row, 0.0)) * incl
        exp_g = jnp.exp(gexp)
        a_c = strict * bexp * kq[0:c] * decay
        qk = kq[c:2 * c] * decay
        p_c = eye - a_c
        x_c = _dot_nn(a_c.astype(BF16), bd(a_c).astype(BF16))
        yield
        n_sq = int(math.log2(c)) - 1
        for it in range(n_sq):
            r_bd = bd(x_c).astype(BF16)
            if it + 1 < n_sq:
                both = _dot_nn(jnp.concatenate([x_c, p_c], axis=0).astype(BF16), r_bd)
                yield
                x_c = both[0:c]
                p_c = p_c + both[c:2 * c]
            else:
                dp = _dot_nn(p_c.astype(BF16), r_bd)
                yield
                p_c = p_c + dp
        ap = _dot_f32(a_c, bd(p_c))
        yield
        resid = eye - p_c - ap
        p_bf = p_c.astype(BF16)
        dp = _dot_nn(p_bf, bd(resid).astype(BF16))
        yield
        p_c = p_c + dp
        p_bf = p_c.astype(BF16)
        u = _dot_nn(p_bf, bd(v * bexp).astype(BF16))
        kc = _dot_nn(p_bf, bd(k * (bexp * exp_g)).astype(BF16))
        yield
        st = st_ref[b]
        st_bf = st.astype(BF16)
        sq = _dot_nn(jnp.concatenate([kc, q * exp_g], axis=0).astype(BF16), st_bf)
        yield
        v_new = u - sq[0:c]
        o_ref[0, b] = sq[c:2 * c] + _dot_nn(qk.astype(BF16), bd(v_new).astype(BF16))
        kd = (k * jnp.exp(end_row - gexp)).astype(BF16)
        upd = _dot_tn(kd, v_new.astype(BF16))
        st_ref[b] = st * jnp.exp(end_row) + jnp.where(bdmask, upd, 0.0)

    n_side = math.gcd(nb, GDN_UNROLL_B)

    def group(i, carry):
        chains = [one_batch(i * n_side + j) for j in range(n_side)]
        live = True
        while live:
            live = False
            for ch in chains:
                try:
                    next(ch)
                    live = True
                except StopIteration:
                    pass
        return carry

    if nb == n_side:
        group(0, 0)
    else:
        lax.fori_loop(0, nb // n_side, group, 0)


def gdn_pallas(nqkv, nba, conv_w, a_log, dt_bias):
    B, S, W = nqkv.shape
    tb = min(GDN_TB, S)
    nt = S // tb
    c = GDN_CHUNK
    nc = S // c
    H = GDN_HEADS
    r = np.arange(GDN_QK_W) // GDN_DK
    ones_bd = jnp.asarray((r[:, None] == r[None, :]).astype(np.float32)).astype(BF16)
    alog = jnp.zeros((1, LANES), F32).at[0, 2 * H:4 * H].set(a_log.astype(F32).reshape(-1))
    dt = jnp.zeros((1, LANES), F32).at[0, 2 * H:4 * H].set(dt_bias.astype(F32).reshape(-1))
    full = lambda shape: pl.BlockSpec(shape, lambda b, i: (0,) * len(shape))
    qkvn, bg = pl.pallas_call(
        functools.partial(_gdn_prep_kernel, nt=nt),
        grid=(B, nt),
        in_specs=_halo_specs(tb, W, nt) + [pl.BlockSpec((1, tb, LANES), lambda b, i: (b, i, 0)),
                                           full((CONV_WIDTH, W)), full((GDN_QK_W, GDN_QK_W)),
                                           full((1, LANES)), full((1, LANES))],
        out_specs=[pl.BlockSpec((1, tb, W), lambda b, i: (b, i, 0)),
                   pl.BlockSpec((2, 1, tb, LANES), lambda b, i: (0, b, i, 0))],
        out_shape=[jax.ShapeDtypeStruct((B, S, W), F32), jax.ShapeDtypeStruct((2, B, S, LANES), F32)],
        compiler_params=_cparams("parallel", "parallel"),
        name="gdn_prep",
    )(nqkv, nqkv, nqkv, nba, conv_w.astype(F32), ones_bd, alog, dt)

    t = np.arange(c)
    tri_np = np.stack([t[:, None] >= t[None, :], t[:, None] <= t[None, :]]).astype(np.float32)
    tri = jnp.asarray(tri_np)
    tri_lanes = jnp.asarray(np.tile(tri_np, (1, 1, H)))
    lane = np.arange(LANES)[:, None]
    head = (np.arange(H * c) // c)[None, :]
    sel = jnp.asarray(np.stack([lane == H + head, lane == head]).astype(np.float32)).astype(BF16)

    def tmap(d, i):
        return i + d * (nc - 1 - 2 * i)

    return pl.pallas_call(
        functools.partial(_gdn_chunk_compact_kernel, nb=B),
        grid=(2, nc),
        in_specs=[pl.BlockSpec((B, c, W), lambda d, i: (0, tmap(d, i), 0)),
                  pl.BlockSpec((1, B, c, LANES), lambda d, i: (d, 0, tmap(d, i), 0)),
                  pl.BlockSpec((1, c, c), lambda d, i: (d, 0, 0)),
                  pl.BlockSpec((1, c, H * c), lambda d, i: (d, 0, 0)),
                  pl.BlockSpec((2, LANES, H * c), lambda d, i: (0, 0, 0))],
        out_specs=pl.BlockSpec((1, B, c, GDN_V_W), lambda d, i: (d, 0, tmap(d, i), 0)),
        out_shape=jax.ShapeDtypeStruct((2, B, S, GDN_V_W), F32),
        scratch_shapes=[pltpu.VMEM((B, H * GDN_DK, H * GDN_DV), F32)],
        compiler_params=_cparams("arbitrary", "arbitrary"),
        name="gdn_chunk",
    )(qkvn, bg, tri, tri_lanes, sel)


def kernel(x, mem, mix_norm, w_in, gla_lr_up, gla_lr_bias, gla_norm, lru_conv_w, lru_conv_b, lru_w_a, lru_b_a, lru_w_x, lru_b_x, lru_lambda, diff_lq1, diff_lk1, diff_lq2, diff_lk2, diff_norm, gdn_conv_w, gdn_a_log, gdn_dt_bias, gdn_norm, merge_w, merge_b, branch_up, mix_out, xattn_norm, mem_norm, xattn_wq, xattn_wkv, xattn_wo, moe_norm, moe_w_group, moe_b_group, moe_w_expert, moe_b_expert, moe_w1, moe_w3, moe_w2, final_norm):
    B, S, D = x.shape
    T = B * S
    xt = x.reshape(T, D)
    for l in range(DEPTH):
        gla_in, gog, lx, lg, dq, dk, dv, nqkv, nz, nba = inproj_pallas(xt, mix_norm[l], w_in[l])
        gla_o = gla_pallas(gla_in.reshape(B, S, -1), gla_lr_up[l], gla_lr_bias[l])
        lru_h = lru_pallas(lx.reshape(B, S, -1), lru_conv_w[l], lru_conv_b[l], lru_w_a[l], lru_b_a[l],
                           lru_w_x[l], lru_b_x[l], lru_lambda[l])
        y_diff = diff_attention_pallas(dq.reshape(B, S, -1), dk.reshape(B, S, -1), dv.reshape(B, S, -1),
                                       diff_lq1[l], diff_lk1[l], diff_lq2[l], diff_lk2[l], diff_norm[l],
                                       0.8 - 0.6 * math.exp(-0.3 * l))
        gdn_o = gdn_pallas(nqkv.reshape(B, S, -1), nba.reshape(B, S, -1), gdn_conv_w[l], gdn_a_log[l],
                           gdn_dt_bias[l])
        xt = merge_pallas(xt, gla_o.reshape(2, T, -1), gog, lru_h.reshape(2, T, -1), lg, y_diff.reshape(T, -1),
                          gdn_o.reshape(2, T, -1), nz, mix_norm[l], gla_norm[l], gdn_norm[l],
                          merge_w[l], merge_b[l], branch_up[l], mix_out[l])
        xt = xattn_pallas(xt.reshape(B, S, D), mem, xattn_norm[l], mem_norm[l], xattn_wq[l], xattn_wkv[l],
                          xattn_wo[l]).reshape(T, D)
        xt = moe_pair_pallas(xt, moe_norm[l], moe_w_group[l], moe_b_group[l], moe_w_expert[l], moe_b_expert[l],
                             moe_w1[l], moe_w3[l], moe_w2[l], final_g=final_norm if l == DEPTH - 1 else None)
    return xt.reshape(B, S, D)
```
